```python
import math
import jax
import jax.numpy as jnp
from jax import lax
import numpy as np

D_MODEL = 1024
BATCH = 2
SEQ = 8192
DEPTH = 2
DEC_BATCH = 32
DEC_SEQ = 1
PAST_LEN = 16384
PAGE_SIZE = 128

HEAD_DIM = 64
N_DIFF_HEADS = 4
DIFF_HALF = HEAD_DIM // 2
N_MOBA_HEADS = 4
N_SOFT_HEADS = N_DIFF_HEADS + N_MOBA_HEADS
N_DN_HEADS = 8
DN_DIM = 64
DIFF_W = N_DIFF_HEADS * HEAD_DIM
MOBA_W = N_MOBA_HEADS * HEAD_DIM
DN_W = N_DN_HEADS * DN_DIM
MIX_W = DIFF_W + MOBA_W + DN_W
IN_SPLITS = (DIFF_W, DIFF_W, DIFF_W, MOBA_W, MOBA_W, MOBA_W, 3 * DN_W, DN_W, N_DN_HEADS, N_DN_HEADS)
IN_W = 3 * DIFF_W + 3 * MOBA_W + 4 * DN_W + 2 * N_DN_HEADS
DN_CONV = 4
DN_CHUNK = 64
MOBA_BLOCK = 256
MOBA_TOPK = 3
MOBA_Q_BLK = 64
ATT_Q_BLK = 128
D_FF = 3584
FFN_CONV = 3
EPS = 1e-6

kernel_name = 'hymba_diff_moba_gdn_convffn_step'


def rmsnorm(x, g):
    x32 = x.astype(jnp.float32)
    y = x32 * lax.rsqrt(jnp.mean(x32 * x32, axis=-1, keepdims=True) + EPS)
    return (y * g.astype(jnp.float32)).astype(x.dtype)


def l2norm(x):
    return x * lax.rsqrt(jnp.sum(x * x, axis=-1, keepdims=True) + EPS)


def alibi_slopes():
    m = jnp.exp2(-(8.0 / N_SOFT_HEADS) * (jnp.arange(N_SOFT_HEADS, dtype=jnp.float32) + 1.0))
    return m[0::2], m[1::2]


def in_split_points():
    pts, acc = [], 0
    for w in IN_SPLITS[:-1]:
        acc += w
        pts.append(acc)
    return pts


def causal_dwconv(x, buf, w):
    width = w.shape[0]
    t = x.shape[1]
    xp = jnp.concatenate([buf.astype(x.dtype), x], axis=1)
    y = xp[:, :t] * w[0]
    for j in range(1, width):
        y = y + xp[:, j:j + t] * w[j]
    return y, xp[:, t:]


def diff_attend(q, segs, q_pos, lam, slopes):
    scale = DIFF_HALF ** -0.5
    scores = []
    for k, _, k_pos in segs:
        s = jnp.einsum('nqhcd,nkhcd->nhcqk', q, k, preferred_element_type=jnp.float32) * scale
        dist = (q_pos[:, None] - k_pos[None, :]).astype(jnp.float32)
        s = jnp.where(dist >= 0, s - slopes[:, None, None, None] * dist, -jnp.inf)
        scores.append(s)
    p = jax.nn.softmax(jnp.concatenate(scores, axis=-1), axis=-1)
    w = p[:, :, 0] - lam * p[:, :, 1]
    outs = []
    off = 0
    for _, v, k_pos in segs:
        n = k_pos.shape[0]
        outs.append(jnp.einsum('nhqk,nkhe->nqhe', w[..., off:off + n].astype(v.dtype), v,
                               preferred_element_type=jnp.float32))
        off += n
    o = outs[0]
    for extra in outs[1:]:
        o = o + extra
    return o


def diff_prompt(q, k, v, lam, slopes):
    n, t = q.shape[:2]
    nb = t // ATT_Q_BLK
    k_pos = jnp.arange(t, dtype=jnp.int32)
    qb = jnp.moveaxis(q.reshape(n, nb, ATT_Q_BLK, N_DIFF_HEADS, 2, DIFF_HALF), 1, 0)

    def one(args):
        qi, i = args
        q_pos = i * ATT_Q_BLK + jnp.arange(ATT_Q_BLK, dtype=jnp.int32)
        return diff_attend(qi, ((k, v, k_pos),), q_pos, lam, slopes)

    o = lax.map(one, (qb, jnp.arange(nb, dtype=jnp.int32)))
    return jnp.moveaxis(o, 0, 1).reshape(n, t, N_DIFF_HEADS, HEAD_DIM)


def to_blocks(x):
    n, length = x.shape[:2]
    pad = (-length) % MOBA_BLOCK
    x = jnp.pad(x, ((0, 0), (0, pad), (0, 0), (0, 0)))
    return x.reshape(n, (length + pad) // MOBA_BLOCK, MOBA_BLOCK, x.shape[2], x.shape[3])


def moba_attend(q, kblk, vblk, kmean, q_pos, slopes):
    n, tq, h, d = q.shape
    nb = kblk.shape[1]
    own = q_pos // MOBA_BLOCK
    gate = jnp.einsum('nqhd,nbhd->nhqb', q.astype(jnp.float32), kmean)
    fully_past = jnp.arange(nb, dtype=jnp.int32)[None, :] < own[:, None]
    gate = jnp.where(fully_past, gate, -jnp.inf)
    n_sel = min(MOBA_TOPK, nb)
    _, top = lax.top_k(gate, n_sel)
    blocks = jnp.concatenate([top.astype(jnp.int32),
                              jnp.broadcast_to(own[None, None, :, None], (n, h, tq, 1))], axis=-1)
    blk_ok = jnp.concatenate([jnp.arange(n_sel, dtype=jnp.int32)[None, :] < own[:, None],
                              jnp.ones((tq, 1), dtype=bool)], axis=-1)
    n_idx = jnp.arange(n)[:, None, None, None]
    h_idx = jnp.arange(h)[None, :, None, None]
    kg = kblk[n_idx, blocks, :, h_idx]
    vg = vblk[n_idx, blocks, :, h_idx]
    k_pos = blocks[..., None] * MOBA_BLOCK + jnp.arange(MOBA_BLOCK, dtype=jnp.int32)
    dist = q_pos[None, None, :, None, None] - k_pos
    valid = blk_ok[None, None, :, :, None] & (dist >= 0)
    s = jnp.einsum('nqhd,nhqsbd->nhqsb', q, kg, preferred_element_type=jnp.float32) * (d ** -0.5)
    s = jnp.where(valid, s - slopes[None, :, None, None, None] * dist.astype(jnp.float32), -jnp.inf)
    p = jax.nn.softmax(s.reshape(n, h, tq, -1), axis=-1).reshape(s.shape)
    return jnp.einsum('nhqsb,nhqsbd->nqhd', p.astype(vg.dtype), vg, preferred_element_type=jnp.float32)


def moba_prompt(q, k, v, slopes):
    n, t = q.shape[:2]
    kblk, vblk = to_blocks(k), to_blocks(v)
    kmean = jnp.mean(kblk, axis=2, dtype=jnp.float32)
    nq = t // MOBA_Q_BLK
    qb = jnp.moveaxis(q.reshape(n, nq, MOBA_Q_BLK, N_MOBA_HEADS, HEAD_DIM), 1, 0)

    def one(args):
        qi, i = args
        q_pos = i * MOBA_Q_BLK + jnp.arange(MOBA_Q_BLK, dtype=jnp.int32)
        return moba_attend(qi, kblk, vblk, kmean, q_pos, slopes)

    o = lax.map(one, (qb, jnp.arange(nq, dtype=jnp.int32)))
    return jnp.moveaxis(o, 0, 1).reshape(n, t, N_MOBA_HEADS, HEAD_DIM)


def gdn_chunked(q, k, v, g, beta, s0):
    n, t, h, _ = q.shape
    c = DN_CHUNK
    nc = t // c

    def chunks(x):
        x = x.reshape((n, nc, c, h) + x.shape[3:])
        return jnp.moveaxis(jnp.moveaxis(x, 1, 0), 2, 3)

    qc, kc, vc, bc = chunks(q), chunks(k), chunks(v), chunks(beta)
    gc = jnp.cumsum(chunks(g), axis=-1)
    idx = jnp.arange(c)
    incl = idx[:, None] >= idx[None, :]
    strict = idx[:, None] > idx[None, :]
    decay = jnp.exp(jnp.where(incl, gc[..., :, None] - gc[..., None, :], -jnp.inf))
    a = jnp.where(strict, bc[..., :, None] * jnp.einsum('znhid,znhjd->znhij', kc, kc) * decay, 0.0)
    ia = a + jnp.eye(c, dtype=a.dtype)
    u = lax.linalg.triangular_solve(ia, vc * bc[..., None], left_side=True, lower=True)
    w = lax.linalg.triangular_solve(ia, kc * (bc * jnp.exp(gc))[..., None], left_side=True, lower=True)
    qk = jnp.where(incl, jnp.einsum('znhid,znhjd->znhij', qc, kc) * decay, 0.0)

    def step(s, xs):
        qi, ki, ui, wi, gi, ai = xs
        v_new = ui - jnp.einsum('nhcd,nhde->nhce', wi, s)
        o = (jnp.einsum('nhcd,nhde->nhce', qi * jnp.exp(gi)[..., None], s)
             + jnp.einsum('nhij,nhje->nhie', ai, v_new))
        g_last = gi[..., -1:]
        s = (s * jnp.exp(g_last)[..., None]
             + jnp.einsum('nhcd,nhce->nhde', ki * jnp.exp(g_last - gi)[..., None], v_new))
        return s, o

    s_final, o = lax.scan(step, s0, (qc, kc, u, w, gc, qk))
    o = jnp.swapaxes(jnp.moveaxis(o, 0, 1), 2, 3).reshape(n, t, h, v.shape[-1])
    return o, s_final


def gdn_recurrent(q, k, v, g, beta, s0):
    def step(s, xs):
        qt, kt, vt, gt, bt = xs
        s = s * jnp.exp(gt)[..., None, None]
        delta = (vt - jnp.einsum('nhd,nhde->nhe', kt, s)) * bt[..., None]
        s = s + kt[..., :, None] * delta[..., None, :]
        return s, jnp.einsum('nhd,nhde->nhe', qt, s)

    xs = tuple(jnp.moveaxis(arr, 1, 0) for arr in (q, k, v, g, beta))
    s_final, o = lax.scan(step, s0, xs)
    return jnp.moveaxis(o, 0, 1), s_final


def gdn_mixer(qkv, zg, b_raw, a_raw, conv_buf, s0, conv_w, a_log, dt_bias, norm_g, recurrent):
    n, t, _ = qkv.shape
    cq, conv_new = causal_dwconv(qkv, conv_buf, conv_w)
    cq = jax.nn.silu(cq.astype(jnp.float32)).reshape(n, t, 3, N_DN_HEADS, DN_DIM)
    q = l2norm(cq[:, :, 0]) * (DN_DIM ** -0.5)
    k = l2norm(cq[:, :, 1])
    v = cq[:, :, 2]
    beta = jax.nn.sigmoid(b_raw.astype(jnp.float32))
    g = -jnp.exp(a_log.astype(jnp.float32)) * jax.nn.softplus(a_raw.astype(jnp.float32) + dt_bias.astype(jnp.float32))
    s0 = s0.astype(jnp.float32)
    if recurrent:
        o, s_new = gdn_recurrent(q, k, v, g, beta, s0)
    else:
        o, s_new = gdn_chunked(q, k, v, g, beta, s0)
    gate = jax.nn.silu(zg.astype(jnp.float32)).reshape(n, t, N_DN_HEADS, DN_DIM)
    o = rmsnorm(o, norm_g) * gate
    return o.reshape(n, t, DN_W).astype(qkv.dtype), s_new, conv_new


def decoder_layer(x, p, lam, lam_init, slopes_a, slopes_b, past):
    n, t, _ = x.shape
    h = rmsnorm(x, p['norm1'])
    z = jnp.einsum('ntd,de->nte', h, p['w_in'])
    qa, ka, va, qb, kb, vb, qkv, zg, b_raw, a_raw = jnp.split(z, in_split_points(), axis=-1)
    qa = qa.reshape(n, t, N_DIFF_HEADS, 2, DIFF_HALF)
    ka = ka.reshape(n, t, N_DIFF_HEADS, 2, DIFF_HALF)
    va = va.reshape(n, t, N_DIFF_HEADS, HEAD_DIM)
    qb = qb.reshape(n, t, N_MOBA_HEADS, HEAD_DIM)
    kb = kb.reshape(n, t, N_MOBA_HEADS, HEAD_DIM)
    vb = vb.reshape(n, t, N_MOBA_HEADS, HEAD_DIM)
    if past is None:
        o_a = diff_prompt(qa, ka, va, lam, slopes_a)
        o_b = moba_prompt(qb, kb, vb, slopes_b)
        s0 = jnp.zeros((n, N_DN_HEADS, DN_DIM, DN_DIM), jnp.float32)
        conv_buf = jnp.zeros((n, DN_CONV - 1, 3 * DN_W), x.dtype)
        ffn_buf = jnp.zeros((n, FFN_CONV - 1, D_FF), x.dtype)
    else:
        k_pa, v_pa, k_pb, v_pb, s0, conv_buf, ffn_buf = past
        plen = k_pa.shape[1]
        q_pos = plen + jnp.arange(t, dtype=jnp.int32)
        segs = ((k_pa.reshape(n, plen, N_DIFF_HEADS, 2, DIFF_HALF), v_pa, jnp.arange(plen, dtype=jnp.int32)),
                (ka, va, q_pos))
        o_a = diff_attend(qa, segs, q_pos, lam, slopes_a)
        kblk = to_blocks(jnp.concatenate([k_pb.astype(kb.dtype), kb], axis=1))
        vblk = to_blocks(jnp.concatenate([v_pb.astype(vb.dtype), vb], axis=1))
        kmean = jnp.mean(kblk, axis=2, dtype=jnp.float32)
        o_b = moba_attend(qb, kblk, vblk, kmean, q_pos, slopes_b)
    o_a = rmsnorm(o_a, p['subln']) * (1.0 - lam_init)
    o_c, s_new, conv_new = gdn_mixer(qkv, zg, b_raw, a_raw, conv_buf, s0, p['dn_conv_w'], p['dn_a_log'],
                                     p['dn_dt_bias'], p['dn_norm'], past is not None)
    mix = jnp.concatenate([o_a.reshape(n, t, DIFF_W).astype(x.dtype),
                           o_b.reshape(n, t, MOBA_W).astype(x.dtype), o_c], axis=-1)
    x = x + jnp.einsum('nte,ed->ntd', mix, p['w_out'])
    h2 = rmsnorm(x, p['norm2'])
    gate, up = jnp.split(jnp.einsum('ntd,df->ntf', h2, p['ffn_in']), 2, axis=-1)
    gate_c, ffn_new = causal_dwconv(gate, ffn_buf, p['ffn_conv_w'])
    act = jax.nn.silu(gate_c + p['ffn_conv_b']) * up
    x = x + jnp.einsum('ntf,fd->ntd', act, p['ffn_out'])
    k_rows = jnp.concatenate([ka.reshape(n, t, N_DIFF_HEADS, HEAD_DIM), kb], axis=2)
    v_rows = jnp.concatenate([va, vb], axis=2)
    return x, (k_rows, v_rows, s_new, conv_new, ffn_new)


def page_gather(cache, layer, page_table, lo, hi):
    n_dec, n_pages = page_table.shape
    rows = cache[layer, page_table, :, lo:hi]
    return rows.reshape(n_dec, n_pages * cache.shape[2], hi - lo, cache.shape[-1])


def setup_inputs(seed: int = 0) -> dict:
    key = jax.random.key(seed)
    ks = jax.random.split(key, 32)
    f32 = jnp.float32
    n_pages = PAST_LEN // PAGE_SIZE
    n_used = DEC_BATCH * n_pages
    n_pool = n_used + n_used // 4

    def nrm(k, shape, s):
        return jax.random.normal(k, shape, f32) * s

    page_table = jax.random.permutation(ks[4], n_pool)[:n_used].reshape(DEC_BATCH, n_pages).astype(jnp.int32)
    dt = jnp.exp(jax.random.uniform(ks[20], (DEPTH, N_DN_HEADS), f32, math.log(1e-3), math.log(1e-1)))
    return {
        'x_prompt': nrm(ks[0], (BATCH, SEQ, D_MODEL), 1.0),
        'x_sample': nrm(ks[1], (DEC_BATCH, DEC_SEQ, D_MODEL), 1.0),
        'cache_k': nrm(ks[2], (DEPTH, n_pool, PAGE_SIZE, N_SOFT_HEADS, HEAD_DIM), 1.0),
        'cache_v': nrm(ks[3], (DEPTH, n_pool, PAGE_SIZE, N_SOFT_HEADS, HEAD_DIM), 1.0),
        'page_table': page_table,
        'state_dn': nrm(ks[5], (DEPTH, DEC_BATCH, N_DN_HEADS, DN_DIM, DN_DIM), 0.1),
        'state_conv_qkv': nrm(ks[6], (DEPTH, DEC_BATCH, DN_CONV - 1, 3 * DN_W), 1.0),
        'state_conv_ffn': nrm(ks[7], (DEPTH, DEC_BATCH, FFN_CONV - 1, D_FF), 1.0),
        'norm1_g': 1.0 + nrm(ks[8], (DEPTH, D_MODEL), 0.02),
        'norm2_g': 1.0 + nrm(ks[9], (DEPTH, D_MODEL), 0.02),
        'normf_g': 1.0 + nrm(ks[10], (D_MODEL,), 0.02),
        'w_in': nrm(ks[11], (DEPTH, D_MODEL, IN_W), D_MODEL ** -0.5),
        'w_out': nrm(ks[12], (DEPTH, MIX_W, D_MODEL), MIX_W ** -0.5),
        'lam_q1': nrm(ks[13], (DEPTH, DIFF_HALF), 0.1),
        'lam_k1': nrm(ks[14], (DEPTH, DIFF_HALF), 0.1),
        'lam_q2': nrm(ks[15], (DEPTH, DIFF_HALF), 0.1),
        'lam_k2': nrm(ks[16], (DEPTH, DIFF_HALF), 0.1),
        'subln_g': 1.0 + nrm(ks[17], (DEPTH, HEAD_DIM), 0.02),
        'dn_conv_w': nrm(ks[18], (DEPTH, DN_CONV, 3 * DN_W), DN_CONV ** -0.5),
        'dn_a_log': jnp.log(jax.random.uniform(ks[19], (DEPTH, N_DN_HEADS), f32, 1.0, 16.0)),
        'dn_dt_bias': dt + jnp.log(-jnp.expm1(-dt)),
        'dn_norm_g': 1.0 + nrm(ks[21], (DEPTH, DN_DIM), 0.02),
        'ffn_w_in': nrm(ks[22], (DEPTH, D_MODEL, 2 * D_FF), D_MODEL ** -0.5),
        'ffn_conv_w': nrm(ks[23], (DEPTH, FFN_CONV, D_FF), FFN_CONV ** -0.5),
        'ffn_conv_b': nrm(ks[24], (DEPTH, D_FF), 0.02),
        'ffn_w_out': nrm(ks[25], (DEPTH, D_FF, D_MODEL), D_FF ** -0.5),
    }


def reference(x_prompt, x_sample, cache_k, cache_v, page_table, state_dn, state_conv_qkv, state_conv_ffn,
              norm1_g, norm2_g, normf_g, w_in, w_out, lam_q1, lam_k1, lam_q2, lam_k2, subln_g,
              dn_conv_w, dn_a_log, dn_dt_bias, dn_norm_g, ffn_w_in, ffn_conv_w, ffn_conv_b, ffn_w_out):
    f32 = jnp.float32
    slopes_a, slopes_b = alibi_slopes()
    xp, xs = x_prompt, x_sample
    rows_p, rows_s = [], []
    for l in range(DEPTH):
        p = {'norm1': norm1_g[l], 'w_in': w_in[l], 'subln': subln_g[l], 'dn_conv_w': dn_conv_w[l],
             'dn_a_log': dn_a_log[l], 'dn_dt_bias': dn_dt_bias[l], 'dn_norm': dn_norm_g[l], 'w_out': w_out[l],
             'norm2': norm2_g[l], 'ffn_in': ffn_w_in[l], 'ffn_conv_w': ffn_conv_w[l], 'ffn_conv_b': ffn_conv_b[l],
             'ffn_out': ffn_w_out[l]}
        lam_init = 0.8 - 0.6 * math.exp(-0.3 * l)
        lam = (jnp.exp(jnp.sum(lam_q1[l].astype(f32) * lam_k1[l].astype(f32)))
               - jnp.exp(jnp.sum(lam_q2[l].astype(f32) * lam_k2[l].astype(f32))) + lam_init)
        xp, rp = decoder_layer(xp, p, lam, lam_init, slopes_a, slopes_b, None)
        past = (page_gather(cache_k, l, page_table, 0, N_DIFF_HEADS),
                page_gather(cache_v, l, page_table, 0, N_DIFF_HEADS),
                page_gather(cache_k, l, page_table, N_DIFF_HEADS, N_SOFT_HEADS),
                page_gather(cache_v, l, page_table, N_DIFF_HEADS, N_SOFT_HEADS),
                state_dn[l], state_conv_qkv[l], state_conv_ffn[l])
        xs, rs = decoder_layer(xs, p, lam, lam_init, slopes_a, slopes_b, past)
        rows_p.append(rp)
        rows_s.append(rs)
    y_prompt = rmsnorm(xp, normf_g)
    y_sample = rmsnorm(xs, normf_g)

    def stack(rows, i):
        return jnp.stack([r[i] for r in rows], axis=0)

    return (y_prompt, y_sample,
            stack(rows_p, 0), stack(rows_p, 1), stack(rows_p, 2), stack(rows_p, 3), stack(rows_p, 4),
            stack(rows_s, 0), stack(rows_s, 1), stack(rows_s, 2), stack(rows_s, 3), stack(rows_s, 4))
```

```python
import functools
import math

import numpy as np
import jax
import jax.numpy as jnp
from jax import lax
from jax.experimental import pallas as pl
from jax.experimental.pallas import tpu as pltpu

F32 = jnp.float32
BF16 = jnp.bfloat16
HI = lax.Precision.HIGHEST

LANES = 128
HEAD_DIM = 64
DIFF_HALF = 32
N_DIFF_HEADS = 4
N_MOBA_HEADS = 4
N_DN_HEADS = 8
N_PAIRS = 4
SOFT_W = 512
DN_W = 512
D_FF = 3584
MOBA_BLOCK = 256
MOBA_TOPK = 3
DN_CONV = 4
FFN_CONV = 3
EPS = 1e-6
NEG_BIG = -1e30
Z_W = 3712
VMEM_LIMIT = 56 * 1024 * 1024


def _cp(n_axes, vmem=VMEM_LIMIT):
    return pltpu.CompilerParams(dimension_semantics=("arbitrary",) * n_axes, vmem_limit_bytes=vmem)


def _sigmoid(x):
    return 1.0 / (1.0 + jnp.exp(-x))


def _silu(x):
    return x * _sigmoid(x)


def _softplus(x):
    return jnp.maximum(x, 0.0) + jnp.log(1.0 + jnp.exp(-jnp.abs(x)))


def _dot(a, b, prec=None):
    return jnp.dot(a, b, preferred_element_type=F32, precision=prec)


def _dot_nt(a, b, prec=None):
    return lax.dot_general(a, b, (((1,), (1,)), ((), ())), preferred_element_type=F32, precision=prec)


def _const_spec(shape):
    nd = len(shape)
    return pl.BlockSpec(shape, lambda *_: (0,) * nd)


def _bd_ones():
    i = np.arange(LANES)
    return jnp.asarray((i[:, None] // HEAD_DIM == i[None, :] // HEAD_DIM).astype(np.float32))


def _head_expand():
    h = np.arange(N_DN_HEADS)[:, None]
    lane = np.arange(DN_W)[None, :]
    return jnp.asarray((lane // HEAD_DIM == h).astype(np.float32))


def _proj_in_body(x_ref, g_ref, w_ref, qs_ref, k_ref, v_ref, qkv_ref, zg_ref, ba_ref):
    x = x_ref[...]
    ms = jnp.mean(x * x, axis=-1, keepdims=True)
    h = (x * lax.rsqrt(ms + EPS) * g_ref[...]).astype(BF16)
    qs_ref[...] = _dot(h, w_ref[:, 0:512])
    k_ref[...] = _dot(h, w_ref[:, 512:1024])
    v_ref[...] = _dot(h, w_ref[:, 1024:1536])
    qkv_ref[...] = _dot(h, w_ref[:, 1536:3072])
    zg_ref[...] = _dot(h, w_ref[:, 3072:3584])
    ba_ref[...] = _dot(h, w_ref[:, 3584:Z_W])


def _proj_in(x2d, g, w_perm):
    m, d = x2d.shape
    tm = min(m, 256)
    widths = (512, 512, 512, 1536, 512, Z_W - 3584)
    return pl.pallas_call(
        _proj_in_body,
        grid=(m // tm,),
        in_specs=[pl.BlockSpec((tm, d), lambda i: (i, 0)), _const_spec((1, d)), _const_spec((d, Z_W))],
        out_specs=[pl.BlockSpec((tm, w), lambda i: (i, 0)) for w in widths],
        out_shape=[jax.ShapeDtypeStruct((m, w), F32) for w in widths],
        compiler_params=_cp(1),
        name="proj_in",
    )(x2d, g.reshape(1, d), w_perm)


def _permute_w_in(w):
    cols = [w[:, 0:256], w[:, 768:1024], w[:, 256:512], w[:, 1024:1280], w[:, 512:768], w[:, 1280:1536],
            w[:, 1536:3600], jnp.zeros((w.shape[0], Z_W - 3600), w.dtype)]
    return jnp.concatenate(cols, axis=1).astype(BF16)


def _online_update(j, s, vt, m_scr, l_scr, acc_scr):
    m_old = m_scr[j]
    m_new = jnp.maximum(m_old, jnp.max(s, axis=-1, keepdims=True))
    p = jnp.exp(s - m_new)
    alpha = jnp.exp(m_old - m_new)
    l_scr[j] = alpha * l_scr[j] + jnp.sum(p, axis=-1, keepdims=True)
    acc_scr[j] = alpha * acc_scr[j] + _dot(p.astype(BF16), vt)
    m_scr[j] = m_new


def _diff_attn_body(lam_ref, q_ref, k_ref, v_ref, g_ref, o_ref,
                    kb_scr, vb_scr, qm_scr, m_scr, l_scr, acc_scr, *, tq, out_scale):
    p = pl.program_id(1)
    qi = pl.program_id(2)

    @pl.when(qi == 0)
    def _():
        kb_scr[...] = k_ref[0].astype(BF16)
        vb_scr[...] = v_ref[0].astype(BF16)

    lane = lax.broadcasted_iota(jnp.int32, (tq, LANES), 1)
    q = q_ref[0] * (DIFF_HALF ** -0.5)
    for j in range(4):
        lo = (j // 2) * HEAD_DIM + (j % 2) * DIFF_HALF
        qm_scr[j] = jnp.where((lane >= lo) & (lane < lo + DIFF_HALF), q, 0.0).astype(BF16)
    m_scr[...] = jnp.full(m_scr.shape, -jnp.inf, F32)
    l_scr[...] = jnp.zeros(l_scr.shape, F32)
    acc_scr[...] = jnp.zeros(acc_scr.shape, F32)
    slopes = [jnp.where(p == 0, 2.0 ** -(2 * hl + 1), 2.0 ** -(2 * (hl + 2) + 1)).astype(F32) for hl in range(2)]
    rel0 = (lax.broadcasted_iota(jnp.int32, (tq, tq), 1) - lax.broadcasted_iota(jnp.int32, (tq, tq), 0))

    def tile(ki, masked):
        k0 = pl.multiple_of(ki * tq, tq)
        kt = kb_scr[pl.ds(k0, tq), :]
        vt = vb_scr[pl.ds(k0, tq), :]
        rel = rel0 + (ki - qi) * tq
        relf = rel.astype(F32)
        for j in range(4):
            s = _dot_nt(qm_scr[j], kt) + slopes[j // 2] * relf
            if masked:
                s = jnp.where(rel <= 0, s, -jnp.inf)
            _online_update(j, s, vt, m_scr, l_scr, acc_scr)

    def body(ki, carry):
        tile(ki, False)
        return carry

    lax.fori_loop(0, qi, body, 0)
    tile(qi, True)

    lam = lam_ref[0]
    out = jnp.zeros((tq, LANES), F32)
    for hl in range(2):
        o = acc_scr[2 * hl] / l_scr[2 * hl] - lam * (acc_scr[2 * hl + 1] / l_scr[2 * hl + 1])
        o = jnp.where((lane >= hl * HEAD_DIM) & (lane < (hl + 1) * HEAD_DIM), o, 0.0)
        ms = jnp.sum(o * o, axis=-1, keepdims=True) * (1.0 / HEAD_DIM)
        out = out + o * lax.rsqrt(ms + EPS)
    o_ref[0] = out * g_ref[...] * out_scale


def _diff_attn_prompt(q_soft, k_rows, v_rows, lam, subln_g, lam_init, tq=256):
    n, t, _ = q_soft.shape
    tq = min(tq, t)
    g2 = jnp.tile(subln_g.reshape(1, HEAD_DIM), (1, 2))
    kern = functools.partial(_diff_attn_body, tq=tq, out_scale=1.0 - lam_init)
    return pl.pallas_call(
        kern,
        grid=(n, 2, t // tq),
        in_specs=[pl.BlockSpec(memory_space=pltpu.SMEM),
                  pl.BlockSpec((1, tq, LANES), lambda b, p, i: (b, i, p)),
                  pl.BlockSpec((1, t, LANES), lambda b, p, i: (b, 0, p)),
                  pl.BlockSpec((1, t, LANES), lambda b, p, i: (b, 0, p)),
                  _const_spec((1, LANES))],
        out_specs=pl.BlockSpec((1, tq, LANES), lambda b, p, i: (b, i, p)),
        out_shape=jax.ShapeDtypeStruct((n, t, 2 * LANES), F32),
        scratch_shapes=[pltpu.VMEM((t, LANES), BF16), pltpu.VMEM((t, LANES), BF16),
                        pltpu.VMEM((4, tq, LANES), BF16),
                        pltpu.VMEM((4, tq, 1), F32), pltpu.VMEM((4, tq, 1), F32),
                        pltpu.VMEM((4, tq, LANES), F32)],
        compiler_params=_cp(3),
        name="diff_attn_prompt",
    )(lam.reshape(1), q_soft, k_rows, v_rows, g2)


def _topk_select(gate, idx, n_valid, axis, size):
    gm = jnp.where(idx < n_valid, gate, -jnp.inf)
    sel = jnp.zeros(gate.shape, jnp.bool_)
    for _ in range(MOBA_TOPK):
        mx = jnp.max(gm, axis=axis, keepdims=True)
        is_max = (gm == mx) & (mx > -jnp.inf)
        first = jnp.min(jnp.where(is_max, idx, size), axis=axis, keepdims=True)
        pick = idx == first
        sel = sel | pick
        gm = jnp.where(pick, -jnp.inf, gm)
    return sel


def _moba_body(q_ref, k_ref, v_ref, o_ref, kb_scr, vb_scr, kmean_scr, lhs_scr, m_scr, l_scr, acc_scr, *, tq, nb):
    p = pl.program_id(1)
    qi = pl.program_id(2)

    @pl.when(qi == 0)
    def _():
        kb_scr[...] = k_ref[0].astype(BF16)
        vb_scr[...] = v_ref[0].astype(BF16)
        kmean_scr[...] = jnp.zeros(kmean_scr.shape, F32)
        for b in range(nb):
            kmean_scr[b:b + 1, :] = jnp.sum(k_ref[0, b * MOBA_BLOCK:(b + 1) * MOBA_BLOCK, :], axis=0,
                                            keepdims=True) * (1.0 / MOBA_BLOCK)

    lane = lax.broadcasted_iota(jnp.int32, (tq, LANES), 1)
    q = q_ref[0]
    kmean = kmean_scr[...]
    for hl in range(2):
        qh = jnp.where((lane >= hl * HEAD_DIM) & (lane < (hl + 1) * HEAD_DIM), q, 0.0)
        gate = _dot_nt(qh, kmean, HI)
        sel = _topk_select(gate, lane, qi, 1, LANES)
        selbias = jnp.where(sel, 0.0, NEG_BIG)
        lhs_scr[hl] = jnp.concatenate([(qh * (HEAD_DIM ** -0.5)).astype(BF16), selbias.astype(BF16)], axis=1)
    m_scr[...] = jnp.full(m_scr.shape, -jnp.inf, F32)
    l_scr[...] = jnp.zeros(l_scr.shape, F32)
    acc_scr[...] = jnp.zeros(acc_scr.shape, F32)
    slopes = [jnp.where(p == 0, 2.0 ** -(2 * hl + 2), 2.0 ** -(2 * (hl + 2) + 2)).astype(F32) for hl in range(2)]
    rel0 = (lax.broadcasted_iota(jnp.int32, (tq, tq), 1) - lax.broadcasted_iota(jnp.int32, (tq, tq), 0))
    krow = lax.broadcasted_iota(jnp.int32, (tq, LANES), 1)

    def tile(ki, diagonal):
        k0 = pl.multiple_of(ki * tq, tq)
        kt = kb_scr[pl.ds(k0, tq), :]
        vt = vb_scr[pl.ds(k0, tq), :]
        if diagonal:
            tag = jnp.zeros((tq, LANES), BF16)
        else:
            tag = jnp.where(krow == ki, 1.0, 0.0).astype(BF16)
        rhs = jnp.concatenate([kt, tag], axis=1)
        rel = rel0 + (ki - qi) * tq
        relf = rel.astype(F32)
        for hl in range(2):
            s = _dot_nt(lhs_scr[hl], rhs) + slopes[hl] * relf
            if diagonal:
                s = jnp.where(rel <= 0, s, -jnp.inf)
            _online_update(hl, s, vt, m_scr, l_scr, acc_scr)

    tile(qi, True)

    def body(ki, carry):
        tile(ki, False)
        return carry

    lax.fori_loop(0, qi, body, 0)

    out = jnp.zeros((tq, LANES), F32)
    for hl in range(2):
        o = acc_scr[hl] / l_scr[hl]
        out = out + jnp.where((lane >= hl * HEAD_DIM) & (lane < (hl + 1) * HEAD_DIM), o, 0.0)
    o_ref[0] = out


def _moba_prompt(q_soft, k_rows, v_rows):
    n, t, _ = q_soft.shape
    tq = MOBA_BLOCK
    assert t % tq == 0 and t // tq <= LANES
    nb = t // tq
    kern = functools.partial(_moba_body, tq=tq, nb=nb)
    return pl.pallas_call(
        kern,
        grid=(n, 2, nb),
        in_specs=[pl.BlockSpec((1, tq, LANES), lambda b, p, i: (b, i, 2 + p)),
                  pl.BlockSpec((1, t, LANES), lambda b, p, i: (b, 0, 2 + p)),
                  pl.BlockSpec((1, t, LANES), lambda b, p, i: (b, 0, 2 + p))],
        out_specs=pl.BlockSpec((1, tq, LANES), lambda b, p, i: (b, i, p)),
        out_shape=jax.ShapeDtypeStruct((n, t, 2 * LANES), F32),
        scratch_shapes=[pltpu.VMEM((t, LANES), BF16), pltpu.VMEM((t, LANES), BF16),
                        pltpu.VMEM((LANES, LANES), F32),
                        pltpu.VMEM((2, tq, 2 * LANES), BF16),
                        pltpu.VMEM((2, tq, 1), F32), pltpu.VMEM((2, tq, 1), F32),
                        pltpu.VMEM((2, tq, LANES), F32)],
        compiler_params=_cp(3),
        name="moba_prompt",
    )(q_soft, k_rows, v_rows)


def _bdiag(x, lane_lo):
    return jnp.concatenate([jnp.where(lane_lo, x, 0.0), jnp.where(lane_lo, 0.0, x)], axis=0)


def _gdn_prompt_body(qkv_ref, zg_ref, b_ref, a_ref, arow_ref, cw_ref, alog_ref, dtb_ref, alogr_ref, dtbr_ref,
                     gn_ref, e8_ref, bd_ref, ltri_ref, ubd_ref,
                     o_ref, sfin_ref, xbuf, s_scr, *, c):
    ci = pl.program_id(1)
    nc = pl.num_programs(1)

    @pl.when(ci == 0)
    def _():
        xbuf[0:8, :] = jnp.zeros((8, 3 * DN_W), F32)
        s_scr[...] = jnp.zeros(s_scr.shape, F32)

    xbuf[8:8 + c, :] = qkv_ref[0]
    cw = cw_ref[...]
    y = (cw[3:4] * xbuf[8:8 + c, :] + cw[2:3] * xbuf[7:7 + c, :]
         + cw[1:2] * xbuf[6:6 + c, :] + cw[0:1] * xbuf[5:5 + c, :])
    xbuf[0:8, :] = xbuf[c:c + 8, :]
    cq = _silu(y)

    beta = _sigmoid(b_ref[0])
    g_col = -jnp.exp(alog_ref[...]) * _softplus(a_ref[0] + dtb_ref[...])
    g_row = -jnp.exp(alogr_ref[...]) * _softplus(arow_ref[0, 0] + dtbr_ref[...])
    gc_col = _dot(ltri_ref[...], g_col, HI)
    gc_row = _dot(g_row, ubd_ref[...], HI)
    bd = bd_ref[...]

    row = lax.broadcasted_iota(jnp.int32, (c, LANES), 0)
    lane = lax.broadcasted_iota(jnp.int32, (c, LANES), 1)
    col = lane % HEAD_DIM
    lane_lo = lane < HEAD_DIM
    incl = row >= col
    strict = row > col
    eye2 = jnp.where(row == col, 1.0, 0.0)
    r128 = lax.broadcasted_iota(jnp.int32, (LANES, LANES), 0)
    c128 = lax.broadcasted_iota(jnp.int32, (LANES, LANES), 1)
    same_head = (r128 // HEAD_DIM) == (c128 // HEAD_DIM)

    for p in range(N_PAIRS):
        sl = slice(p * LANES, (p + 1) * LANES)
        qp = cq[:, p * LANES:(p + 1) * LANES]
        kp = cq[:, DN_W + p * LANES:DN_W + (p + 1) * LANES]
        vp = cq[:, 2 * DN_W + p * LANES:2 * DN_W + (p + 1) * LANES]
        qn = qp * lax.rsqrt(_dot(qp * qp, bd, HI) + EPS) * (HEAD_DIM ** -0.5)
        kn = kp * lax.rsqrt(_dot(kp * kp, bd, HI) + EPS)
        e8 = e8_ref[:, sl]
        bexp = _dot(beta, e8, HI)
        gcx = _dot(gc_col, e8, HI)
        gcr = gc_row[p:p + 1, :]
        decay = jnp.exp(jnp.where(incl, gcx - gcr, -jnp.inf))
        kbd = _bdiag(kn, lane_lo)
        kk = _dot_nt(kn, kbd, HI)
        a = jnp.where(strict, bexp * kk * decay, 0.0)
        qk = jnp.where(incl, _dot_nt(qn, kbd, HI) * decay, 0.0)
        bpow = -a
        x = eye2 + bpow
        steps = max(1, int(math.ceil(math.log2(c))) - 1)
        for _ in range(steps):
            bpow = _dot(bpow, _bdiag(bpow, lane_lo), HI)
            x = x + _dot(x, _bdiag(bpow, lane_lo), HI)
        u = _dot(x, _bdiag(vp * bexp, lane_lo), HI)
        w = _dot(x, _bdiag(kn * bexp * jnp.exp(gcx), lane_lo), HI)
        s = s_scr[p]
        v_new = u - _dot(w, s, HI)
        o = _dot(qn * jnp.exp(gcx), s, HI) + _dot(qk, _bdiag(v_new, lane_lo), HI)
        glast = gcx[c - 1:c, :]
        kdec = kn * jnp.exp(glast - gcx)
        upd = lax.dot_general(kdec, v_new, (((0,), (0,)), ((), ())), preferred_element_type=F32, precision=HI)
        s_scr[p] = s * jnp.exp(glast) + jnp.where(same_head, upd, 0.0)
        ms = _dot(o * o, bd, HI) * (1.0 / HEAD_DIM)
        o_ref[0, :, sl] = o * lax.rsqrt(ms + EPS) * gn_ref[...] * _silu(zg_ref[0, :, sl])

    @pl.when(ci == nc - 1)
    def _():
        sfin_ref[0] = s_scr[...]


def _gdn_prompt(qkv, zg, ba, conv_w, a_log, dt_bias, norm_g, c=64):
    n, t, _ = qkv.shape
    nc = t // c
    b_raw = ba[..., 0:N_DN_HEADS]
    a_raw = ba[..., N_DN_HEADS:2 * N_DN_HEADS]
    a_row = a_raw.reshape(n, nc, c, N_PAIRS, 2).transpose(0, 1, 3, 4, 2).reshape(n, nc, N_PAIRS, 2 * c)
    rep = lambda v: jnp.repeat(v.reshape(N_PAIRS, 2), c, axis=1)
    i = np.arange(c)
    ltri = jnp.asarray((i[:, None] >= i[None, :]).astype(np.float32))
    j = np.arange(2 * c)
    ubd = jnp.asarray(((j[:, None] // c == j[None, :] // c) & (j[:, None] % c <= j[None, :] % c)).astype(np.float32))
    assert 2 * c == LANES
    kern = functools.partial(_gdn_prompt_body, c=c)
    return pl.pallas_call(
        kern,
        grid=(n, nc),
        in_specs=[pl.BlockSpec((1, c, 3 * DN_W), lambda b, i: (b, i, 0)),
                  pl.BlockSpec((1, c, DN_W), lambda b, i: (b, i, 0)),
                  pl.BlockSpec((1, c, N_DN_HEADS), lambda b, i: (b, i, 0)),
                  pl.BlockSpec((1, c, N_DN_HEADS), lambda b, i: (b, i, 0)),
                  pl.BlockSpec((1, 1, N_PAIRS, LANES), lambda b, i: (b, i, 0, 0)),
                  _const_spec((DN_CONV, 3 * DN_W)),
                  _const_spec((1, N_DN_HEADS)), _const_spec((1, N_DN_HEADS)),
                  _const_spec((N_PAIRS, LANES)), _const_spec((N_PAIRS, LANES)),
                  _const_spec((1, LANES)), _const_spec((N_DN_HEADS, DN_W)), _const_spec((LANES, LANES)),
                  _const_spec((c, c)), _const_spec((LANES, LANES))],
        out_specs=[pl.BlockSpec((1, c, DN_W), lambda b, i: (b, i, 0)),
                   pl.BlockSpec((1, N_PAIRS, LANES, LANES), lambda b, i: (b, 0, 0, 0))],
        out_shape=[jax.ShapeDtypeStruct((n, t, DN_W), F32),
                   jax.ShapeDtypeStruct((n, N_PAIRS, LANES, LANES), F32)],
        scratch_shapes=[pltpu.VMEM((c + 8, 3 * DN_W), F32), pltpu.VMEM((N_PAIRS, LANES, LANES), F32)],
        compiler_params=_cp(2),
        name="gdn_prompt",
    )(qkv, zg, b_raw, a_raw, a_row, conv_w, a_log.reshape(1, -1), dt_bias.reshape(1, -1),
      rep(a_log), rep(dt_bias), jnp.tile(norm_g.reshape(1, HEAD_DIM), (1, 2)), _head_expand(), _bd_ones(), ltri, ubd)


def _state_to_bd(s):
    n = s.shape[0]
    s = s.reshape(n, N_PAIRS, 2, HEAD_DIM, HEAD_DIM)
    z = jnp.zeros_like(s[:, :, 0])
    top = jnp.concatenate([s[:, :, 0], z], axis=-1)
    bot = jnp.concatenate([z, s[:, :, 1]], axis=-1)
    return jnp.concatenate([top, bot], axis=-2)


def _state_from_bd(sbd):
    n = sbd.shape[0]
    s = jnp.stack([sbd[:, :, :HEAD_DIM, :HEAD_DIM], sbd[:, :, HEAD_DIM:, HEAD_DIM:]], axis=2)
    return s.reshape(n, N_DN_HEADS, HEAD_DIM, HEAD_DIM)


def _gdn_decode_body(qkv_ref, cb_ref, zg_ref, b_ref, a_ref, cw_ref, alog_ref, dtb_ref, gn_ref, e8_ref, bd_ref,
                     s_ref, o_ref, snew_ref, cnew_ref):
    x = qkv_ref[...]
    buf = cb_ref[...]
    cw = cw_ref[...]
    y = cw[0:1] * buf[0:1] + cw[1:2] * buf[1:2] + cw[2:3] * buf[2:3] + cw[3:4] * x
    cnew_ref[...] = jnp.concatenate([buf[1:3], x], axis=0)
    cq = _silu(y)
    beta = _sigmoid(b_ref[...])
    g = -jnp.exp(alog_ref[...]) * _softplus(a_ref[...] + dtb_ref[...])
    bd = bd_ref[...]
    r128 = lax.broadcasted_iota(jnp.int32, (LANES, LANES), 0)
    c128 = lax.broadcasted_iota(jnp.int32, (LANES, LANES), 1)
    eye = r128 == c128
    rows8 = lambda v: jnp.broadcast_to(v, (8, v.shape[-1]))
    bexp = _dot(rows8(beta), e8_ref[...], HI)[0:1]
    eg = jnp.exp(_dot(rows8(g), e8_ref[...], HI)[0:1])
    for p in range(N_PAIRS):
        sl = slice(p * LANES, (p + 1) * LANES)
        qp = cq[:, p * LANES:(p + 1) * LANES]
        kp = cq[:, DN_W + p * LANES:DN_W + (p + 1) * LANES]
        vp = cq[:, 2 * DN_W + p * LANES:2 * DN_W + (p + 1) * LANES]
        qn = qp * lax.rsqrt(_dot(rows8(qp * qp), bd, HI)[0:1] + EPS) * (HEAD_DIM ** -0.5)
        kn = kp * lax.rsqrt(_dot(rows8(kp * kp), bd, HI)[0:1] + EPS)
        s = s_ref[p] * eg[:, sl]
        ks = _dot(rows8(kn), s, HI)[0:1]
        delta = (vp - ks) * bexp[:, sl]
        kcol = _dot(jnp.where(eye, jnp.broadcast_to(kn, (LANES, LANES)), 0.0), bd, HI)
        s = s + kcol * delta
        snew_ref[p] = s
        o = _dot(rows8(qn), s, HI)[0:1]
        ms = _dot(rows8(o * o), bd, HI)[0:1] * (1.0 / HEAD_DIM)
        o_ref[:, sl] = o * lax.rsqrt(ms + EPS) * gn_ref[...] * _silu(zg_ref[:, sl])


def _gdn_decode(qkv, zg, ba, conv_buf, s_bd, conv_w, a_log, dt_bias, norm_g):
    ns = qkv.shape[0]
    b_raw = ba[:, None, 0:N_DN_HEADS]
    a_raw = ba[:, None, N_DN_HEADS:2 * N_DN_HEADS]
    row = lambda w: pl.BlockSpec((None, 1, w), lambda i: (i, 0, 0))
    return pl.pallas_call(
        _gdn_decode_body,
        grid=(ns,),
        in_specs=[row(3 * DN_W), pl.BlockSpec((None, DN_CONV - 1, 3 * DN_W), lambda i: (i, 0, 0)), row(DN_W),
                  row(N_DN_HEADS), row(N_DN_HEADS),
                  _const_spec((DN_CONV, 3 * DN_W)), _const_spec((1, N_DN_HEADS)), _const_spec((1, N_DN_HEADS)),
                  _const_spec((1, LANES)), _const_spec((N_DN_HEADS, DN_W)), _const_spec((LANES, LANES)),
                  pl.BlockSpec((None, N_PAIRS, LANES, LANES), lambda i: (i, 0, 0, 0))],
        out_specs=[row(DN_W), pl.BlockSpec((None, N_PAIRS, LANES, LANES), lambda i: (i, 0, 0, 0)),
                   pl.BlockSpec((None, DN_CONV - 1, 3 * DN_W), lambda i: (i, 0, 0))],
        out_shape=[jax.ShapeDtypeStruct((ns, 1, DN_W), F32),
                   jax.ShapeDtypeStruct((ns, N_PAIRS, LANES, LANES), F32),
                   jax.ShapeDtypeStruct((ns, DN_CONV - 1, 3 * DN_W), F32)],
        compiler_params=_cp(1),
        name="gdn_decode",
    )(qkv[:, None, :], conv_buf, zg[:, None, :], b_raw, a_raw, conv_w, a_log.reshape(1, -1), dt_bias.reshape(1, -1),
      jnp.tile(norm_g.reshape(1, HEAD_DIM), (1, 2)), _head_expand(), _bd_ones(), s_bd)


def _rms(x, g):
    return x * lax.rsqrt(jnp.mean(x * x, axis=-1, keepdims=True) + EPS) * g


def _ffn_prompt_body(x_ref, oa_ref, ob_ref, oc_ref, wo_ref, g2_ref, wg_ref, wu_ref, cw_ref, cb_ref, wd_ref, gf_ref,
                     y_ref, st_ref, acc, h2, gbuf, carry, *, tm, final):
    i = pl.program_id(1)
    j = pl.program_id(2)
    nj = pl.num_programs(2)

    @pl.when(j == 0)
    def _():
        x1 = (x_ref[0] + _dot(oa_ref[0].astype(BF16), wo_ref[0:256, :]) + _dot(ob_ref[0].astype(BF16), wo_ref[256:512, :])
              + _dot(oc_ref[0].astype(BF16), wo_ref[512:1024, :]))
        acc[...] = x1
        h2[...] = _rms(x1, g2_ref[...]).astype(BF16)

    @pl.when(i == 0)
    def _():
        carry[j] = jnp.zeros(carry.shape[1:], F32)

    h = h2[...]
    g = _dot(h, wg_ref[...])
    u = _dot(h, wu_ref[...])
    gbuf[0:8, :] = carry[j]
    gbuf[8:8 + tm, :] = g
    cw = cw_ref[...]
    gc = cw[2:3] * g + cw[1:2] * gbuf[7:7 + tm, :] + cw[0:1] * gbuf[6:6 + tm, :] + cb_ref[...]
    carry[j] = gbuf[tm:tm + 8, :]
    st_ref[0, j] = gbuf[tm + 6:tm + 8, :]
    act = (_silu(gc) * u).astype(BF16)
    acc[...] += _dot(act, wd_ref[...])

    @pl.when(j == nj - 1)
    def _():
        if final:
            y_ref[0] = _rms(acc[...], gf_ref[...])
        else:
            y_ref[0] = acc[...]


def _ffn_prompt(x, o_a, o_b, o_c, w_out, norm2_g, w_ffn_in, conv_w, conv_b, w_ffn_out, normf_g, final, tm=512, tf=512):
    n, t, d = x.shape
    tm = min(tm, t)
    nj = D_FF // tf
    kern = functools.partial(_ffn_prompt_body, tm=tm, final=final)
    rows = lambda w: pl.BlockSpec((1, tm, w), lambda b, i, j: (b, i, 0))
    y, st = pl.pallas_call(
        kern,
        grid=(n, t // tm, nj),
        in_specs=[rows(d), rows(256), rows(256), rows(512),
                  _const_spec((d, d)), _const_spec((1, d)),
                  pl.BlockSpec((d, tf), lambda b, i, j: (0, j)),
                  pl.BlockSpec((d, tf), lambda b, i, j: (0, j + nj)),
                  pl.BlockSpec((FFN_CONV, tf), lambda b, i, j: (0, j)),
                  pl.BlockSpec((1, tf), lambda b, i, j: (0, j)),
                  pl.BlockSpec((tf, d), lambda b, i, j: (j, 0)),
                  _const_spec((1, d))],
        out_specs=[rows(d), pl.BlockSpec((1, nj, FFN_CONV - 1, tf), lambda b, i, j: (b, 0, 0, 0))],
        out_shape=[jax.ShapeDtypeStruct((n, t, d), F32), jax.ShapeDtypeStruct((n, nj, FFN_CONV - 1, tf), F32)],
        scratch_shapes=[pltpu.VMEM((tm, d), F32), pltpu.VMEM((tm, d), BF16), pltpu.VMEM((tm + 8, tf), F32),
                        pltpu.VMEM((nj, 8, tf), F32)],
        compiler_params=_cp(3),
        name="ffn_prompt",
    )(x, o_a, o_b, o_c, w_out, norm2_g.reshape(1, d), w_ffn_in, w_ffn_in, conv_w, conv_b.reshape(1, D_FF),
      w_ffn_out, normf_g.reshape(1, d))
    return y, st.transpose(0, 2, 1, 3).reshape(n, FFN_CONV - 1, D_FF)


def _ffn_decode_body(x_ref, oa_ref, ob_ref, oc_ref, wo_ref, g2_ref, wg_ref, wu_ref, cw_ref, cb_ref, wd_ref, gf_ref,
                     prev_ref, y_ref, st_ref, acc, h2, *, final):
    j = pl.program_id(0)
    nj = pl.num_programs(0)

    @pl.when(j == 0)
    def _():
        x1 = (x_ref[...] + _dot(oa_ref[...].astype(BF16), wo_ref[0:256, :]) + _dot(ob_ref[...].astype(BF16), wo_ref[256:512, :])
              + _dot(oc_ref[...].astype(BF16), wo_ref[512:1024, :]))
        acc[...] = x1
        h2[...] = _rms(x1, g2_ref[...]).astype(BF16)

    h = h2[...]
    g = _dot(h, wg_ref[...])
    u = _dot(h, wu_ref[...])
    cw = cw_ref[...]
    gc = cw[2:3] * g + cw[1:2] * prev_ref[1] + cw[0:1] * prev_ref[0] + cb_ref[...]
    st_ref[0] = prev_ref[1]
    st_ref[1] = g
    act = (_silu(gc) * u).astype(BF16)
    acc[...] += _dot(act, wd_ref[...])

    @pl.when(j == nj - 1)
    def _():
        if final:
            y_ref[...] = _rms(acc[...], gf_ref[...])
        else:
            y_ref[...] = acc[...]


def _ffn_decode(x, o_a, o_b, o_c, prev, w_out, norm2_g, w_ffn_in, conv_w, conv_b, w_ffn_out, normf_g, final, tf=512):
    ns, d = x.shape
    nj = D_FF // tf
    kern = functools.partial(_ffn_decode_body, final=final)
    return pl.pallas_call(
        kern,
        grid=(nj,),
        in_specs=[_const_spec((ns, d)), _const_spec((ns, 256)), _const_spec((ns, 256)), _const_spec((ns, 512)),
                  _const_spec((d, d)), _const_spec((1, d)),
                  pl.BlockSpec((d, tf), lambda j: (0, j)),
                  pl.BlockSpec((d, tf), lambda j: (0, j + nj)),
                  pl.BlockSpec((FFN_CONV, tf), lambda j: (0, j)),
                  pl.BlockSpec((1, tf), lambda j: (0, j)),
                  pl.BlockSpec((tf, d), lambda j: (j, 0)),
                  _const_spec((1, d)),
                  pl.BlockSpec((FFN_CONV - 1, ns, tf), lambda j: (0, 0, j))],
        out_specs=[_const_spec((ns, d)), pl.BlockSpec((FFN_CONV - 1, ns, tf), lambda j: (0, 0, j))],
        out_shape=[jax.ShapeDtypeStruct((ns, d), F32), jax.ShapeDtypeStruct((FFN_CONV - 1, ns, D_FF), F32)],
        scratch_shapes=[pltpu.VMEM((ns, d), F32), pltpu.VMEM((ns, d), BF16)],
        compiler_params=_cp(1),
        name="ffn_decode",
    )(x, o_a, o_b, o_c, w_out, norm2_g.reshape(1, d), w_ffn_in, w_ffn_in, conv_w, conv_b.reshape(1, D_FF),
      w_ffn_out, normf_g.reshape(1, d), prev)


ROWS = 16


def _decode_consts(lam_init):
    j = np.arange(ROWS)[:, None]
    lane = np.arange(SOFT_W)[None, :]
    is_diff = j < 8
    is_moba = (j >= 8) & (j < 12)
    q_lo = np.where(is_diff, (j // 2) * HEAD_DIM + (j % 2) * DIFF_HALF, 256 + (j - 8) * HEAD_DIM)
    q_w = np.where(is_diff, DIFF_HALF, HEAD_DIM)
    qmask = ((lane >= q_lo) & (lane < q_lo + q_w) & (is_diff | is_moba))
    qscale = np.where(is_diff, DIFF_HALF ** -0.5, HEAD_DIM ** -0.5)
    v_lo = np.where(is_diff, (j // 2) * HEAD_DIM, 256 + (j - 8) * HEAD_DIM)
    vmask = ((lane >= v_lo) & (lane < v_lo + HEAD_DIM) & (is_diff | is_moba))
    head = np.where(is_diff, j // 2, j - 8)
    slope = np.where(is_diff, 2.0 ** -(2 * head + 1), np.where(is_moba, 2.0 ** -(2 * head + 2), 0.0))
    coef_a = np.where((is_diff & (j % 2 == 0)) | is_moba, 1.0, 0.0)
    coef_b = np.where(is_diff & (j % 2 == 1), -1.0, 0.0)
    f = lambda a: jnp.asarray(np.broadcast_to(a, (ROWS, LANES)).astype(np.float32))
    i = np.arange(SOFT_W)
    bd512 = (i[:, None] // HEAD_DIM == i[None, :] // HEAD_DIM).astype(np.float32)
    return (jnp.asarray((qmask * qscale).astype(np.float32)), jnp.asarray(vmask.astype(np.float32)),
            f(slope), f(coef_a), f(coef_b), jnp.asarray(bd512))


def _decode_attn_body(pt_ref, lam_ref, q_ref, ks_ref, vs_ref, k0_ref, k1_ref, v0_ref, v1_ref,
                      qmask_ref, vmask_ref, slope_ref, ca_ref, cb_ref, bd_ref, g_ref, o_ref,
                      m_scr, l_scr, acc_scr, mblk, lblk, gblk, accblk, *, nblk, past_len, out_scale):
    del pt_ref
    b = pl.program_id(1)
    tk = 2 * k0_ref.shape[0]

    @pl.when(b == 0)
    def _():
        m_scr[...] = jnp.full(m_scr.shape, -jnp.inf, F32)
        l_scr[...] = jnp.zeros(l_scr.shape, F32)
        acc_scr[...] = jnp.zeros(acc_scr.shape, F32)
        mblk[...] = jnp.full(mblk.shape, -jnp.inf, F32)
        lblk[...] = jnp.zeros(lblk.shape, F32)
        gblk[...] = jnp.full(gblk.shape, -jnp.inf, F32)

    kk = jnp.concatenate([k0_ref[...], k1_ref[...]], axis=0)
    vv = jnp.concatenate([v0_ref[...], v1_ref[...]], axis=0)
    qs = q_ref[...] * qmask_ref[...]
    slope = slope_ref[:, 0:1]
    kpos = b * tk + lax.broadcasted_iota(jnp.int32, (1, tk), 1)
    dist = (past_len - kpos).astype(F32)
    s = _dot_nt(qs.astype(BF16), kk.astype(BF16)) - slope * dist
    mb = jnp.max(s, axis=-1, keepdims=True)
    pexp = jnp.exp(s - mb)
    lb = jnp.sum(pexp, axis=-1, keepdims=True)
    accb = _dot(pexp.astype(BF16), vv.astype(BF16))

    m_old = m_scr[:, 0:1]
    m_new = jnp.maximum(m_old, mb)
    a_old = jnp.exp(m_old - m_new)
    a_blk = jnp.exp(mb - m_new)
    l_scr[...] = jnp.broadcast_to(a_old * l_scr[:, 0:1] + a_blk * lb, l_scr.shape)
    acc_scr[...] = a_old * acc_scr[...] + a_blk * accb
    m_scr[...] = jnp.broadcast_to(m_new, m_scr.shape)

    lane = lax.broadcasted_iota(jnp.int32, (ROWS, LANES), 1)
    hit = lane == b
    kmean = jnp.sum(kk, axis=0, keepdims=True) * (1.0 / tk)
    gate = jnp.sum(qs * kmean, axis=-1, keepdims=True)
    mblk[...] = jnp.where(hit, mb, mblk[...])
    lblk[...] = jnp.where(hit, lb, lblk[...])
    gblk[...] = jnp.where(hit, gate, gblk[...])
    accblk[b] = accb

    @pl.when(b == nblk - 1)
    def _():
        ks = ks_ref[...]
        vs = vs_ref[...]
        s_self = jnp.sum(qs * ks, axis=-1, keepdims=True)
        m_o = m_scr[:, 0:1]
        m_d = jnp.maximum(m_o, s_self)
        a_o = jnp.exp(m_o - m_d)
        p_d = jnp.exp(s_self - m_d)
        o_d = (a_o * acc_scr[...] + p_d * vs) / (a_o * l_scr[:, 0:1] + p_d)
        sel = _topk_select(gblk[...], lane, nblk, 1, LANES)
        mm = mblk[...]
        m_f = jnp.maximum(jnp.max(jnp.where(sel, mm, -jnp.inf), axis=-1, keepdims=True), s_self)
        wgt = jnp.where(sel, jnp.exp(mm - m_f), 0.0)
        p_m = jnp.exp(s_self - m_f)
        l_m = jnp.sum(wgt * lblk[...], axis=-1, keepdims=True) + p_m
        acc_m = p_m * vs
        for blk in range(nblk):
            acc_m = acc_m + wgt[:, blk:blk + 1] * accblk[blk]
        o_m = acc_m / l_m
        rowi = lax.broadcasted_iota(jnp.int32, (ROWS, SOFT_W), 0)
        o_all = jnp.where(rowi < 8, o_d, o_m)
        coef = ca_ref[:, 0:1] + lam_ref[0] * cb_ref[:, 0:1]
        o_row = jnp.sum(coef * vmask_ref[...] * o_all, axis=0, keepdims=True)
        ms = _dot(jnp.broadcast_to(o_row * o_row, (8, SOFT_W)), bd_ref[...], HI)[0:1] * (1.0 / HEAD_DIM)
        o_norm = o_row * lax.rsqrt(ms + EPS) * g_ref[...] * out_scale
        lane5 = lax.broadcasted_iota(jnp.int32, (1, SOFT_W), 1)
        o_ref[...] = jnp.where(lane5 < 256, o_norm, o_row)


def _decode_attn(layer, q_soft, k_self, v_self, cache_k, cache_v, page_table, lam, subln_g, lam_init):
    ns = q_soft.shape[0]
    n_pages = page_table.shape[1]
    page = cache_k.shape[2]
    assert 2 * page == MOBA_BLOCK and n_pages % 2 == 0
    nblk = n_pages // 2
    assert nblk <= LANES
    consts = _decode_consts(lam_init)
    g4 = jnp.concatenate([jnp.tile(subln_g.reshape(1, HEAD_DIM), (1, 4)), jnp.ones((1, 256), F32)], axis=1)
    kern = functools.partial(_decode_attn_body, nblk=nblk, past_len=n_pages * page, out_scale=1.0 - lam_init)
    row = pl.BlockSpec((None, 1, SOFT_W), lambda s, b, pt: (s, 0, 0))
    pg = lambda off: pl.BlockSpec((None, None, page, SOFT_W), lambda s, b, pt: (layer, pt[s, 2 * b + off], 0, 0))
    cst = lambda shp: pl.BlockSpec(shp, lambda s, b, pt: (0,) * len(shp))
    grid_spec = pltpu.PrefetchScalarGridSpec(
        num_scalar_prefetch=1,
        grid=(ns, nblk),
        in_specs=[pl.BlockSpec(memory_space=pltpu.SMEM), row, row, row, pg(0), pg(1), pg(0), pg(1),
                  cst((ROWS, SOFT_W)), cst((ROWS, SOFT_W)), cst((ROWS, LANES)), cst((ROWS, LANES)), cst((ROWS, LANES)),
                  cst((SOFT_W, SOFT_W)), cst((1, SOFT_W))],
        out_specs=row,
        scratch_shapes=[pltpu.VMEM((ROWS, LANES), F32), pltpu.VMEM((ROWS, LANES), F32), pltpu.VMEM((ROWS, SOFT_W), F32),
                        pltpu.VMEM((ROWS, LANES), F32), pltpu.VMEM((ROWS, LANES), F32), pltpu.VMEM((ROWS, LANES), F32),
                        pltpu.VMEM((nblk, ROWS, SOFT_W), F32)],
    )
    out = pl.pallas_call(
        kern,
        grid_spec=grid_spec,
        out_shape=jax.ShapeDtypeStruct((ns, 1, SOFT_W), F32),
        compiler_params=_cp(2),
        name="decode_attn",
    )(page_table, lam.reshape(1), q_soft[:, None, :], k_self[:, None, :], v_self[:, None, :],
      cache_k, cache_k, cache_v, cache_v, *consts, g4)
    return out[:, 0, :]


def _lam(l, lam_q1, lam_k1, lam_q2, lam_k2):
    lam_init = 0.8 - 0.6 * math.exp(-0.3 * l)
    lam = (jnp.exp(jnp.sum(lam_q1[l] * lam_k1[l])) - jnp.exp(jnp.sum(lam_q2[l] * lam_k2[l])) + lam_init)
    return lam.astype(F32), lam_init


def _prompt_layer(x, w, lam, lam_init, final):
    n, t, d = x.shape
    qs, kr, vr, qkv, zg, ba = _proj_in(x.reshape(n * t, d), w["norm1"], w["w_in"])
    r3 = lambda a: a.reshape(n, t, a.shape[-1])
    qs, kr, vr, qkv, zg, ba = r3(qs), r3(kr), r3(vr), r3(qkv), r3(zg), r3(ba)
    o_a = _diff_attn_prompt(qs, kr, vr, lam, w["subln"], lam_init)
    o_b = _moba_prompt(qs, kr, vr)
    o_c, s_bd = _gdn_prompt(qkv, zg, ba, w["dn_conv_w"], w["dn_a_log"], w["dn_dt_bias"], w["dn_norm"])
    y, ffn_new = _ffn_prompt(x, o_a, o_b, o_c, w["w_out"], w["norm2"], w["ffn_in"], w["ffn_conv_w"], w["ffn_conv_b"],
                             w["ffn_out"], w["normf"], final)
    rows = (kr.reshape(n, t, 8, HEAD_DIM), vr.reshape(n, t, 8, HEAD_DIM), _state_from_bd(s_bd),
            qkv[:, t - (DN_CONV - 1):, :], ffn_new)
    return y, rows


def _sample_layer(l, x, w, lam, lam_init, final, cache_k, cache_v, page_table, state_dn, conv_qkv, conv_ffn):
    ns, d = x.shape
    qs, kr, vr, qkv, zg, ba = _proj_in(x, w["norm1"], w["w_in"])
    o_ab = _decode_attn(l, qs, kr, vr, cache_k, cache_v, page_table, lam, w["subln"], lam_init)
    o_c, s_bd, conv_new = _gdn_decode(qkv, zg, ba, conv_qkv, _state_to_bd(state_dn), w["dn_conv_w"], w["dn_a_log"],
                                      w["dn_dt_bias"], w["dn_norm"])
    y, ffn_new = _ffn_decode(x, o_ab[:, 0:256], o_ab[:, 256:512], o_c[:, 0, :], jnp.swapaxes(conv_ffn, 0, 1),
                             w["w_out"], w["norm2"], w["ffn_in"], w["ffn_conv_w"], w["ffn_conv_b"], w["ffn_out"],
                             w["normf"], final)
    rows = (kr.reshape(ns, 1, 8, HEAD_DIM), vr.reshape(ns, 1, 8, HEAD_DIM), _state_from_bd(s_bd), conv_new,
            jnp.swapaxes(ffn_new, 0, 1))
    return y, rows


def kernel(x_prompt, x_sample, cache_k, cache_v, page_table, state_dn, state_conv_qkv, state_conv_ffn, norm1_g, norm2_g, normf_g, w_in, w_out, lam_q1, lam_k1, lam_q2, lam_k2, subln_g, dn_conv_w, dn_a_log, dn_dt_bias, dn_norm_g, ffn_w_in, ffn_conv_w, ffn_conv_b, ffn_w_out):
    depth = w_in.shape[0]
    n_pool, page = cache_k.shape[1], cache_k.shape[2]
    ck = cache_k.reshape(depth, n_pool, page, SOFT_W)
    cv = cache_v.reshape(depth, n_pool, page, SOFT_W)
    xp = x_prompt
    xs = x_sample.reshape(x_sample.shape[0], x_sample.shape[2])
    rows_p, rows_s = [], []
    for l in range(depth):
        w = {"norm1": norm1_g[l], "w_in": _permute_w_in(w_in[l]), "subln": subln_g[l], "dn_conv_w": dn_conv_w[l],
             "dn_a_log": dn_a_log[l], "dn_dt_bias": dn_dt_bias[l], "dn_norm": dn_norm_g[l],
             "w_out": w_out[l].astype(BF16), "norm2": norm2_g[l], "ffn_in": ffn_w_in[l].astype(BF16),
             "ffn_conv_w": ffn_conv_w[l], "ffn_conv_b": ffn_conv_b[l], "ffn_out": ffn_w_out[l].astype(BF16),
             "normf": normf_g}
        lam, lam_init = _lam(l, lam_q1, lam_k1, lam_q2, lam_k2)
        final = l == depth - 1
        xp, rp = _prompt_layer(xp, w, lam, lam_init, final)
        xs, rs = _sample_layer(l, xs, w, lam, lam_init, final, ck, cv, page_table, state_dn[l], state_conv_qkv[l],
                               state_conv_ffn[l])
        rows_p.append(rp)
        rows_s.append(rs)
    stack = lambda rows, i: jnp.stack([r[i] for r in rows], axis=0)
    y_sample = xs.reshape(x_sample.shape)
    return (xp, y_sample,
            stack(rows_p, 0), stack(rows_p, 1), stack(rows_p, 2), stack(rows_p, 3), stack(rows_p, 4),
            stack(rows_s, 0), stack(rows_s, 1), stack(rows_s, 2), stack(rows_s, 3), stack(rows_s, 4))
```

```python
import functools
import math

import numpy as np
import jax
import jax.numpy as jnp
from jax import lax
from jax.experimental import pallas as pl
from jax.experimental.pallas import tpu as pltpu

F32 = jnp.float32
BF16 = jnp.bfloat16
HI = lax.Precision.HIGHEST

LANES = 128
HEAD_DIM = 64
DIFF_HALF = 32
N_DIFF_HEADS = 4
N_MOBA_HEADS = 4
N_SOFT_HEADS = 8
N_DN_HEADS = 8
N_PAIRS = 4
SOFT_W = 512
DN_W = 512
D_FF = 3584
MOBA_BLOCK = 256
MOBA_TOPK = 3
DN_CONV = 4
FFN_CONV = 3
EPS = 1e-6
LOG2E = 1.4426950408889634
NEG_BIG = -1e30
Z_W = 3584 + 2 * DN_W
VMEM_LIMIT = 56 * 1024 * 1024


def _cp(n_axes, vmem=VMEM_LIMIT):
    return pltpu.CompilerParams(dimension_semantics=("arbitrary",) * n_axes, vmem_limit_bytes=vmem)


def _sigmoid(x):
    return 1.0 / (1.0 + jnp.exp(-x))


def _silu(x):
    return x * _sigmoid(x)


def _softplus(x):
    return jnp.maximum(x, 0.0) + jnp.log(1.0 + jnp.exp(-jnp.abs(x)))


def _dot(a, b, prec=None):
    return jnp.dot(a, b, preferred_element_type=F32, precision=prec)


def _dot_nt(a, b, prec=None):
    return lax.dot_general(a, b, (((1,), (1,)), ((), ())), preferred_element_type=F32, precision=prec)


def _dot_tn(a, b, prec=None):
    return lax.dot_general(a, b, (((0,), (0,)), ((), ())), preferred_element_type=F32, precision=prec)


def _split2(x):
    hi = x.astype(BF16)
    return hi, (x - hi.astype(F32)).astype(BF16)


def _mm3(a, b, dot=_dot):
    ah, al = _split2(a)
    bh, bl = _split2(b)
    return dot(ah, bh) + dot(ah, bl) + dot(al, bh)


def _mm2(a, b01, dot=_dot):
    ah, al = _split2(a)
    return dot(ah, b01) + dot(al, b01)


def _const_spec(shape):
    nd = len(shape)
    return pl.BlockSpec(shape, lambda *_: (0,) * nd)


def _bd_ones(dtype=F32):
    i = np.arange(LANES)
    return jnp.asarray((i[:, None] // HEAD_DIM == i[None, :] // HEAD_DIM).astype(np.float32)).astype(dtype)


_PROJ_WIDTHS = (512, 512, 512, 1536, 512, 512, 512)


def _proj_in_body(x_ref, g_ref, w_ref, *out_refs):
    x = x_ref[...]
    ms = jnp.mean(x * x, axis=-1, keepdims=True)
    h = (x * lax.rsqrt(ms + EPS) * g_ref[...]).astype(BF16)
    lo = 0
    for o_ref, w in zip(out_refs, _PROJ_WIDTHS):
        o_ref[...] = _dot(h, w_ref[:, lo:lo + w])
        lo += w


def _proj_in(x2d, g, w_perm):
    m, d = x2d.shape
    tm = min(m, 256)
    return pl.pallas_call(
        _proj_in_body,
        grid=(m // tm,),
        in_specs=[pl.BlockSpec((tm, d), lambda i: (i, 0)), _const_spec((1, d)), _const_spec((d, Z_W))],
        out_specs=[pl.BlockSpec((tm, w), lambda i: (i, 0)) for w in _PROJ_WIDTHS],
        out_shape=[jax.ShapeDtypeStruct((m, w), F32) for w in _PROJ_WIDTHS],
        compiler_params=_cp(1),
        name="proj_in",
    )(x2d, g.reshape(1, d), w_perm)


def _permute_w_in(w):
    cols = [w[:, 0:256], w[:, 768:1024], w[:, 256:512], w[:, 1024:1280], w[:, 512:768], w[:, 1280:1536],
            w[:, 1536:3584], jnp.repeat(w[:, 3584:3592], HEAD_DIM, axis=1), jnp.repeat(w[:, 3592:3600], HEAD_DIM, axis=1)]
    return jnp.concatenate(cols, axis=1).astype(BF16)


def _prep_kv(k_ref, v_ref, kb_scr, vt_scr, t, tq):
    kb_scr[...] = k_ref[0].astype(BF16)
    for c in range(t // tq):
        vt_scr[c] = v_ref[0, c * tq:(c + 1) * tq, :].T.astype(BF16)


def _online_stats_t(j, sp, shift, m_scr, l_scr):
    m_old = m_scr[j]
    m_new = jnp.maximum(m_old, jnp.max(sp, axis=0, keepdims=True) + shift)
    p = jnp.exp2(sp - (m_new - shift))
    alpha = jnp.exp2(m_old - m_new)
    l_scr[j] = alpha * l_scr[j] + jnp.sum(p, axis=0, keepdims=True)
    m_scr[j] = m_new
    return p.astype(BF16), alpha


def _diff_attn_body(lam_ref, q_ref, k_ref, v_ref, g_ref, o_ref,
                    kb_scr, vt_scr, qm_scr, sa_scr, sb_scr, m_scr, l_scr, acc_scr, *, t, tq, out_scale):
    p = pl.program_id(1)
    qi = pl.program_id(2)

    @pl.when(qi == 0)
    def _():
        _prep_kv(k_ref, v_ref, kb_scr, vt_scr, t, tq)

    lane = lax.broadcasted_iota(jnp.int32, (tq, LANES), 1)
    q = q_ref[0] * (DIFF_HALF ** -0.5 * LOG2E)
    for j in range(4):
        lo = (j // 2) * HEAD_DIM + (j % 2) * DIFF_HALF
        qm_scr[j] = jnp.where((lane >= lo) & (lane < lo + DIFF_HALF), q, 0.0).astype(BF16)
    m_scr[...] = jnp.full(m_scr.shape, -jnp.inf, F32)
    l_scr[...] = jnp.zeros(l_scr.shape, F32)
    acc_scr[...] = jnp.zeros(acc_scr.shape, F32)
    slopes = [jnp.where(p == 0, LOG2E * 2.0 ** -(2 * hl + 1), LOG2E * 2.0 ** -(2 * (hl + 2) + 1)).astype(F32)
              for hl in range(2)]
    krow = lax.broadcasted_iota(jnp.int32, (tq, tq), 0)
    qcol = lax.broadcasted_iota(jnp.int32, (tq, tq), 1)
    krow_f = krow.astype(F32)
    ramps = [slopes[hl] * krow_f for hl in range(2)]

    def scores(ki, s_ref):
        kt = kb_scr[pl.ds(pl.multiple_of(ki * tq, tq), tq), :]
        for j in range(4):
            s_ref[j] = _dot_nt(kt, qm_scr[j])

    def softmax_pv(ki, s_ref, masked):
        vt = vt_scr[ki]
        off = ((ki - qi) * tq).astype(F32)
        pa = []
        for j in range(4):
            sp = s_ref[j] + ramps[j // 2]
            if masked:
                sp = jnp.where(krow <= qcol, sp, -jnp.inf)
            pa.append(_online_stats_t(j, sp, slopes[j // 2] * off, m_scr, l_scr))
        for j in range(4):
            acc_scr[j] = pa[j][1] * acc_scr[j] + _dot(vt, pa[j][0])

    scores(0, sa_scr)

    def body(i, carry):
        ki = 2 * i
        scores(ki + 1, sb_scr)
        softmax_pv(ki, sa_scr, False)
        scores(ki + 2, sa_scr)
        softmax_pv(ki + 1, sb_scr, False)
        return carry

    lax.fori_loop(0, qi // 2, body, 0)

    @pl.when(qi % 2 == 1)
    def _():
        scores(qi, sb_scr)
        softmax_pv(qi - 1, sa_scr, False)
        softmax_pv(qi, sb_scr, True)

    @pl.when(qi % 2 == 0)
    def _():
        softmax_pv(qi, sa_scr, True)

    lam = lam_ref[0]
    outs = []
    for hl in range(2):
        r = slice(hl * HEAD_DIM, (hl + 1) * HEAD_DIM)
        o = (acc_scr[2 * hl, r, :] * (1.0 / l_scr[2 * hl]) - lam * (acc_scr[2 * hl + 1, r, :] * (1.0 / l_scr[2 * hl + 1])))
        ms = jnp.sum(o * o, axis=0, keepdims=True) * (1.0 / HEAD_DIM)
        outs.append(o * lax.rsqrt(ms + EPS))
    ot = jnp.concatenate(outs, axis=0) * g_ref[...] * out_scale
    o_ref[0] = ot.T


def _diff_attn_prompt(q_soft, k_rows, v_rows, lam, subln_g, lam_init, tq=256):
    n, t, _ = q_soft.shape
    tq = min(tq, t)
    g2 = jnp.tile(subln_g.reshape(HEAD_DIM, 1), (2, 1))
    kern = functools.partial(_diff_attn_body, t=t, tq=tq, out_scale=1.0 - lam_init)
    return pl.pallas_call(
        kern,
        grid=(n, 2, t // tq),
        in_specs=[pl.BlockSpec(memory_space=pltpu.SMEM),
                  pl.BlockSpec((1, tq, LANES), lambda b, p, i: (b, i, p)),
                  pl.BlockSpec((1, t, LANES), lambda b, p, i: (b, 0, p)),
                  pl.BlockSpec((1, t, LANES), lambda b, p, i: (b, 0, p)),
                  _const_spec((LANES, 1))],
        out_specs=pl.BlockSpec((1, tq, LANES), lambda b, p, i: (b, i, p)),
        out_shape=jax.ShapeDtypeStruct((n, t, 2 * LANES), F32),
        scratch_shapes=[pltpu.VMEM((t, LANES), BF16), pltpu.VMEM((t // tq, LANES, tq), BF16),
                        pltpu.VMEM((4, tq, LANES), BF16), pltpu.VMEM((4, tq, tq), F32), pltpu.VMEM((4, tq, tq), F32),
                        pltpu.VMEM((4, 1, tq), F32), pltpu.VMEM((4, 1, tq), F32),
                        pltpu.VMEM((4, LANES, tq), F32)],
        compiler_params=_cp(3),
        name="diff_attn_prompt",
    )(lam.reshape(1), q_soft, k_rows, v_rows, g2)


def _topk_select(gate, idx, n_valid, axis, size):
    gm = jnp.where(idx < n_valid, gate, -jnp.inf)
    sel = jnp.zeros(gate.shape, jnp.bool_)
    for _ in range(MOBA_TOPK):
        mx = jnp.max(gm, axis=axis, keepdims=True)
        is_max = (gm == mx) & (mx > -jnp.inf)
        first = jnp.min(jnp.where(is_max, idx, size), axis=axis, keepdims=True)
        pick = idx == first
        sel = sel | pick
        gm = jnp.where(pick, -jnp.inf, gm)
    return sel


def _moba_body(q_ref, k_ref, v_ref, o_ref, kb_scr, vt_scr, kmean_scr, qs_scr, sel_scr, sd_scr, sa_scr, sb_scr,
               m_scr, l_scr, acc_scr, *, t, tq, nb):
    p = pl.program_id(1)
    qi = pl.program_id(2)

    @pl.when(qi == 0)
    def _():
        _prep_kv(k_ref, v_ref, kb_scr, vt_scr, t, tq)
        kmean_scr[...] = jnp.zeros(kmean_scr.shape, F32)
        for b in range(nb):
            kmean_scr[b:b + 1, :] = jnp.sum(k_ref[0, b * MOBA_BLOCK:(b + 1) * MOBA_BLOCK, :], axis=0,
                                            keepdims=True) * (1.0 / MOBA_BLOCK)

    lane = lax.broadcasted_iota(jnp.int32, (tq, LANES), 1)
    blk = lax.broadcasted_iota(jnp.int32, (LANES, tq), 0)
    q = q_ref[0]
    kmean = kmean_scr[...]
    for hl in range(2):
        qh = jnp.where((lane >= hl * HEAD_DIM) & (lane < (hl + 1) * HEAD_DIM), q, 0.0)
        gate_t = _dot_nt(kmean.astype(BF16), qh.astype(BF16))
        sel = _topk_select(gate_t, blk, qi, 0, LANES)
        sel_scr[hl] = jnp.where(sel, 0.0, NEG_BIG)
        qs_scr[hl] = (qh * (HEAD_DIM ** -0.5 * LOG2E)).astype(BF16)
    m_scr[...] = jnp.full(m_scr.shape, -jnp.inf, F32)
    l_scr[...] = jnp.zeros(l_scr.shape, F32)
    acc_scr[...] = jnp.zeros(acc_scr.shape, F32)
    slopes = [jnp.where(p == 0, LOG2E * 2.0 ** -(2 * hl + 2), LOG2E * 2.0 ** -(2 * (hl + 2) + 2)).astype(F32)
              for hl in range(2)]
    krow = lax.broadcasted_iota(jnp.int32, (tq, tq), 0)
    qcol = lax.broadcasted_iota(jnp.int32, (tq, tq), 1)
    krow_f = krow.astype(F32)
    ramps = [slopes[hl] * krow_f for hl in range(2)]

    def scores(ki, s_ref):
        kt = kb_scr[pl.ds(pl.multiple_of(ki * tq, tq), tq), :]
        for hl in range(2):
            s_ref[hl] = _dot_nt(kt, qs_scr[hl])

    def softmax_pv(ki, s_ref, diagonal):
        vt = vt_scr[ki]
        off = ((ki - qi) * tq).astype(F32)
        pa = []
        for hl in range(2):
            sp = s_ref[hl] + ramps[hl]
            shift = slopes[hl] * off
            if diagonal:
                sp = jnp.where(krow <= qcol, sp, -jnp.inf)
            else:
                shift = shift + sel_scr[hl, pl.ds(ki, 1), :]
            pa.append(_online_stats_t(hl, sp, shift, m_scr, l_scr))
        for hl in range(2):
            acc_scr[hl] = pa[hl][1] * acc_scr[hl] + _dot(vt, pa[hl][0])

    scores(qi, sd_scr)
    scores(0, sa_scr)
    softmax_pv(qi, sd_scr, True)

    def body(i, carry):
        ki = 2 * i
        scores(ki + 1, sb_scr)
        softmax_pv(ki, sa_scr, False)
        scores(ki + 2, sa_scr)
        softmax_pv(ki + 1, sb_scr, False)
        return carry

    lax.fori_loop(0, qi // 2, body, 0)

    @pl.when(qi % 2 == 1)
    def _():
        softmax_pv(qi - 1, sa_scr, False)

    outs = []
    for hl in range(2):
        outs.append(acc_scr[hl, hl * HEAD_DIM:(hl + 1) * HEAD_DIM, :] * (1.0 / l_scr[hl]))
    o_ref[0] = jnp.concatenate(outs, axis=0).T


def _moba_prompt(q_soft, k_rows, v_rows):
    n, t, _ = q_soft.shape
    tq = MOBA_BLOCK
    assert t % tq == 0 and t // tq <= LANES
    nb = t // tq
    kern = functools.partial(_moba_body, t=t, tq=tq, nb=nb)
    return pl.pallas_call(
        kern,
        grid=(n, 2, nb),
        in_specs=[pl.BlockSpec((1, tq, LANES), lambda b, p, i: (b, i, 2 + p)),
                  pl.BlockSpec((1, t, LANES), lambda b, p, i: (b, 0, 2 + p)),
                  pl.BlockSpec((1, t, LANES), lambda b, p, i: (b, 0, 2 + p))],
        out_specs=pl.BlockSpec((1, tq, LANES), lambda b, p, i: (b, i, p)),
        out_shape=jax.ShapeDtypeStruct((n, t, 2 * LANES), F32),
        scratch_shapes=[pltpu.VMEM((t, LANES), BF16), pltpu.VMEM((nb, LANES, tq), BF16),
                        pltpu.VMEM((LANES, LANES), F32),
                        pltpu.VMEM((2, tq, LANES), BF16), pltpu.VMEM((2, LANES, tq), F32),
                        pltpu.VMEM((2, tq, tq), F32), pltpu.VMEM((2, tq, tq), F32), pltpu.VMEM((2, tq, tq), F32),
                        pltpu.VMEM((2, 1, tq), F32), pltpu.VMEM((2, 1, tq), F32),
                        pltpu.VMEM((2, LANES, tq), F32)],
        compiler_params=_cp(3),
        name="moba_prompt",
    )(q_soft, k_rows, v_rows)


def _bdiag(x, lane_lo):
    return jnp.concatenate([jnp.where(lane_lo, x, 0.0), jnp.where(lane_lo, 0.0, x)], axis=0)


def _gdn_prompt_body(qkv_ref, zg_ref, bx_ref, ax_ref, arow_ref, cw_ref, alogx_ref, dtbx_ref, alogr_ref, dtbr_ref,
                     gn_ref, bd_ref, ltri_ref, ubd_ref,
                     o_ref, sfin_ref, xbuf, s_scr, *, c, nseq):
    ci = pl.program_id(0)
    nc = pl.num_programs(0)

    @pl.when(ci == 0)
    def _():
        xbuf[:, 0:8, :] = jnp.zeros((nseq, 8, 3 * DN_W), F32)
        s_scr[...] = jnp.zeros(s_scr.shape, F32)

    ltri = ltri_ref[...]
    ubd = ubd_ref[...]
    bd = bd_ref[...]
    cw = cw_ref[...]
    row = lax.broadcasted_iota(jnp.int32, (c, LANES), 0)
    lane = lax.broadcasted_iota(jnp.int32, (c, LANES), 1)
    col = lane % HEAD_DIM
    lane_lo = lane < HEAD_DIM
    incl = row >= col
    strict = row > col
    eye2 = jnp.where(row == col, 1.0, 0.0)
    r128 = lax.broadcasted_iota(jnp.int32, (LANES, LANES), 0)
    c128 = lax.broadcasted_iota(jnp.int32, (LANES, LANES), 1)
    same_head = (r128 // HEAD_DIM) == (c128 // HEAD_DIM)
    steps = max(1, int(math.ceil(math.log2(c))) - 1)

    def split3_dot(x, lhs01=None, rhs01=None):
        x0, x1 = _split2(x)
        x2 = (x - x0.astype(F32) - x1.astype(F32)).astype(BF16)
        if lhs01 is not None:
            return _dot(lhs01, x0) + _dot(lhs01, x1) + _dot(lhs01, x2)
        return _dot(x0, rhs01) + _dot(x1, rhs01) + _dot(x2, rhs01)

    cq, beta_x, gc_x, gc_row = [], [], [], []
    for b in range(nseq):
        xbuf[b, 8:8 + c, :] = qkv_ref[b]
        y = (cw[3:4] * xbuf[b, 8:8 + c, :] + cw[2:3] * xbuf[b, 7:7 + c, :]
             + cw[1:2] * xbuf[b, 6:6 + c, :] + cw[0:1] * xbuf[b, 5:5 + c, :])
        xbuf[b, 0:8, :] = xbuf[b, c:c + 8, :]
        cq.append(_silu(y))
        beta_x.append(_sigmoid(bx_ref[b]))
        g_x = -jnp.exp(alogx_ref[...]) * _softplus(ax_ref[b] + dtbx_ref[...])
        g_row = -jnp.exp(alogr_ref[...]) * _softplus(arow_ref[b, 0] + dtbr_ref[...])
        gc_x.append(split3_dot(g_x, lhs01=ltri))
        gc_row.append(split3_dot(g_row, rhs01=ubd))

    chains = [(b, p) for b in range(nseq) for p in range(N_PAIRS)]
    pair = lambda arr, p, off=0: arr[:, off + p * LANES:off + (p + 1) * LANES]
    qp = [pair(cq[b], p) for b, p in chains]
    kp = [pair(cq[b], p, DN_W) for b, p in chains]
    vp = [pair(cq[b], p, 2 * DN_W) for b, p in chains]
    bexp = [pair(beta_x[b], p) for b, p in chains]
    gcx = [pair(gc_x[b], p) for b, p in chains]
    gcr = [gc_row[b][p:p + 1, :] for b, p in chains]
    n_ch = len(chains)
    rng = range(n_ch)

    ssq_q = [_mm2(qp[i] * qp[i], bd) for i in rng]
    ssq_k = [_mm2(kp[i] * kp[i], bd) for i in rng]
    qn = [qp[i] * lax.rsqrt(ssq_q[i] + EPS) * (HEAD_DIM ** -0.5) for i in rng]
    kn = [kp[i] * lax.rsqrt(ssq_k[i] + EPS) for i in rng]
    decay = [jnp.exp(jnp.where(incl, gcx[i] - gcr[i], -jnp.inf)) for i in rng]
    kq = [_mm3(jnp.concatenate([kn[i], qn[i]], axis=0), _bdiag(kn[i], lane_lo), _dot_nt) for i in rng]
    qk = [jnp.where(incl, kq[i][c:2 * c] * decay[i], 0.0) for i in rng]
    bpow = [-jnp.where(strict, bexp[i] * kq[i][0:c] * decay[i], 0.0) for i in rng]
    x = [eye2 + bpow[i] for i in rng]
    for _ in range(steps):
        bpow = [_mm3(bpow[i], _bdiag(bpow[i], lane_lo)) for i in rng]
        x = [x[i] + _mm3(x[i], _bdiag(bpow[i], lane_lo)) for i in rng]
    eg = [jnp.exp(gcx[i]) for i in rng]
    uw = [_mm3(x[i], jnp.concatenate([_bdiag(vp[i] * bexp[i], lane_lo), _bdiag(kn[i] * bexp[i] * eg[i], lane_lo)], axis=1))
          for i in rng]
    s_old = [s_scr[b, p] for b, p in chains]
    ws = [_mm3(jnp.concatenate([uw[i][:, LANES:2 * LANES], qn[i] * eg[i]], axis=0), s_old[i]) for i in rng]
    v_new = [uw[i][:, 0:LANES] - ws[i][0:c] for i in rng]
    glast = [gcx[i][c - 1:c, :] for i in rng]
    intra = [_mm3(qk[i], _bdiag(v_new[i], lane_lo)) for i in rng]
    upd = [_mm3(kn[i] * jnp.exp(glast[i] - gcx[i]), v_new[i], _dot_tn) for i in rng]
    o = [ws[i][c:2 * c] + intra[i] for i in rng]
    ms = [_mm2(o[i] * o[i], bd) * (1.0 / HEAD_DIM) for i in rng]
    for i, (b, p) in enumerate(chains):
        sl = slice(p * LANES, (p + 1) * LANES)
        s_scr[b, p] = s_old[i] * jnp.exp(glast[i]) + jnp.where(same_head, upd[i], 0.0)
        o_ref[b, :, sl] = o[i] * lax.rsqrt(ms[i] + EPS) * gn_ref[...] * _silu(zg_ref[b, :, sl])

    @pl.when(ci == nc - 1)
    def _():
        sfin_ref[...] = s_scr[...]


def _gdn_prompt(qkv, zg, b_x, a_x, conv_w, a_log, dt_bias, norm_g, c=64):
    n, t, _ = qkv.shape
    nc = t // c
    a_raw = a_x[..., ::HEAD_DIM]
    a_row = a_raw.reshape(n, nc, c, N_PAIRS, 2).transpose(0, 1, 3, 4, 2).reshape(n, nc, N_PAIRS, 2 * c)
    rep = lambda v: jnp.repeat(v.reshape(N_PAIRS, 2), c, axis=1)
    lanes = lambda v: jnp.repeat(v, HEAD_DIM).reshape(1, DN_W)
    i = np.arange(c)
    ltri = jnp.asarray((i[:, None] >= i[None, :]).astype(np.float32)).astype(BF16)
    j = np.arange(2 * c)
    ubd = jnp.asarray(((j[:, None] // c == j[None, :] // c) & (j[:, None] % c <= j[None, :] % c)).astype(np.float32))
    assert 2 * c == LANES
    kern = functools.partial(_gdn_prompt_body, c=c, nseq=n)
    tok = lambda w: pl.BlockSpec((n, c, w), lambda i: (0, i, 0))
    return pl.pallas_call(
        kern,
        grid=(nc,),
        in_specs=[tok(3 * DN_W), tok(DN_W), tok(DN_W), tok(DN_W),
                  pl.BlockSpec((n, 1, N_PAIRS, LANES), lambda i: (0, i, 0, 0)),
                  _const_spec((DN_CONV, 3 * DN_W)),
                  _const_spec((1, DN_W)), _const_spec((1, DN_W)),
                  _const_spec((N_PAIRS, LANES)), _const_spec((N_PAIRS, LANES)),
                  _const_spec((1, LANES)), _const_spec((LANES, LANES)),
                  _const_spec((c, c)), _const_spec((LANES, LANES))],
        out_specs=[tok(DN_W), _const_spec((n, N_PAIRS, LANES, LANES))],
        out_shape=[jax.ShapeDtypeStruct((n, t, DN_W), F32),
                   jax.ShapeDtypeStruct((n, N_PAIRS, LANES, LANES), F32)],
        scratch_shapes=[pltpu.VMEM((n, c + 8, 3 * DN_W), F32), pltpu.VMEM((n, N_PAIRS, LANES, LANES), F32)],
        compiler_params=_cp(1),
        name="gdn_prompt",
    )(qkv, zg, b_x, a_x, a_row, conv_w, lanes(a_log), lanes(dt_bias),
      rep(a_log), rep(dt_bias), jnp.tile(norm_g.reshape(1, HEAD_DIM), (1, 2)), _bd_ones(BF16), ltri, ubd.astype(BF16))


def _state_to_bd(s):
    n = s.shape[0]
    s = s.reshape(n, N_PAIRS, 2, HEAD_DIM, HEAD_DIM)
    z = jnp.zeros_like(s[:, :, 0])
    top = jnp.concatenate([s[:, :, 0], z], axis=-1)
    bot = jnp.concatenate([z, s[:, :, 1]], axis=-1)
    return jnp.concatenate([top, bot], axis=-2)


def _state_from_bd(sbd):
    n = sbd.shape[0]
    s = jnp.stack([sbd[:, :, :HEAD_DIM, :HEAD_DIM], sbd[:, :, HEAD_DIM:, HEAD_DIM:]], axis=2)
    return s.reshape(n, N_DN_HEADS, HEAD_DIM, HEAD_DIM)


def _gdn_decode_body(qkv_ref, cb_ref, zg_ref, bx_ref, ax_ref, cw_ref, alogx_ref, dtbx_ref, gn_ref, bd_ref,
                     s_ref, o_ref, snew_ref, cnew_ref):
    x = qkv_ref[...]
    buf = cb_ref[...]
    cw = cw_ref[...]
    y = cw[0:1] * buf[0:1] + cw[1:2] * buf[1:2] + cw[2:3] * buf[2:3] + cw[3:4] * x
    cnew_ref[...] = jnp.concatenate([buf[1:3], x], axis=0)
    cq = _silu(y)
    bexp = _sigmoid(bx_ref[...])
    eg = jnp.exp(-jnp.exp(alogx_ref[...]) * _softplus(ax_ref[...] + dtbx_ref[...]))
    bd = bd_ref[...]
    r128 = lax.broadcasted_iota(jnp.int32, (LANES, LANES), 0)
    c128 = lax.broadcasted_iota(jnp.int32, (LANES, LANES), 1)
    eye = r128 == c128
    rows8 = lambda v: jnp.broadcast_to(v, (8, v.shape[-1]))
    for p in range(N_PAIRS):
        sl = slice(p * LANES, (p + 1) * LANES)
        qp = cq[:, p * LANES:(p + 1) * LANES]
        kp = cq[:, DN_W + p * LANES:DN_W + (p + 1) * LANES]
        vp = cq[:, 2 * DN_W + p * LANES:2 * DN_W + (p + 1) * LANES]
        qn = qp * lax.rsqrt(_dot(rows8(qp * qp), bd, HI)[0:1] + EPS) * (HEAD_DIM ** -0.5)
        kn = kp * lax.rsqrt(_dot(rows8(kp * kp), bd, HI)[0:1] + EPS)
        s = s_ref[p] * eg[:, sl]
        ks = _dot(rows8(kn), s, HI)[0:1]
        delta = (vp - ks) * bexp[:, sl]
        kcol = _dot(jnp.where(eye, jnp.broadcast_to(kn, (LANES, LANES)), 0.0), bd, HI)
        s = s + kcol * delta
        snew_ref[p] = s
        o = _dot(rows8(qn), s, HI)[0:1]
        ms = _dot(rows8(o * o), bd, HI)[0:1] * (1.0 / HEAD_DIM)
        o_ref[:, sl] = o * lax.rsqrt(ms + EPS) * gn_ref[...] * _silu(zg_ref[:, sl])


def _gdn_decode(qkv, zg, b_x, a_x, conv_buf, s_bd, conv_w, a_log, dt_bias, norm_g):
    ns = qkv.shape[0]
    row = lambda w: pl.BlockSpec((None, 1, w), lambda i: (i, 0, 0))
    lanes = lambda v: jnp.repeat(v, HEAD_DIM).reshape(1, DN_W)
    return pl.pallas_call(
        _gdn_decode_body,
        grid=(ns,),
        in_specs=[row(3 * DN_W), pl.BlockSpec((None, DN_CONV - 1, 3 * DN_W), lambda i: (i, 0, 0)), row(DN_W),
                  row(DN_W), row(DN_W),
                  _const_spec((DN_CONV, 3 * DN_W)), _const_spec((1, DN_W)), _const_spec((1, DN_W)),
                  _const_spec((1, LANES)), _const_spec((LANES, LANES)),
                  pl.BlockSpec((None, N_PAIRS, LANES, LANES), lambda i: (i, 0, 0, 0))],
        out_specs=[row(DN_W), pl.BlockSpec((None, N_PAIRS, LANES, LANES), lambda i: (i, 0, 0, 0)),
                   pl.BlockSpec((None, DN_CONV - 1, 3 * DN_W), lambda i: (i, 0, 0))],
        out_shape=[jax.ShapeDtypeStruct((ns, 1, DN_W), F32),
                   jax.ShapeDtypeStruct((ns, N_PAIRS, LANES, LANES), F32),
                   jax.ShapeDtypeStruct((ns, DN_CONV - 1, 3 * DN_W), F32)],
        compiler_params=_cp(1),
        name="gdn_decode",
    )(qkv[:, None, :], conv_buf, zg[:, None, :], b_x[:, None, :], a_x[:, None, :], conv_w, lanes(a_log), lanes(dt_bias),
      jnp.tile(norm_g.reshape(1, HEAD_DIM), (1, 2)), _bd_ones(), s_bd)


def _rms(x, g):
    return x * lax.rsqrt(jnp.mean(x * x, axis=-1, keepdims=True) + EPS) * g


def _ffn_prompt_body(x_ref, oa_ref, ob_ref, oc_ref, wo_ref, g2_ref, wg_ref, wu_ref, cw_ref, cb_ref, wd_ref, gf_ref,
                     y_ref, st_ref, acc, h2, gbuf, carry, *, tm, final):
    i = pl.program_id(1)
    j = pl.program_id(2)
    nj = pl.num_programs(2)

    @pl.when(j == 0)
    def _():
        x1 = (x_ref[0] + _dot(oa_ref[0].astype(BF16), wo_ref[0:256, :]) + _dot(ob_ref[0].astype(BF16), wo_ref[256:512, :])
              + _dot(oc_ref[0].astype(BF16), wo_ref[512:1024, :]))
        acc[...] = x1
        h2[...] = _rms(x1, g2_ref[...]).astype(BF16)

    @pl.when(i == 0)
    def _():
        carry[j] = jnp.zeros(carry.shape[1:], F32)

    h = h2[...]
    g = _dot(h, wg_ref[...])
    u = _dot(h, wu_ref[...])
    gbuf[0:8, :] = carry[j]
    gbuf[8:8 + tm, :] = g
    cw = cw_ref[...]
    gc = cw[2:3] * g + cw[1:2] * gbuf[7:7 + tm, :] + cw[0:1] * gbuf[6:6 + tm, :] + cb_ref[...]
    carry[j] = gbuf[tm:tm + 8, :]
    st_ref[0, j] = gbuf[tm + 6:tm + 8, :]
    act = (_silu(gc) * u).astype(BF16)
    acc[...] += _dot(act, wd_ref[...])

    @pl.when(j == nj - 1)
    def _():
        if final:
            y_ref[0] = _rms(acc[...], gf_ref[...])
        else:
            y_ref[0] = acc[...]


def _ffn_prompt(x, o_a, o_b, o_c, w_out, norm2_g, w_ffn_in, conv_w, conv_b, w_ffn_out, normf_g, final, tm=512, tf=512):
    n, t, d = x.shape
    tm = min(tm, t)
    nj = D_FF // tf
    kern = functools.partial(_ffn_prompt_body, tm=tm, final=final)
    rows = lambda w: pl.BlockSpec((1, tm, w), lambda b, i, j: (b, i, 0))
    y, st = pl.pallas_call(
        kern,
        grid=(n, t // tm, nj),
        in_specs=[rows(d), rows(256), rows(256), rows(512),
                  _const_spec((d, d)), _const_spec((1, d)),
                  pl.BlockSpec((d, tf), lambda b, i, j: (0, j)),
                  pl.BlockSpec((d, tf), lambda b, i, j: (0, j + nj)),
                  pl.BlockSpec((FFN_CONV, tf), lambda b, i, j: (0, j)),
                  pl.BlockSpec((1, tf), lambda b, i, j: (0, j)),
                  pl.BlockSpec((tf, d), lambda b, i, j: (j, 0)),
                  _const_spec((1, d))],
        out_specs=[rows(d), pl.BlockSpec((1, nj, FFN_CONV - 1, tf), lambda b, i, j: (b, 0, 0, 0))],
        out_shape=[jax.ShapeDtypeStruct((n, t, d), F32), jax.ShapeDtypeStruct((n, nj, FFN_CONV - 1, tf), F32)],
        scratch_shapes=[pltpu.VMEM((tm, d), F32), pltpu.VMEM((tm, d), BF16), pltpu.VMEM((tm + 8, tf), F32),
                        pltpu.VMEM((nj, 8, tf), F32)],
        compiler_params=_cp(3),
        name="ffn_prompt",
    )(x, o_a, o_b, o_c, w_out, norm2_g.reshape(1, d), w_ffn_in, w_ffn_in, conv_w, conv_b.reshape(1, D_FF),
      w_ffn_out, normf_g.reshape(1, d))
    return y, st.transpose(0, 2, 1, 3).reshape(n, FFN_CONV - 1, D_FF)


def _ffn_decode_body(x_ref, oa_ref, ob_ref, oc_ref, wo_ref, g2_ref, wg_ref, wu_ref, cw_ref, cb_ref, wd_ref, gf_ref,
                     prev_ref, y_ref, st_ref, acc, h2, *, final):
    j = pl.program_id(0)
    nj = pl.num_programs(0)

    @pl.when(j == 0)
    def _():
        x1 = (x_ref[...] + _dot(oa_ref[...].astype(BF16), wo_ref[0:256, :]) + _dot(ob_ref[...].astype(BF16), wo_ref[256:512, :])
              + _dot(oc_ref[...].astype(BF16), wo_ref[512:1024, :]))
        acc[...] = x1
        h2[...] = _rms(x1, g2_ref[...]).astype(BF16)

    h = h2[...]
    g = _dot(h, wg_ref[...])
    u = _dot(h, wu_ref[...])
    cw = cw_ref[...]
    gc = cw[2:3] * g + cw[1:2] * prev_ref[1] + cw[0:1] * prev_ref[0] + cb_ref[...]
    st_ref[0] = prev_ref[1]
    st_ref[1] = g
    act = (_silu(gc) * u).astype(BF16)
    acc[...] += _dot(act, wd_ref[...])

    @pl.when(j == nj - 1)
    def _():
        if final:
            y_ref[...] = _rms(acc[...], gf_ref[...])
        else:
            y_ref[...] = acc[...]


def _ffn_decode(x, o_a, o_b, o_c, prev, w_out, norm2_g, w_ffn_in, conv_w, conv_b, w_ffn_out, normf_g, final, tf=512):
    ns, d = x.shape
    nj = D_FF // tf
    kern = functools.partial(_ffn_decode_body, final=final)
    return pl.pallas_call(
        kern,
        grid=(nj,),
        in_specs=[_const_spec((ns, d)), _const_spec((ns, 256)), _const_spec((ns, 256)), _const_spec((ns, 512)),
                  _const_spec((d, d)), _const_spec((1, d)),
                  pl.BlockSpec((d, tf), lambda j: (0, j)),
                  pl.BlockSpec((d, tf), lambda j: (0, j + nj)),
                  pl.BlockSpec((FFN_CONV, tf), lambda j: (0, j)),
                  pl.BlockSpec((1, tf), lambda j: (0, j)),
                  pl.BlockSpec((tf, d), lambda j: (j, 0)),
                  _const_spec((1, d)),
                  pl.BlockSpec((FFN_CONV - 1, ns, tf), lambda j: (0, 0, j))],
        out_specs=[_const_spec((ns, d)), pl.BlockSpec((FFN_CONV - 1, ns, tf), lambda j: (0, 0, j))],
        out_shape=[jax.ShapeDtypeStruct((ns, d), F32), jax.ShapeDtypeStruct((FFN_CONV - 1, ns, D_FF), F32)],
        scratch_shapes=[pltpu.VMEM((ns, d), F32), pltpu.VMEM((ns, d), BF16)],
        compiler_params=_cp(1),
        name="ffn_decode",
    )(x, o_a, o_b, o_c, w_out, norm2_g.reshape(1, d), w_ffn_in, w_ffn_in, conv_w, conv_b.reshape(1, D_FF),
      w_ffn_out, normf_g.reshape(1, d), prev)


ROWS = 16
PAGES_PER_STEP = 4


def _decode_consts():
    j = np.arange(ROWS)
    is_diff = j < 8
    is_moba = (j >= 8) & (j < 12)
    head = np.where(is_diff, j // 2, np.where(is_moba, j - 4, 0))
    slope = np.where(is_diff, 2.0 ** -(2 * (j // 2) + 1), np.where(is_moba, 2.0 ** -(2 * (j - 8) + 2), 0.0)) * LOG2E
    onehot = ((np.arange(N_SOFT_HEADS)[None, :] == head[:, None]) & (is_diff | is_moba)[:, None]).astype(np.float32)
    coef_a = np.zeros((N_SOFT_HEADS, ROWS), np.float32)
    coef_b = np.zeros((N_SOFT_HEADS, ROWS), np.float32)
    for h in range(N_DIFF_HEADS):
        coef_a[h, 2 * h] = 1.0
        coef_b[h, 2 * h + 1] = -1.0
    for h in range(N_MOBA_HEADS):
        coef_a[N_DIFF_HEADS + h, 8 + h] = 1.0
    bc = lambda a, dt: jnp.asarray(np.broadcast_to(a[:, None], (ROWS, LANES)).astype(dt))
    return bc(head, np.int32), bc(slope, np.float32), jnp.asarray(onehot), jnp.asarray(coef_a), jnp.asarray(coef_b)


def _decode_qmat(q_soft):
    ns = q_soft.shape[0]
    qa = q_soft[:, 0:256].reshape(ns, N_DIFF_HEADS, 1, HEAD_DIM)
    half = jnp.asarray((np.arange(HEAD_DIM)[None, :] // DIFF_HALF == np.arange(2)[:, None]).astype(np.float32))
    qd = (qa * half[None, None] * (DIFF_HALF ** -0.5 * LOG2E)).reshape(ns, 8, HEAD_DIM)
    qm = q_soft[:, 256:512].reshape(ns, N_MOBA_HEADS, HEAD_DIM)
    pad = jnp.zeros((ns, ROWS - 12, HEAD_DIM), F32)
    scaled = jnp.concatenate([qd, qm * (HEAD_DIM ** -0.5 * LOG2E), pad], axis=1)
    raw = jnp.concatenate([jnp.zeros((ns, 8, HEAD_DIM), F32), qm, pad], axis=1)
    return scaled, raw


def _decode_attn_body(pt_ref, lam_ref, q_ref, qg_ref, ks_ref, vs_ref, *rest, nblk, npg, page, past_len, out_scale):
    del pt_ref
    k_refs = rest[0:npg]
    v_refs = rest[npg:2 * npg]
    head_ref, slope_ref, oh_ref, ca_ref, cb_ref, g_ref, o_ref = rest[2 * npg:2 * npg + 7]
    m_scr, l_scr, acc_scr, mblk, lblk, gblk, accblk = rest[2 * npg + 7:]
    b = pl.program_id(1)
    nstep = pl.num_programs(1)
    tk = 2 * page
    ncol = tk * N_SOFT_HEADS

    @pl.when(b == 0)
    def _():
        m_scr[...] = jnp.full(m_scr.shape, -jnp.inf, F32)
        l_scr[...] = jnp.zeros(l_scr.shape, F32)
        acc_scr[...] = jnp.zeros(acc_scr.shape, F32)
        mblk[...] = jnp.full(mblk.shape, -jnp.inf, F32)
        lblk[...] = jnp.zeros(lblk.shape, F32)
        gblk[...] = jnp.full(gblk.shape, -jnp.inf, F32)

    qm = q_ref[...]
    qmb = qm.astype(BF16)
    col = lax.broadcasted_iota(jnp.int32, (ROWS, ncol), 1)
    own = (col & (N_SOFT_HEADS - 1)) == head_ref[:, 0:1]
    tok = lax.broadcasted_iota(jnp.int32, (1, ncol), 1) >> 3
    slope = slope_ref[:, 0:1]
    lane = lax.broadcasted_iota(jnp.int32, (ROWS, LANES), 1)
    oh = oh_ref[...]

    nu = npg // 2
    flat = lambda r: r[...].reshape(page * N_SOFT_HEADS, HEAD_DIM)
    raw = [_dot_nt(qmb, jnp.concatenate([flat(k_refs[2 * u]), flat(k_refs[2 * u + 1])], axis=0).astype(BF16))
           for u in range(nu)]
    mbs, lbs, pes = [], [], []
    for u in range(nu):
        dist = (past_len - ((b * nu + u) * tk + tok)).astype(F32)
        s = jnp.where(own, raw[u] - slope * dist, -jnp.inf)
        mb = jnp.max(s, axis=-1, keepdims=True)
        pexp = jnp.exp2(s - mb)
        mbs.append(mb)
        lbs.append(jnp.sum(pexp, axis=-1, keepdims=True))
        pes.append(pexp.astype(BF16))
    accbs = [_dot(pes[u], jnp.concatenate([flat(v_refs[2 * u]), flat(v_refs[2 * u + 1])], axis=0).astype(BF16))
             for u in range(nu)]
    kmeans = [(jnp.sum(k_refs[2 * u][...], axis=0) + jnp.sum(k_refs[2 * u + 1][...], axis=0)) * (1.0 / tk)
              for u in range(nu)]
    qgb = qg_ref[...].astype(BF16)
    gates = [jnp.sum(_dot_nt(qgb, kmeans[u].astype(BF16)) * oh, axis=-1, keepdims=True) for u in range(nu)]

    for u in range(nu):
        bb = b * nu + u
        mb, lb, accb = mbs[u], lbs[u], accbs[u]
        m_old = m_scr[:, 0:1]
        m_new = jnp.maximum(m_old, mb)
        a_old = jnp.exp2(m_old - m_new)
        a_blk = jnp.exp2(mb - m_new)
        l_scr[...] = jnp.broadcast_to(a_old * l_scr[:, 0:1] + a_blk * lb, l_scr.shape)
        acc_scr[...] = a_old * acc_scr[...] + a_blk * accb
        m_scr[...] = jnp.broadcast_to(m_new, m_scr.shape)
        hit = lane == bb
        mblk[...] = jnp.where(hit, mb, mblk[...])
        lblk[...] = jnp.where(hit, lb, lblk[...])
        gblk[...] = jnp.where(hit, gates[u], gblk[...])
        accblk[bb] = accb

    @pl.when(b == nstep - 1)
    def _():
        s_self = jnp.sum(_dot_nt(qm, ks_ref[...], HI) * oh, axis=-1, keepdims=True)
        vs = _dot(oh, vs_ref[...], HI)
        m_o = m_scr[:, 0:1]
        m_d = jnp.maximum(m_o, s_self)
        a_o = jnp.exp2(m_o - m_d)
        p_d = jnp.exp2(s_self - m_d)
        o_d = (a_o * acc_scr[...] + p_d * vs) / (a_o * l_scr[:, 0:1] + p_d)
        sel = _topk_select(gblk[...], lane, nblk, 1, LANES)
        mm = mblk[...]
        m_f = jnp.maximum(jnp.max(jnp.where(sel, mm, -jnp.inf), axis=-1, keepdims=True), s_self)
        wgt = jnp.where(sel, jnp.exp2(mm - m_f), 0.0)
        p_m = jnp.exp2(s_self - m_f)
        l_m = jnp.sum(wgt * lblk[...], axis=-1, keepdims=True) + p_m
        acc_m = p_m * vs
        for blk in range(nblk):
            acc_m = acc_m + wgt[:, blk:blk + 1] * accblk[blk]
        o_m = acc_m / l_m
        rowi = lax.broadcasted_iota(jnp.int32, (ROWS, HEAD_DIM), 0)
        o_all = jnp.where(rowi < 8, o_d, o_m)
        o8 = _dot(ca_ref[...] + lam_ref[0] * cb_ref[...], o_all, HI)
        ms = jnp.mean(o8 * o8, axis=-1, keepdims=True)
        o_n = o8 * lax.rsqrt(ms + EPS) * g_ref[...] * out_scale
        row8 = lax.broadcasted_iota(jnp.int32, (N_SOFT_HEADS, HEAD_DIM), 0)
        o_ref[...] = jnp.where(row8 < N_DIFF_HEADS, o_n, o8)


def _decode_attn(layer, q_soft, k_self, v_self, cache_k, cache_v, page_table, lam, subln_g, lam_init):
    ns = q_soft.shape[0]
    n_pages = page_table.shape[1]
    page = cache_k.shape[2]
    npg = PAGES_PER_STEP
    assert 2 * page == MOBA_BLOCK and n_pages % npg == 0
    nblk = n_pages // 2
    assert nblk <= LANES
    kern = functools.partial(_decode_attn_body, nblk=nblk, npg=npg, page=page, past_len=n_pages * page,
                             out_scale=1.0 - lam_init)
    hd = lambda r: pl.BlockSpec((None, r, HEAD_DIM), lambda s, b, pt: (s, 0, 0))
    pg = lambda off: pl.BlockSpec((None, None, page, N_SOFT_HEADS, HEAD_DIM),
                                  lambda s, b, pt: (layer, pt[s, npg * b + off], 0, 0, 0))
    cst = lambda shp: pl.BlockSpec(shp, lambda s, b, pt: (0,) * len(shp))
    grid_spec = pltpu.PrefetchScalarGridSpec(
        num_scalar_prefetch=1,
        grid=(ns, n_pages // npg),
        in_specs=([pl.BlockSpec(memory_space=pltpu.SMEM), hd(ROWS), hd(ROWS), hd(N_SOFT_HEADS), hd(N_SOFT_HEADS)]
                  + [pg(u) for u in range(npg)] + [pg(u) for u in range(npg)]
                  + [cst((ROWS, LANES)), cst((ROWS, LANES)), cst((ROWS, N_SOFT_HEADS)), cst((N_SOFT_HEADS, ROWS)),
                     cst((N_SOFT_HEADS, ROWS)), cst((1, HEAD_DIM))]),
        out_specs=hd(N_SOFT_HEADS),
        scratch_shapes=[pltpu.VMEM((ROWS, LANES), F32), pltpu.VMEM((ROWS, LANES), F32), pltpu.VMEM((ROWS, HEAD_DIM), F32),
                        pltpu.VMEM((ROWS, LANES), F32), pltpu.VMEM((ROWS, LANES), F32), pltpu.VMEM((ROWS, LANES), F32),
                        pltpu.VMEM((nblk, ROWS, HEAD_DIM), F32)],
    )
    out = pl.pallas_call(
        kern,
        grid_spec=grid_spec,
        out_shape=jax.ShapeDtypeStruct((ns, N_SOFT_HEADS, HEAD_DIM), F32),
        compiler_params=_cp(2),
        name="decode_attn",
    )(page_table, lam.reshape(1), *_decode_qmat(q_soft), k_self.reshape(ns, N_SOFT_HEADS, HEAD_DIM),
      v_self.reshape(ns, N_SOFT_HEADS, HEAD_DIM), *([cache_k] * npg), *([cache_v] * npg), *_decode_consts(),
      subln_g.reshape(1, HEAD_DIM))
    return out.reshape(ns, SOFT_W)


def _lam(l, lam_q1, lam_k1, lam_q2, lam_k2):
    lam_init = 0.8 - 0.6 * math.exp(-0.3 * l)
    lam = (jnp.exp(jnp.sum(lam_q1[l] * lam_k1[l])) - jnp.exp(jnp.sum(lam_q2[l] * lam_k2[l])) + lam_init)
    return lam.astype(F32), lam_init


def _prompt_layer(x, w, lam, lam_init, final):
    n, t, d = x.shape
    outs = _proj_in(x.reshape(n * t, d), w["norm1"], w["w_in"])
    qs, kr, vr, qkv, zg, b_x, a_x = [a.reshape(n, t, a.shape[-1]) for a in outs]
    o_a = _diff_attn_prompt(qs, kr, vr, lam, w["subln"], lam_init)
    o_b = _moba_prompt(qs, kr, vr)
    o_c, s_bd = _gdn_prompt(qkv, zg, b_x, a_x, w["dn_conv_w"], w["dn_a_log"], w["dn_dt_bias"], w["dn_norm"])
    y, ffn_new = _ffn_prompt(x, o_a, o_b, o_c, w["w_out"], w["norm2"], w["ffn_in"], w["ffn_conv_w"], w["ffn_conv_b"],
                             w["ffn_out"], w["normf"], final)
    rows = (kr.reshape(n, t, 8, HEAD_DIM), vr.reshape(n, t, 8, HEAD_DIM), _state_from_bd(s_bd),
            qkv[:, t - (DN_CONV - 1):, :], ffn_new)
    return y, rows


def _sample_layer(l, x, w, lam, lam_init, final, cache_k, cache_v, page_table, state_dn, conv_qkv, conv_ffn):
    ns, d = x.shape
    qs, kr, vr, qkv, zg, b_x, a_x = _proj_in(x, w["norm1"], w["w_in"])
    o_ab = _decode_attn(l, qs, kr, vr, cache_k, cache_v, page_table, lam, w["subln"], lam_init)
    o_c, s_bd, conv_new = _gdn_decode(qkv, zg, b_x, a_x, conv_qkv, _state_to_bd(state_dn), w["dn_conv_w"], w["dn_a_log"],
                                      w["dn_dt_bias"], w["dn_norm"])
    y, ffn_new = _ffn_decode(x, o_ab[:, 0:256], o_ab[:, 256:512], o_c[:, 0, :], jnp.swapaxes(conv_ffn, 0, 1),
                             w["w_out"], w["norm2"], w["ffn_in"], w["ffn_conv_w"], w["ffn_conv_b"], w["ffn_out"],
                             w["normf"], final)
    rows = (kr.reshape(ns, 1, 8, HEAD_DIM), vr.reshape(ns, 1, 8, HEAD_DIM), _state_from_bd(s_bd), conv_new,
            jnp.swapaxes(ffn_new, 0, 1))
    return y, rows


def kernel(x_prompt, x_sample, cache_k, cache_v, page_table, state_dn, state_conv_qkv, state_conv_ffn, norm1_g, norm2_g, normf_g, w_in, w_out, lam_q1, lam_k1, lam_q2, lam_k2, subln_g, dn_conv_w, dn_a_log, dn_dt_bias, dn_norm_g, ffn_w_in, ffn_conv_w, ffn_conv_b, ffn_w_out):
    depth = w_in.shape[0]
    xp = x_prompt
    xs = x_sample.reshape(x_sample.shape[0], x_sample.shape[2])
    rows_p, rows_s = [], []
    for l in range(depth):
        w = {"norm1": norm1_g[l], "w_in": _permute_w_in(w_in[l]), "subln": subln_g[l], "dn_conv_w": dn_conv_w[l],
             "dn_a_log": dn_a_log[l], "dn_dt_bias": dn_dt_bias[l], "dn_norm": dn_norm_g[l],
             "w_out": w_out[l].astype(BF16), "norm2": norm2_g[l], "ffn_in": ffn_w_in[l].astype(BF16),
             "ffn_conv_w": ffn_conv_w[l], "ffn_conv_b": ffn_conv_b[l], "ffn_out": ffn_w_out[l].astype(BF16),
             "normf": normf_g}
        lam, lam_init = _lam(l, lam_q1, lam_k1, lam_q2, lam_k2)
        final = l == depth - 1
        xp, rp = _prompt_layer(xp, w, lam, lam_init, final)
        xs, rs = _sample_layer(l, xs, w, lam, lam_init, final, cache_k, cache_v, page_table, state_dn[l],
                               state_conv_qkv[l], state_conv_ffn[l])
        rows_p.append(rp)
        rows_s.append(rs)
    stack = lambda rows, i: jnp.stack([r[i] for r in rows], axis=0)
    y_sample = xs.reshape(x_sample.shape)
    return (xp, y_sample,
            stack(rows_p, 0), stack(rows_p, 1), stack(rows_p, 2), stack(rows_p, 3), stack(rows_p, 4),
            stack(rows_s, 0), stack(rows_s, 1), stack(rows_s, 2), stack(rows_s, 3), stack(rows_s, 4))
```

```python
import functools
import math

import numpy as np
import jax
import jax.numpy as jnp
from jax import lax
from jax.experimental import pallas as pl
from jax.experimental.pallas import tpu as pltpu

F32 = jnp.float32
BF16 = jnp.bfloat16
HI = lax.Precision.HIGHEST

LANES = 128
HEAD_DIM = 64
DIFF_HALF = 32
N_DIFF_HEADS = 4
N_MOBA_HEADS = 4
N_SOFT_HEADS = 8
N_DN_HEADS = 8
N_PAIRS = 4
SOFT_W = 512
DN_W = 512
D_FF = 3584
MOBA_BLOCK = 256
MOBA_TOPK = 3
DN_CONV = 4
FFN_CONV = 3
EPS = 1e-6
LOG2E = 1.4426950408889634
NEG_BIG = -1e30
Z_W = 3584 + 2 * DN_W
VMEM_LIMIT = 56 * 1024 * 1024


def _cp(n_axes, vmem=VMEM_LIMIT):
    return pltpu.CompilerParams(dimension_semantics=("arbitrary",) * n_axes, vmem_limit_bytes=vmem)


def _sigmoid(x):
    return 1.0 / (1.0 + jnp.exp(-x))


def _silu(x):
    return x * _sigmoid(x)


def _softplus(x):
    return jnp.maximum(x, 0.0) + jnp.log(1.0 + jnp.exp(-jnp.abs(x)))


def _dot(a, b, prec=None):
    return jnp.dot(a, b, preferred_element_type=F32, precision=prec)


def _dot_nt(a, b, prec=None):
    return lax.dot_general(a, b, (((1,), (1,)), ((), ())), preferred_element_type=F32, precision=prec)


def _dot_tn(a, b, prec=None):
    return lax.dot_general(a, b, (((0,), (0,)), ((), ())), preferred_element_type=F32, precision=prec)


def _split2(x):
    hi = x.astype(BF16)
    return hi, (x - hi.astype(F32)).astype(BF16)


def _mm3(a, b, dot=_dot):
    ah, al = _split2(a)
    bh, bl = _split2(b)
    return dot(ah, bh) + dot(ah, bl) + dot(al, bh)


def _mm2(a, b01, dot=_dot):
    ah, al = _split2(a)
    return dot(ah, b01) + dot(al, b01)


def _const_spec(shape):
    nd = len(shape)
    return pl.BlockSpec(shape, lambda *_: (0,) * nd)


def _bd_ones(dtype=F32):
    i = np.arange(LANES)
    return jnp.asarray((i[:, None] // HEAD_DIM == i[None, :] // HEAD_DIM).astype(np.float32)).astype(dtype)


_PROJ_WIDTHS = (512, 512, 512, 1536, 512, 512, 512)


def _proj_in_body(x_ref, g_ref, w_ref, *out_refs):
    x = x_ref[...]
    ms = jnp.mean(x * x, axis=-1, keepdims=True)
    h = (x * lax.rsqrt(ms + EPS) * g_ref[...]).astype(BF16)
    lo = 0
    for o_ref, w in zip(out_refs, _PROJ_WIDTHS):
        o_ref[...] = _dot(h, w_ref[:, lo:lo + w])
        lo += w


def _proj_in(x2d, g, w_perm):
    m, d = x2d.shape
    tm = min(m, 256)
    return pl.pallas_call(
        _proj_in_body,
        grid=(m // tm,),
        in_specs=[pl.BlockSpec((tm, d), lambda i: (i, 0)), _const_spec((1, d)), _const_spec((d, Z_W))],
        out_specs=[pl.BlockSpec((tm, w), lambda i: (i, 0)) for w in _PROJ_WIDTHS],
        out_shape=[jax.ShapeDtypeStruct((m, w), F32) for w in _PROJ_WIDTHS],
        compiler_params=_cp(1),
        name="proj_in",
    )(x2d, g.reshape(1, d), w_perm)


def _permute_w_in(w):
    cols = [w[:, 0:256], w[:, 768:1024], w[:, 256:512], w[:, 1024:1280], w[:, 512:768], w[:, 1280:1536],
            w[:, 1536:3584], jnp.repeat(w[:, 3584:3592], HEAD_DIM, axis=1), jnp.repeat(w[:, 3592:3600], HEAD_DIM, axis=1)]
    return jnp.concatenate(cols, axis=1).astype(BF16)


def _prep_kv(k_ref, v_ref, kb_scr, vt_scr, t, tq):
    kb_scr[...] = k_ref[0].astype(BF16)
    for c in range(t // tq):
        vt_scr[c] = v_ref[0, c * tq:(c + 1) * tq, :].T.astype(BF16)


def _online_stats_t(j, sp, shift, m_scr, l_scr):
    m_old = m_scr[j]
    m_new = jnp.maximum(m_old, jnp.max(sp, axis=0, keepdims=True) + shift)
    p = jnp.exp2(sp - (m_new - shift))
    alpha = jnp.exp2(m_old - m_new)
    l_scr[j] = alpha * l_scr[j] + jnp.sum(p, axis=0, keepdims=True)
    m_scr[j] = m_new
    return p.astype(BF16), alpha


def _diff_attn_body(lam_ref, q_ref, k_ref, v_ref, g_ref, o_ref,
                    kb_scr, vt_scr, qm_scr, sa_scr, sb_scr, m_scr, l_scr, acc_scr, *, t, tq, out_scale):
    p = pl.program_id(1)
    qi = pl.program_id(2)

    @pl.when(qi == 0)
    def _():
        _prep_kv(k_ref, v_ref, kb_scr, vt_scr, t, tq)

    lane = lax.broadcasted_iota(jnp.int32, (tq, LANES), 1)
    q = q_ref[0] * (DIFF_HALF ** -0.5 * LOG2E)
    for j in range(4):
        lo = (j // 2) * HEAD_DIM + (j % 2) * DIFF_HALF
        qm_scr[j] = jnp.where((lane >= lo) & (lane < lo + DIFF_HALF), q, 0.0).astype(BF16)
    m_scr[...] = jnp.full(m_scr.shape, -jnp.inf, F32)
    l_scr[...] = jnp.zeros(l_scr.shape, F32)
    acc_scr[...] = jnp.zeros(acc_scr.shape, F32)
    slopes = [jnp.where(p == 0, LOG2E * 2.0 ** -(2 * hl + 1), LOG2E * 2.0 ** -(2 * (hl + 2) + 1)).astype(F32)
              for hl in range(2)]
    krow = lax.broadcasted_iota(jnp.int32, (tq, tq), 0)
    qcol = lax.broadcasted_iota(jnp.int32, (tq, tq), 1)
    krow_f = krow.astype(F32)
    ramps = [slopes[hl] * krow_f for hl in range(2)]

    def scores(ki, s_ref):
        kt = kb_scr[pl.ds(pl.multiple_of(ki * tq, tq), tq), :]
        for j in range(4):
            s_ref[j] = _dot_nt(kt, qm_scr[j])

    def softmax_pv(ki, s_ref, masked):
        vt = vt_scr[ki]
        off = ((ki - qi) * tq).astype(F32)
        pa = []
        for j in range(4):
            sp = s_ref[j] + ramps[j // 2]
            if masked:
                sp = jnp.where(krow <= qcol, sp, -jnp.inf)
            pa.append(_online_stats_t(j, sp, slopes[j // 2] * off, m_scr, l_scr))
        for j in range(4):
            acc_scr[j] = pa[j][1] * acc_scr[j] + _dot(vt, pa[j][0])

    scores(0, sa_scr)

    def body(i, carry):
        ki = 2 * i
        scores(ki + 1, sb_scr)
        softmax_pv(ki, sa_scr, False)
        scores(ki + 2, sa_scr)
        softmax_pv(ki + 1, sb_scr, False)
        return carry

    lax.fori_loop(0, qi // 2, body, 0)

    @pl.when(qi % 2 == 1)
    def _():
        scores(qi, sb_scr)
        softmax_pv(qi - 1, sa_scr, False)
        softmax_pv(qi, sb_scr, True)

    @pl.when(qi % 2 == 0)
    def _():
        softmax_pv(qi, sa_scr, True)

    lam = lam_ref[0]
    outs = []
    for hl in range(2):
        r = slice(hl * HEAD_DIM, (hl + 1) * HEAD_DIM)
        o = (acc_scr[2 * hl, r, :] * (1.0 / l_scr[2 * hl]) - lam * (acc_scr[2 * hl + 1, r, :] * (1.0 / l_scr[2 * hl + 1])))
        ms = jnp.sum(o * o, axis=0, keepdims=True) * (1.0 / HEAD_DIM)
        outs.append(o * lax.rsqrt(ms + EPS))
    ot = jnp.concatenate(outs, axis=0) * g_ref[...] * out_scale
    o_ref[0] = ot.T


def _diff_attn_prompt(q_soft, k_rows, v_rows, lam, subln_g, lam_init, tq=256):
    n, t, _ = q_soft.shape
    tq = min(tq, t)
    g2 = jnp.tile(subln_g.reshape(HEAD_DIM, 1), (2, 1))
    kern = functools.partial(_diff_attn_body, t=t, tq=tq, out_scale=1.0 - lam_init)
    return pl.pallas_call(
        kern,
        grid=(n, 2, t // tq),
        in_specs=[pl.BlockSpec(memory_space=pltpu.SMEM),
                  pl.BlockSpec((1, tq, LANES), lambda b, p, i: (b, i, p)),
                  pl.BlockSpec((1, t, LANES), lambda b, p, i: (b, 0, p)),
                  pl.BlockSpec((1, t, LANES), lambda b, p, i: (b, 0, p)),
                  _const_spec((LANES, 1))],
        out_specs=pl.BlockSpec((1, tq, LANES), lambda b, p, i: (b, i, p)),
        out_shape=jax.ShapeDtypeStruct((n, t, 2 * LANES), F32),
        scratch_shapes=[pltpu.VMEM((t, LANES), BF16), pltpu.VMEM((t // tq, LANES, tq), BF16),
                        pltpu.VMEM((4, tq, LANES), BF16), pltpu.VMEM((4, tq, tq), F32), pltpu.VMEM((4, tq, tq), F32),
                        pltpu.VMEM((4, 1, tq), F32), pltpu.VMEM((4, 1, tq), F32),
                        pltpu.VMEM((4, LANES, tq), F32)],
        compiler_params=_cp(3),
        name="diff_attn_prompt",
    )(lam.reshape(1), q_soft, k_rows, v_rows, g2)


def _topk_select(gate, idx, n_valid, axis, size):
    gm = jnp.where(idx < n_valid, gate, -jnp.inf)
    sel = jnp.zeros(gate.shape, jnp.bool_)
    for _ in range(MOBA_TOPK):
        mx = jnp.max(gm, axis=axis, keepdims=True)
        is_max = (gm == mx) & (mx > -jnp.inf)
        first = jnp.min(jnp.where(is_max, idx, size), axis=axis, keepdims=True)
        pick = idx == first
        sel = sel | pick
        gm = jnp.where(pick, -jnp.inf, gm)
    return sel


def _moba_body(q_ref, k_ref, v_ref, o_ref, kb_scr, vt_scr, kmean_scr, qs_scr, sel_scr, sd_scr, sa_scr, sb_scr,
               m_scr, l_scr, acc_scr, *, t, tq, nb):
    p = pl.program_id(1)
    qi = pl.program_id(2)

    @pl.when(qi == 0)
    def _():
        _prep_kv(k_ref, v_ref, kb_scr, vt_scr, t, tq)
        kmean_scr[...] = jnp.zeros(kmean_scr.shape, F32)
        for b in range(nb):
            kmean_scr[b:b + 1, :] = jnp.sum(k_ref[0, b * MOBA_BLOCK:(b + 1) * MOBA_BLOCK, :], axis=0,
                                            keepdims=True) * (1.0 / MOBA_BLOCK)

    lane = lax.broadcasted_iota(jnp.int32, (tq, LANES), 1)
    blk = lax.broadcasted_iota(jnp.int32, (LANES, tq), 0)
    q = q_ref[0]
    kmean = kmean_scr[...]
    for hl in range(2):
        qh = jnp.where((lane >= hl * HEAD_DIM) & (lane < (hl + 1) * HEAD_DIM), q, 0.0)
        gate_t = _dot_nt(kmean.astype(BF16), qh.astype(BF16))
        sel = _topk_select(gate_t, blk, qi, 0, LANES)
        sel_scr[hl] = jnp.where(sel, 0.0, NEG_BIG)
        qs_scr[hl] = (qh * (HEAD_DIM ** -0.5 * LOG2E)).astype(BF16)
    m_scr[...] = jnp.full(m_scr.shape, -jnp.inf, F32)
    l_scr[...] = jnp.zeros(l_scr.shape, F32)
    acc_scr[...] = jnp.zeros(acc_scr.shape, F32)
    slopes = [jnp.where(p == 0, LOG2E * 2.0 ** -(2 * hl + 2), LOG2E * 2.0 ** -(2 * (hl + 2) + 2)).astype(F32)
              for hl in range(2)]
    krow = lax.broadcasted_iota(jnp.int32, (tq, tq), 0)
    qcol = lax.broadcasted_iota(jnp.int32, (tq, tq), 1)
    krow_f = krow.astype(F32)
    ramps = [slopes[hl] * krow_f for hl in range(2)]

    def scores(ki, s_ref):
        kt = kb_scr[pl.ds(pl.multiple_of(ki * tq, tq), tq), :]
        for hl in range(2):
            s_ref[hl] = _dot_nt(kt, qs_scr[hl])

    def softmax_pv(ki, s_ref, diagonal):
        vt = vt_scr[ki]
        off = ((ki - qi) * tq).astype(F32)
        pa = []
        for hl in range(2):
            sp = s_ref[hl] + ramps[hl]
            shift = slopes[hl] * off
            if diagonal:
                sp = jnp.where(krow <= qcol, sp, -jnp.inf)
            else:
                shift = shift + sel_scr[hl, pl.ds(ki, 1), :]
            pa.append(_online_stats_t(hl, sp, shift, m_scr, l_scr))
        for hl in range(2):
            acc_scr[hl] = pa[hl][1] * acc_scr[hl] + _dot(vt, pa[hl][0])

    scores(qi, sd_scr)
    scores(0, sa_scr)
    softmax_pv(qi, sd_scr, True)

    def body(i, carry):
        ki = 2 * i
        scores(ki + 1, sb_scr)
        softmax_pv(ki, sa_scr, False)
        scores(ki + 2, sa_scr)
        softmax_pv(ki + 1, sb_scr, False)
        return carry

    lax.fori_loop(0, qi // 2, body, 0)

    @pl.when(qi % 2 == 1)
    def _():
        softmax_pv(qi - 1, sa_scr, False)

    outs = []
    for hl in range(2):
        outs.append(acc_scr[hl, hl * HEAD_DIM:(hl + 1) * HEAD_DIM, :] * (1.0 / l_scr[hl]))
    o_ref[0] = jnp.concatenate(outs, axis=0).T


def _moba_prompt(q_soft, k_rows, v_rows):
    n, t, _ = q_soft.shape
    tq = MOBA_BLOCK
    assert t % tq == 0 and t // tq <= LANES
    nb = t // tq
    kern = functools.partial(_moba_body, t=t, tq=tq, nb=nb)
    return pl.pallas_call(
        kern,
        grid=(n, 2, nb),
        in_specs=[pl.BlockSpec((1, tq, LANES), lambda b, p, i: (b, i, 2 + p)),
                  pl.BlockSpec((1, t, LANES), lambda b, p, i: (b, 0, 2 + p)),
                  pl.BlockSpec((1, t, LANES), lambda b, p, i: (b, 0, 2 + p))],
        out_specs=pl.BlockSpec((1, tq, LANES), lambda b, p, i: (b, i, p)),
        out_shape=jax.ShapeDtypeStruct((n, t, 2 * LANES), F32),
        scratch_shapes=[pltpu.VMEM((t, LANES), BF16), pltpu.VMEM((nb, LANES, tq), BF16),
                        pltpu.VMEM((LANES, LANES), F32),
                        pltpu.VMEM((2, tq, LANES), BF16), pltpu.VMEM((2, LANES, tq), F32),
                        pltpu.VMEM((2, tq, tq), F32), pltpu.VMEM((2, tq, tq), F32), pltpu.VMEM((2, tq, tq), F32),
                        pltpu.VMEM((2, 1, tq), F32), pltpu.VMEM((2, 1, tq), F32),
                        pltpu.VMEM((2, LANES, tq), F32)],
        compiler_params=_cp(3),
        name="moba_prompt",
    )(q_soft, k_rows, v_rows)


def _bdiag(x, lane_lo):
    return jnp.concatenate([jnp.where(lane_lo, x, 0.0), jnp.where(lane_lo, 0.0, x)], axis=0)


def _gdn_prompt_body(qkv_ref, zg_ref, bx_ref, ax_ref, arow_ref, cw_ref, alogx_ref, dtbx_ref, alogr_ref, dtbr_ref,
                     gn_ref, bd_ref, ltri_ref, ubd_ref,
                     o_ref, sfin_ref, xbuf, s_scr, *, c, nseq):
    ci = pl.program_id(0)
    nc = pl.num_programs(0)

    @pl.when(ci == 0)
    def _():
        xbuf[:, 0:8, :] = jnp.zeros((nseq, 8, 3 * DN_W), F32)
        s_scr[...] = jnp.zeros(s_scr.shape, F32)

    ltri = ltri_ref[...]
    ubd = ubd_ref[...]
    bd = bd_ref[...]
    cw = cw_ref[...]
    row = lax.broadcasted_iota(jnp.int32, (c, LANES), 0)
    lane = lax.broadcasted_iota(jnp.int32, (c, LANES), 1)
    col = lane % HEAD_DIM
    lane_lo = lane < HEAD_DIM
    incl = row >= col
    strict = row > col
    eye2 = jnp.where(row == col, 1.0, 0.0)
    r128 = lax.broadcasted_iota(jnp.int32, (LANES, LANES), 0)
    c128 = lax.broadcasted_iota(jnp.int32, (LANES, LANES), 1)
    same_head = (r128 // HEAD_DIM) == (c128 // HEAD_DIM)
    steps = max(1, int(math.ceil(math.log2(c))) - 1)

    def split3_dot(x, lhs01=None, rhs01=None):
        x0, x1 = _split2(x)
        x2 = (x - x0.astype(F32) - x1.astype(F32)).astype(BF16)
        if lhs01 is not None:
            return _dot(lhs01, x0) + _dot(lhs01, x1) + _dot(lhs01, x2)
        return _dot(x0, rhs01) + _dot(x1, rhs01) + _dot(x2, rhs01)

    cq, beta_x, gc_x, gc_row = [], [], [], []
    for b in range(nseq):
        xbuf[b, 8:8 + c, :] = qkv_ref[b]
        y = (cw[3:4] * xbuf[b, 8:8 + c, :] + cw[2:3] * xbuf[b, 7:7 + c, :]
             + cw[1:2] * xbuf[b, 6:6 + c, :] + cw[0:1] * xbuf[b, 5:5 + c, :])
        xbuf[b, 0:8, :] = xbuf[b, c:c + 8, :]
        cq.append(_silu(y))
        beta_x.append(_sigmoid(bx_ref[b]))
        g_x = -jnp.exp(alogx_ref[...]) * _softplus(ax_ref[b] + dtbx_ref[...])
        g_row = -jnp.exp(alogr_ref[...]) * _softplus(arow_ref[b, 0] + dtbr_ref[...])
        gc_x.append(split3_dot(g_x, lhs01=ltri))
        gc_row.append(split3_dot(g_row, rhs01=ubd))

    chains = [(b, p) for b in range(nseq) for p in range(N_PAIRS)]
    pair = lambda arr, p, off=0: arr[:, off + p * LANES:off + (p + 1) * LANES]
    qp = [pair(cq[b], p) for b, p in chains]
    kp = [pair(cq[b], p, DN_W) for b, p in chains]
    vp = [pair(cq[b], p, 2 * DN_W) for b, p in chains]
    bexp = [pair(beta_x[b], p) for b, p in chains]
    gcx = [pair(gc_x[b], p) for b, p in chains]
    gcr = [gc_row[b][p:p + 1, :] for b, p in chains]
    n_ch = len(chains)
    rng = range(n_ch)

    ssq_q = [_mm2(qp[i] * qp[i], bd) for i in rng]
    ssq_k = [_mm2(kp[i] * kp[i], bd) for i in rng]
    qn = [qp[i] * lax.rsqrt(ssq_q[i] + EPS) * (HEAD_DIM ** -0.5) for i in rng]
    kn = [kp[i] * lax.rsqrt(ssq_k[i] + EPS) for i in rng]
    decay = [jnp.exp(jnp.where(incl, gcx[i] - gcr[i], -jnp.inf)) for i in rng]
    kq = [_mm3(jnp.concatenate([kn[i], qn[i]], axis=0), _bdiag(kn[i], lane_lo), _dot_nt) for i in rng]
    qk = [jnp.where(incl, kq[i][c:2 * c] * decay[i], 0.0) for i in rng]
    bpow = [-jnp.where(strict, bexp[i] * kq[i][0:c] * decay[i], 0.0) for i in rng]
    x = [eye2 + bpow[i] for i in rng]
    for _ in range(steps):
        bpow = [_mm3(bpow[i], _bdiag(bpow[i], lane_lo)) for i in rng]
        x = [x[i] + _mm3(x[i], _bdiag(bpow[i], lane_lo)) for i in rng]
    eg = [jnp.exp(gcx[i]) for i in rng]
    uw = [_mm3(x[i], jnp.concatenate([_bdiag(vp[i] * bexp[i], lane_lo), _bdiag(kn[i] * bexp[i] * eg[i], lane_lo)], axis=1))
          for i in rng]
    s_old = [s_scr[b, p] for b, p in chains]
    ws = [_mm3(jnp.concatenate([uw[i][:, LANES:2 * LANES], qn[i] * eg[i]], axis=0), s_old[i]) for i in rng]
    v_new = [uw[i][:, 0:LANES] - ws[i][0:c] for i in rng]
    glast = [gcx[i][c - 1:c, :] for i in rng]
    intra = [_mm3(qk[i], _bdiag(v_new[i], lane_lo)) for i in rng]
    upd = [_mm3(kn[i] * jnp.exp(glast[i] - gcx[i]), v_new[i], _dot_tn) for i in rng]
    o = [ws[i][c:2 * c] + intra[i] for i in rng]
    ms = [_mm2(o[i] * o[i], bd) * (1.0 / HEAD_DIM) for i in rng]
    for i, (b, p) in enumerate(chains):
        sl = slice(p * LANES, (p + 1) * LANES)
        s_scr[b, p] = s_old[i] * jnp.exp(glast[i]) + jnp.where(same_head, upd[i], 0.0)
        o_ref[b, :, sl] = o[i] * lax.rsqrt(ms[i] + EPS) * gn_ref[...] * _silu(zg_ref[b, :, sl])

    @pl.when(ci == nc - 1)
    def _():
        sfin_ref[...] = s_scr[...]


def _gdn_prompt(qkv, zg, b_x, a_x, conv_w, a_log, dt_bias, norm_g, c=64):
    n, t, _ = qkv.shape
    nc = t // c
    a_raw = a_x[..., ::HEAD_DIM]
    a_row = a_raw.reshape(n, nc, c, N_PAIRS, 2).transpose(0, 1, 3, 4, 2).reshape(n, nc, N_PAIRS, 2 * c)
    rep = lambda v: jnp.repeat(v.reshape(N_PAIRS, 2), c, axis=1)
    lanes = lambda v: jnp.repeat(v, HEAD_DIM).reshape(1, DN_W)
    i = np.arange(c)
    ltri = jnp.asarray((i[:, None] >= i[None, :]).astype(np.float32)).astype(BF16)
    j = np.arange(2 * c)
    ubd = jnp.asarray(((j[:, None] // c == j[None, :] // c) & (j[:, None] % c <= j[None, :] % c)).astype(np.float32))
    assert 2 * c == LANES
    kern = functools.partial(_gdn_prompt_body, c=c, nseq=n)
    tok = lambda w: pl.BlockSpec((n, c, w), lambda i: (0, i, 0))
    return pl.pallas_call(
        kern,
        grid=(nc,),
        in_specs=[tok(3 * DN_W), tok(DN_W), tok(DN_W), tok(DN_W),
                  pl.BlockSpec((n, 1, N_PAIRS, LANES), lambda i: (0, i, 0, 0)),
                  _const_spec((DN_CONV, 3 * DN_W)),
                  _const_spec((1, DN_W)), _const_spec((1, DN_W)),
                  _const_spec((N_PAIRS, LANES)), _const_spec((N_PAIRS, LANES)),
                  _const_spec((1, LANES)), _const_spec((LANES, LANES)),
                  _const_spec((c, c)), _const_spec((LANES, LANES))],
        out_specs=[tok(DN_W), _const_spec((n, N_PAIRS, LANES, LANES))],
        out_shape=[jax.ShapeDtypeStruct((n, t, DN_W), F32),
                   jax.ShapeDtypeStruct((n, N_PAIRS, LANES, LANES), F32)],
        scratch_shapes=[pltpu.VMEM((n, c + 8, 3 * DN_W), F32), pltpu.VMEM((n, N_PAIRS, LANES, LANES), F32)],
        compiler_params=_cp(1),
        name="gdn_prompt",
    )(qkv, zg, b_x, a_x, a_row, conv_w, lanes(a_log), lanes(dt_bias),
      rep(a_log), rep(dt_bias), jnp.tile(norm_g.reshape(1, HEAD_DIM), (1, 2)), _bd_ones(BF16), ltri, ubd.astype(BF16))


def _state_to_bd(s):
    n = s.shape[0]
    s = s.reshape(n, N_PAIRS, 2, HEAD_DIM, HEAD_DIM)
    z = jnp.zeros_like(s[:, :, 0])
    top = jnp.concatenate([s[:, :, 0], z], axis=-1)
    bot = jnp.concatenate([z, s[:, :, 1]], axis=-1)
    return jnp.concatenate([top, bot], axis=-2)


def _state_from_bd(sbd):
    n = sbd.shape[0]
    s = jnp.stack([sbd[:, :, :HEAD_DIM, :HEAD_DIM], sbd[:, :, HEAD_DIM:, HEAD_DIM:]], axis=2)
    return s.reshape(n, N_DN_HEADS, HEAD_DIM, HEAD_DIM)


def _gdn_decode_body(qkv_ref, cb_ref, zg_ref, bx_ref, ax_ref, cw_ref, alogx_ref, dtbx_ref, gn_ref, bd_ref,
                     s_ref, o_ref, snew_ref, cnew_ref):
    x = qkv_ref[...]
    buf = cb_ref[...]
    cw = cw_ref[...]
    y = cw[0:1] * buf[0:1] + cw[1:2] * buf[1:2] + cw[2:3] * buf[2:3] + cw[3:4] * x
    cnew_ref[...] = jnp.concatenate([buf[1:3], x], axis=0)
    cq = _silu(y)
    bexp = _sigmoid(bx_ref[...])
    eg = jnp.exp(-jnp.exp(alogx_ref[...]) * _softplus(ax_ref[...] + dtbx_ref[...]))
    bd = bd_ref[...]
    r128 = lax.broadcasted_iota(jnp.int32, (LANES, LANES), 0)
    c128 = lax.broadcasted_iota(jnp.int32, (LANES, LANES), 1)
    eye = r128 == c128
    rows8 = lambda v: jnp.broadcast_to(v, (8, v.shape[-1]))
    for p in range(N_PAIRS):
        sl = slice(p * LANES, (p + 1) * LANES)
        qp = cq[:, p * LANES:(p + 1) * LANES]
        kp = cq[:, DN_W + p * LANES:DN_W + (p + 1) * LANES]
        vp = cq[:, 2 * DN_W + p * LANES:2 * DN_W + (p + 1) * LANES]
        qn = qp * lax.rsqrt(_dot(rows8(qp * qp), bd, HI)[0:1] + EPS) * (HEAD_DIM ** -0.5)
        kn = kp * lax.rsqrt(_dot(rows8(kp * kp), bd, HI)[0:1] + EPS)
        s = s_ref[p] * eg[:, sl]
        ks = _dot(rows8(kn), s, HI)[0:1]
        delta = (vp - ks) * bexp[:, sl]
        kcol = _dot(jnp.where(eye, jnp.broadcast_to(kn, (LANES, LANES)), 0.0), bd, HI)
        s = s + kcol * delta
        snew_ref[p] = s
        o = _dot(rows8(qn), s, HI)[0:1]
        ms = _dot(rows8(o * o), bd, HI)[0:1] * (1.0 / HEAD_DIM)
        o_ref[:, sl] = o * lax.rsqrt(ms + EPS) * gn_ref[...] * _silu(zg_ref[:, sl])


def _gdn_decode(qkv, zg, b_x, a_x, conv_buf, s_bd, conv_w, a_log, dt_bias, norm_g):
    ns = qkv.shape[0]
    row = lambda w: pl.BlockSpec((None, 1, w), lambda i: (i, 0, 0))
    lanes = lambda v: jnp.repeat(v, HEAD_DIM).reshape(1, DN_W)
    return pl.pallas_call(
        _gdn_decode_body,
        grid=(ns,),
        in_specs=[row(3 * DN_W), pl.BlockSpec((None, DN_CONV - 1, 3 * DN_W), lambda i: (i, 0, 0)), row(DN_W),
                  row(DN_W), row(DN_W),
                  _const_spec((DN_CONV, 3 * DN_W)), _const_spec((1, DN_W)), _const_spec((1, DN_W)),
                  _const_spec((1, LANES)), _const_spec((LANES, LANES)),
                  pl.BlockSpec((None, N_PAIRS, LANES, LANES), lambda i: (i, 0, 0, 0))],
        out_specs=[row(DN_W), pl.BlockSpec((None, N_PAIRS, LANES, LANES), lambda i: (i, 0, 0, 0)),
                   pl.BlockSpec((None, DN_CONV - 1, 3 * DN_W), lambda i: (i, 0, 0))],
        out_shape=[jax.ShapeDtypeStruct((ns, 1, DN_W), F32),
                   jax.ShapeDtypeStruct((ns, N_PAIRS, LANES, LANES), F32),
                   jax.ShapeDtypeStruct((ns, DN_CONV - 1, 3 * DN_W), F32)],
        compiler_params=_cp(1),
        name="gdn_decode",
    )(qkv[:, None, :], conv_buf, zg[:, None, :], b_x[:, None, :], a_x[:, None, :], conv_w, lanes(a_log), lanes(dt_bias),
      jnp.tile(norm_g.reshape(1, HEAD_DIM), (1, 2)), _bd_ones(), s_bd)


def _rms(x, g):
    return x * lax.rsqrt(jnp.mean(x * x, axis=-1, keepdims=True) + EPS) * g


def _ffn_prompt_body(x_ref, oa_ref, ob_ref, oc_ref, wo_ref, g2_ref, wg_ref, wu_ref, cw_ref, cb_ref, wd_ref, gf_ref,
                     y_ref, st_ref, acc, h2, gbuf, carry, *, tm, final):
    i = pl.program_id(1)
    j = pl.program_id(2)
    nj = pl.num_programs(2)

    @pl.when(j == 0)
    def _():
        x1 = (x_ref[0] + _dot(oa_ref[0].astype(BF16), wo_ref[0:256, :]) + _dot(ob_ref[0].astype(BF16), wo_ref[256:512, :])
              + _dot(oc_ref[0].astype(BF16), wo_ref[512:1024, :]))
        acc[...] = x1
        h2[...] = _rms(x1, g2_ref[...]).astype(BF16)

    @pl.when(i == 0)
    def _():
        carry[j] = jnp.zeros(carry.shape[1:], F32)

    h = h2[...]
    g = _dot(h, wg_ref[...])
    u = _dot(h, wu_ref[...])
    gbuf[0:8, :] = carry[j]
    gbuf[8:8 + tm, :] = g
    cw = cw_ref[...]
    gc = cw[2:3] * g + cw[1:2] * gbuf[7:7 + tm, :] + cw[0:1] * gbuf[6:6 + tm, :] + cb_ref[...]
    carry[j] = gbuf[tm:tm + 8, :]
    st_ref[0, j] = gbuf[tm + 6:tm + 8, :]
    act = (_silu(gc) * u).astype(BF16)
    acc[...] += _dot(act, wd_ref[...])

    @pl.when(j == nj - 1)
    def _():
        if final:
            y_ref[0] = _rms(acc[...], gf_ref[...])
        else:
            y_ref[0] = acc[...]


def _ffn_prompt(x, o_a, o_b, o_c, w_out, norm2_g, w_ffn_in, conv_w, conv_b, w_ffn_out, normf_g, final, tm=512, tf=512):
    n, t, d = x.shape
    tm = min(tm, t)
    nj = D_FF // tf
    kern = functools.partial(_ffn_prompt_body, tm=tm, final=final)
    rows = lambda w: pl.BlockSpec((1, tm, w), lambda b, i, j: (b, i, 0))
    y, st = pl.pallas_call(
        kern,
        grid=(n, t // tm, nj),
        in_specs=[rows(d), rows(256), rows(256), rows(512),
                  _const_spec((d, d)), _const_spec((1, d)),
                  pl.BlockSpec((d, tf), lambda b, i, j: (0, j)),
                  pl.BlockSpec((d, tf), lambda b, i, j: (0, j + nj)),
                  pl.BlockSpec((FFN_CONV, tf), lambda b, i, j: (0, j)),
                  pl.BlockSpec((1, tf), lambda b, i, j: (0, j)),
                  pl.BlockSpec((tf, d), lambda b, i, j: (j, 0)),
                  _const_spec((1, d))],
        out_specs=[rows(d), pl.BlockSpec((1, nj, FFN_CONV - 1, tf), lambda b, i, j: (b, 0, 0, 0))],
        out_shape=[jax.ShapeDtypeStruct((n, t, d), F32), jax.ShapeDtypeStruct((n, nj, FFN_CONV - 1, tf), F32)],
        scratch_shapes=[pltpu.VMEM((tm, d), F32), pltpu.VMEM((tm, d), BF16), pltpu.VMEM((tm + 8, tf), F32),
                        pltpu.VMEM((nj, 8, tf), F32)],
        compiler_params=_cp(3),
        name="ffn_prompt",
    )(x, o_a, o_b, o_c, w_out, norm2_g.reshape(1, d), w_ffn_in, w_ffn_in, conv_w, conv_b.reshape(1, D_FF),
      w_ffn_out, normf_g.reshape(1, d))
    return y, st.transpose(0, 2, 1, 3).reshape(n, FFN_CONV - 1, D_FF)


def _ffn_decode_body(x_ref, oa_ref, ob_ref, oc_ref, wo_ref, g2_ref, wg_ref, wu_ref, cw_ref, cb_ref, wd_ref, gf_ref,
                     prev_ref, y_ref, st_ref, acc, h2, *, final):
    j = pl.program_id(0)
    nj = pl.num_programs(0)

    @pl.when(j == 0)
    def _():
        x1 = (x_ref[...] + _dot(oa_ref[...].astype(BF16), wo_ref[0:256, :]) + _dot(ob_ref[...].astype(BF16), wo_ref[256:512, :])
              + _dot(oc_ref[...].astype(BF16), wo_ref[512:1024, :]))
        acc[...] = x1
        h2[...] = _rms(x1, g2_ref[...]).astype(BF16)

    h = h2[...]
    g = _dot(h, wg_ref[...])
    u = _dot(h, wu_ref[...])
    cw = cw_ref[...]
    gc = cw[2:3] * g + cw[1:2] * prev_ref[1] + cw[0:1] * prev_ref[0] + cb_ref[...]
    st_ref[0] = prev_ref[1]
    st_ref[1] = g
    act = (_silu(gc) * u).astype(BF16)
    acc[...] += _dot(act, wd_ref[...])

    @pl.when(j == nj - 1)
    def _():
        if final:
            y_ref[...] = _rms(acc[...], gf_ref[...])
        else:
            y_ref[...] = acc[...]


def _ffn_decode(x, o_a, o_b, o_c, prev, w_out, norm2_g, w_ffn_in, conv_w, conv_b, w_ffn_out, normf_g, final, tf=512):
    ns, d = x.shape
    nj = D_FF // tf
    kern = functools.partial(_ffn_decode_body, final=final)
    return pl.pallas_call(
        kern,
        grid=(nj,),
        in_specs=[_const_spec((ns, d)), _const_spec((ns, 256)), _const_spec((ns, 256)), _const_spec((ns, 512)),
                  _const_spec((d, d)), _const_spec((1, d)),
                  pl.BlockSpec((d, tf), lambda j: (0, j)),
                  pl.BlockSpec((d, tf), lambda j: (0, j + nj)),
                  pl.BlockSpec((FFN_CONV, tf), lambda j: (0, j)),
                  pl.BlockSpec((1, tf), lambda j: (0, j)),
                  pl.BlockSpec((tf, d), lambda j: (j, 0)),
                  _const_spec((1, d)),
                  pl.BlockSpec((FFN_CONV - 1, ns, tf), lambda j: (0, 0, j))],
        out_specs=[_const_spec((ns, d)), pl.BlockSpec((FFN_CONV - 1, ns, tf), lambda j: (0, 0, j))],
        out_shape=[jax.ShapeDtypeStruct((ns, d), F32), jax.ShapeDtypeStruct((FFN_CONV - 1, ns, D_FF), F32)],
        scratch_shapes=[pltpu.VMEM((ns, d), F32), pltpu.VMEM((ns, d), BF16)],
        compiler_params=_cp(1),
        name="ffn_decode",
    )(x, o_a, o_b, o_c, w_out, norm2_g.reshape(1, d), w_ffn_in, w_ffn_in, conv_w, conv_b.reshape(1, D_FF),
      w_ffn_out, normf_g.reshape(1, d), prev)


ROWS = 16
PAGES_PER_STEP = 4


def _decode_consts():
    j = np.arange(ROWS)[:, None]
    lane = np.arange(SOFT_W)[None, :]
    is_diff = j < 8
    is_moba = (j >= 8) & (j < 12)
    q_lo = np.where(is_diff, (j // 2) * HEAD_DIM + (j % 2) * DIFF_HALF, 256 + (j - 8) * HEAD_DIM)
    q_w = np.where(is_diff, DIFF_HALF, HEAD_DIM)
    qmask = (lane >= q_lo) & (lane < q_lo + q_w) & (is_diff | is_moba)
    qscale = np.where(is_diff, DIFF_HALF ** -0.5, HEAD_DIM ** -0.5) * LOG2E
    v_lo = np.where(is_diff, (j // 2) * HEAD_DIM, 256 + (j - 8) * HEAD_DIM)
    vmask = (lane >= v_lo) & (lane < v_lo + HEAD_DIM) & (is_diff | is_moba)
    head = np.where(is_diff, j // 2, j - 8)
    slope = np.where(is_diff, 2.0 ** -(2 * head + 1), np.where(is_moba, 2.0 ** -(2 * head + 2), 0.0)) * LOG2E
    coef_a = np.where((is_diff & (j % 2 == 0)) | is_moba, 1.0, 0.0)
    coef_b = np.where(is_diff & (j % 2 == 1), -1.0, 0.0)
    bc = lambda a: jnp.asarray(np.broadcast_to(a, (ROWS, LANES)).astype(np.float32))
    i = np.arange(SOFT_W)
    bd512 = (i[:, None] // HEAD_DIM == i[None, :] // HEAD_DIM).astype(np.float32)
    return (jnp.asarray((qmask * qscale).astype(np.float32)), jnp.asarray((qmask & is_moba).astype(np.float32)),
            jnp.asarray(vmask.astype(np.float32)), bc(slope), bc(coef_a), bc(coef_b), jnp.asarray(bd512))


def _decode_attn_body(pt_ref, lam_ref, q_ref, ks_ref, vs_ref, *rest, nblk, npg, page, past_len, out_scale):
    del pt_ref
    k_refs = rest[0:npg]
    v_refs = rest[npg:2 * npg]
    qmask_ref, gmask_ref, vmask_ref, slope_ref, ca_ref, cb_ref, bd_ref, g_ref, o_ref = rest[2 * npg:2 * npg + 9]
    m_scr, l_scr, acc_scr, mblk, lblk, gblk, accblk = rest[2 * npg + 9:]
    b = pl.program_id(1)
    nstep = pl.num_programs(1)
    tk = 2 * page

    @pl.when(b == 0)
    def _():
        m_scr[...] = jnp.full(m_scr.shape, -jnp.inf, F32)
        l_scr[...] = jnp.zeros(l_scr.shape, F32)
        acc_scr[...] = jnp.zeros(acc_scr.shape, F32)
        mblk[...] = jnp.full(mblk.shape, -jnp.inf, F32)
        lblk[...] = jnp.zeros(lblk.shape, F32)
        gblk[...] = jnp.full(gblk.shape, -jnp.inf, F32)

    qs = q_ref[...] * qmask_ref[...]
    qsb = qs.astype(BF16)
    qgb = (q_ref[...] * gmask_ref[...]).astype(BF16)
    slope = slope_ref[:, 0:1]
    lane = lax.broadcasted_iota(jnp.int32, (ROWS, LANES), 1)
    tok = lax.broadcasted_iota(jnp.int32, (1, tk), 1)

    nu = npg // 2
    raw = [jnp.concatenate([_dot(qsb, k_refs[2 * u][...].astype(BF16)), _dot(qsb, k_refs[2 * u + 1][...].astype(BF16))],
                           axis=1) for u in range(nu)]
    mbs, lbs, pes = [], [], []
    for u in range(nu):
        dist = (past_len - ((b * nu + u) * tk + tok)).astype(F32)
        s = raw[u] - slope * dist
        mb = jnp.max(s, axis=-1, keepdims=True)
        pexp = jnp.exp2(s - mb)
        mbs.append(mb)
        lbs.append(jnp.sum(pexp, axis=-1, keepdims=True))
        pes.append(pexp.astype(BF16))
    accbs = [_dot_nt(pes[u][:, 0:page], v_refs[2 * u][...].astype(BF16))
             + _dot_nt(pes[u][:, page:tk], v_refs[2 * u + 1][...].astype(BF16)) for u in range(nu)]
    kmeans = [(jnp.sum(k_refs[2 * u][...], axis=-1, keepdims=True)
               + jnp.sum(k_refs[2 * u + 1][...], axis=-1, keepdims=True)) * (1.0 / tk) for u in range(nu)]
    gates = [_dot(qgb, kmeans[u].astype(BF16)) for u in range(nu)]

    for u in range(nu):
        bb = b * nu + u
        mb, lb, accb = mbs[u], lbs[u], accbs[u]
        m_old = m_scr[:, 0:1]
        m_new = jnp.maximum(m_old, mb)
        a_old = jnp.exp2(m_old - m_new)
        a_blk = jnp.exp2(mb - m_new)
        l_scr[...] = jnp.broadcast_to(a_old * l_scr[:, 0:1] + a_blk * lb, l_scr.shape)
        acc_scr[...] = a_old * acc_scr[...] + a_blk * accb
        m_scr[...] = jnp.broadcast_to(m_new, m_scr.shape)
        hit = lane == bb
        mblk[...] = jnp.where(hit, mb, mblk[...])
        lblk[...] = jnp.where(hit, lb, lblk[...])
        gblk[...] = jnp.where(hit, gates[u], gblk[...])
        accblk[bb] = accb

    @pl.when(b == nstep - 1)
    def _():
        s_self = jnp.sum(qs * ks_ref[...], axis=-1, keepdims=True)
        vs = vs_ref[...]
        m_o = m_scr[:, 0:1]
        m_d = jnp.maximum(m_o, s_self)
        a_o = jnp.exp2(m_o - m_d)
        p_d = jnp.exp2(s_self - m_d)
        o_d = (a_o * acc_scr[...] + p_d * vs) / (a_o * l_scr[:, 0:1] + p_d)
        sel = _topk_select(gblk[...], lane, nblk, 1, LANES)
        mm = mblk[...]
        m_f = jnp.maximum(jnp.max(jnp.where(sel, mm, -jnp.inf), axis=-1, keepdims=True), s_self)
        wgt = jnp.where(sel, jnp.exp2(mm - m_f), 0.0)
        p_m = jnp.exp2(s_self - m_f)
        l_m = jnp.sum(wgt * lblk[...], axis=-1, keepdims=True) + p_m
        acc_m = p_m * vs
        for blk in range(nblk):
            acc_m = acc_m + wgt[:, blk:blk + 1] * accblk[blk]
        o_m = acc_m / l_m
        rowi = lax.broadcasted_iota(jnp.int32, (ROWS, SOFT_W), 0)
        o_all = jnp.where(rowi < 8, o_d, o_m)
        coef = ca_ref[:, 0:1] + lam_ref[0] * cb_ref[:, 0:1]
        o_row = jnp.sum(coef * vmask_ref[...] * o_all, axis=0, keepdims=True)
        ms = _dot(jnp.broadcast_to(o_row * o_row, (8, SOFT_W)), bd_ref[...], HI)[0:1] * (1.0 / HEAD_DIM)
        o_norm = o_row * lax.rsqrt(ms + EPS) * g_ref[...] * out_scale
        lane5 = lax.broadcasted_iota(jnp.int32, (1, SOFT_W), 1)
        o_ref[...] = jnp.where(lane5 < 256, o_norm, o_row)


def _decode_attn(layer, q_soft, k_self, v_self, cache_k, cache_v, page_table, lam, subln_g, lam_init):
    ns = q_soft.shape[0]
    n_pages = page_table.shape[1]
    page = cache_k.shape[3]
    npg = PAGES_PER_STEP
    assert 2 * page == MOBA_BLOCK and n_pages % npg == 0 and cache_k.shape[2] == SOFT_W
    nblk = n_pages // 2
    assert nblk <= LANES
    kern = functools.partial(_decode_attn_body, nblk=nblk, npg=npg, page=page, past_len=n_pages * page,
                             out_scale=1.0 - lam_init)
    row = pl.BlockSpec((None, 1, SOFT_W), lambda s, b, pt: (s, 0, 0))
    pg = lambda off: pl.BlockSpec((None, None, SOFT_W, page), lambda s, b, pt: (layer, pt[s, npg * b + off], 0, 0))
    cst = lambda shp: pl.BlockSpec(shp, lambda s, b, pt: (0,) * len(shp))
    g4 = jnp.concatenate([jnp.tile(subln_g.reshape(1, HEAD_DIM), (1, 4)), jnp.ones((1, 256), F32)], axis=1)
    grid_spec = pltpu.PrefetchScalarGridSpec(
        num_scalar_prefetch=1,
        grid=(ns, n_pages // npg),
        in_specs=([pl.BlockSpec(memory_space=pltpu.SMEM), row, row, row]
                  + [pg(u) for u in range(npg)] + [pg(u) for u in range(npg)]
                  + [cst((ROWS, SOFT_W)), cst((ROWS, SOFT_W)), cst((ROWS, SOFT_W)), cst((ROWS, LANES)), cst((ROWS, LANES)),
                     cst((ROWS, LANES)), cst((SOFT_W, SOFT_W)), cst((1, SOFT_W))]),
        out_specs=row,
        scratch_shapes=[pltpu.VMEM((ROWS, LANES), F32), pltpu.VMEM((ROWS, LANES), F32), pltpu.VMEM((ROWS, SOFT_W), F32),
                        pltpu.VMEM((ROWS, LANES), F32), pltpu.VMEM((ROWS, LANES), F32), pltpu.VMEM((ROWS, LANES), F32),
                        pltpu.VMEM((nblk, ROWS, SOFT_W), F32)],
    )
    out = pl.pallas_call(
        kern,
        grid_spec=grid_spec,
        out_shape=jax.ShapeDtypeStruct((ns, 1, SOFT_W), F32),
        compiler_params=_cp(2),
        name="decode_attn",
    )(page_table, lam.reshape(1), q_soft[:, None, :], k_self[:, None, :], v_self[:, None, :],
      *([cache_k] * npg), *([cache_v] * npg), *_decode_consts(), g4)
    return out[:, 0, :]


def _pages_transposed(cache):
    d, n_pool, page = cache.shape[0:3]
    return jnp.transpose(cache, (0, 1, 3, 4, 2)).reshape(d, n_pool, SOFT_W, page)


def _lam(l, lam_q1, lam_k1, lam_q2, lam_k2):
    lam_init = 0.8 - 0.6 * math.exp(-0.3 * l)
    lam = (jnp.exp(jnp.sum(lam_q1[l] * lam_k1[l])) - jnp.exp(jnp.sum(lam_q2[l] * lam_k2[l])) + lam_init)
    return lam.astype(F32), lam_init


def _prompt_layer(x, w, lam, lam_init, final):
    n, t, d = x.shape
    outs = _proj_in(x.reshape(n * t, d), w["norm1"], w["w_in"])
    qs, kr, vr, qkv, zg, b_x, a_x = [a.reshape(n, t, a.shape[-1]) for a in outs]
    o_a = _diff_attn_prompt(qs, kr, vr, lam, w["subln"], lam_init)
    o_b = _moba_prompt(qs, kr, vr)
    o_c, s_bd = _gdn_prompt(qkv, zg, b_x, a_x, w["dn_conv_w"], w["dn_a_log"], w["dn_dt_bias"], w["dn_norm"])
    y, ffn_new = _ffn_prompt(x, o_a, o_b, o_c, w["w_out"], w["norm2"], w["ffn_in"], w["ffn_conv_w"], w["ffn_conv_b"],
                             w["ffn_out"], w["normf"], final)
    rows = (kr.reshape(n, t, 8, HEAD_DIM), vr.reshape(n, t, 8, HEAD_DIM), _state_from_bd(s_bd),
            qkv[:, t - (DN_CONV - 1):, :], ffn_new)
    return y, rows


def _sample_layer(l, x, w, lam, lam_init, final, cache_k, cache_v, page_table, state_dn, conv_qkv, conv_ffn):
    ns, d = x.shape
    qs, kr, vr, qkv, zg, b_x, a_x = _proj_in(x, w["norm1"], w["w_in"])
    o_ab = _decode_attn(l, qs, kr, vr, cache_k, cache_v, page_table, lam, w["subln"], lam_init)
    o_c, s_bd, conv_new = _gdn_decode(qkv, zg, b_x, a_x, conv_qkv, _state_to_bd(state_dn), w["dn_conv_w"], w["dn_a_log"],
                                      w["dn_dt_bias"], w["dn_norm"])
    y, ffn_new = _ffn_decode(x, o_ab[:, 0:256], o_ab[:, 256:512], o_c[:, 0, :], jnp.swapaxes(conv_ffn, 0, 1),
                             w["w_out"], w["norm2"], w["ffn_in"], w["ffn_conv_w"], w["ffn_conv_b"], w["ffn_out"],
                             w["normf"], final)
    rows = (kr.reshape(ns, 1, 8, HEAD_DIM), vr.reshape(ns, 1, 8, HEAD_DIM), _state_from_bd(s_bd), conv_new,
            jnp.swapaxes(ffn_new, 0, 1))
    return y, rows


def kernel(x_prompt, x_sample, cache_k, cache_v, page_table, state_dn, state_conv_qkv, state_conv_ffn, norm1_g, norm2_g, normf_g, w_in, w_out, lam_q1, lam_k1, lam_q2, lam_k2, subln_g, dn_conv_w, dn_a_log, dn_dt_bias, dn_norm_g, ffn_w_in, ffn_conv_w, ffn_conv_b, ffn_w_out):
    depth = w_in.shape[0]
    cache_k = _pages_transposed(cache_k)
    cache_v = _pages_transposed(cache_v)
    xp = x_prompt
    xs = x_sample.reshape(x_sample.shape[0], x_sample.shape[2])
    rows_p, rows_s = [], []
    for l in range(depth):
        w = {"norm1": norm1_g[l], "w_in": _permute_w_in(w_in[l]), "subln": subln_g[l], "dn_conv_w": dn_conv_w[l],
             "dn_a_log": dn_a_log[l], "dn_dt_bias": dn_dt_bias[l], "dn_norm": dn_norm_g[l],
             "w_out": w_out[l].astype(BF16), "norm2": norm2_g[l], "ffn_in": ffn_w_in[l].astype(BF16),
             "ffn_conv_w": ffn_conv_w[l], "ffn_conv_b": ffn_conv_b[l], "ffn_out": ffn_w_out[l].astype(BF16),
             "normf": normf_g}
        lam, lam_init = _lam(l, lam_q1, lam_k1, lam_q2, lam_k2)
        final = l == depth - 1
        xp, rp = _prompt_layer(xp, w, lam, lam_init, final)
        xs, rs = _sample_layer(l, xs, w, lam, lam_init, final, cache_k, cache_v, page_table, state_dn[l],
                               state_conv_qkv[l], state_conv_ffn[l])
        rows_p.append(rp)
        rows_s.append(rs)
    stack = lambda rows, i: jnp.stack([r[i] for r in rows], axis=0)
    y_sample = xs.reshape(x_sample.shape)
    return (xp, y_sample,
            stack(rows_p, 0), stack(rows_p, 1), stack(rows_p, 2), stack(rows_p, 3), stack(rows_p, 4),
            stack(rows_s, 0), stack(rows_s, 1), stack(rows_s, 2), stack(rows_s, 3), stack(rows_s, 4))
```

```python
import functools
import math

import numpy as np
import jax
import jax.numpy as jnp
from jax import lax
from jax.experimental import pallas as pl
from jax.experimental.pallas import tpu as pltpu

F32 = jnp.float32
BF16 = jnp.bfloat16
HI = lax.Precision.HIGHEST

LANES = 128
HEAD_DIM = 64
DIFF_HALF = 32
N_DIFF_HEADS = 4
N_MOBA_HEADS = 4
N_SOFT_HEADS = 8
N_DN_HEADS = 8
N_PAIRS = 4
SOFT_W = 512
DN_W = 512
D_FF = 3584
MOBA_BLOCK = 256
MOBA_TOPK = 3
DN_CONV = 4
FFN_CONV = 3
EPS = 1e-6
LOG2E = 1.4426950408889634
NEG_BIG = -1e30
Z_W = 3584 + 2 * DN_W + LANES
VMEM_LIMIT = 56 * 1024 * 1024


def _cp(n_axes, vmem=VMEM_LIMIT):
    return pltpu.CompilerParams(dimension_semantics=("arbitrary",) * n_axes, vmem_limit_bytes=vmem)


def _sigmoid(x):
    return 1.0 / (1.0 + jnp.exp(-x))


def _silu(x):
    return x * _sigmoid(x)


def _softplus(x):
    return jnp.maximum(x, 0.0) + jnp.log(1.0 + jnp.exp(-jnp.abs(x)))


def _dot(a, b, prec=None):
    return jnp.dot(a, b, preferred_element_type=F32, precision=prec)


def _dot_nt(a, b, prec=None):
    return lax.dot_general(a, b, (((1,), (1,)), ((), ())), preferred_element_type=F32, precision=prec)


def _dot_tn(a, b, prec=None):
    return lax.dot_general(a, b, (((0,), (0,)), ((), ())), preferred_element_type=F32, precision=prec)


def _split2(x):
    hi = x.astype(BF16)
    return hi, (x - hi.astype(F32)).astype(BF16)


def _mm3(a, b, dot=_dot):
    ah, al = _split2(a)
    bh, bl = _split2(b)
    return dot(ah, bh) + dot(ah, bl) + dot(al, bh)


def _mm2(a, b01, dot=_dot):
    ah, al = _split2(a)
    return dot(ah, b01) + dot(al, b01)


def _const_spec(shape):
    nd = len(shape)
    return pl.BlockSpec(shape, lambda *_: (0,) * nd)


def _bd_ones(dtype=F32):
    i = np.arange(LANES)
    return jnp.asarray((i[:, None] // HEAD_DIM == i[None, :] // HEAD_DIM).astype(np.float32)).astype(dtype)


_PROJ_WIDTHS = (512, 512, 512, 1536, 512, 512, 512, LANES)


def _proj_in_body(x_ref, g_ref, w_ref, *out_refs):
    x = x_ref[...]
    ms = jnp.mean(x * x, axis=-1, keepdims=True)
    h = (x * lax.rsqrt(ms + EPS) * g_ref[...]).astype(BF16)
    lo = 0
    for o_ref, w in zip(out_refs, _PROJ_WIDTHS):
        o_ref[...] = _dot(h, w_ref[:, lo:lo + w])
        lo += w


def _proj_in(x2d, g, w_perm):
    m, d = x2d.shape
    tm = min(m, 256)
    return pl.pallas_call(
        _proj_in_body,
        grid=(m // tm,),
        in_specs=[pl.BlockSpec((tm, d), lambda i: (i, 0)), _const_spec((1, d)), _const_spec((d, Z_W))],
        out_specs=[pl.BlockSpec((tm, w), lambda i: (i, 0)) for w in _PROJ_WIDTHS],
        out_shape=[jax.ShapeDtypeStruct((m, w), F32) for w in _PROJ_WIDTHS],
        compiler_params=_cp(1),
        name="proj_in",
    )(x2d, g.reshape(1, d), w_perm)


def _permute_w_in(w):
    cols = [w[:, 0:256], w[:, 768:1024], w[:, 256:512], w[:, 1024:1280], w[:, 512:768], w[:, 1280:1536],
            w[:, 1536:3584], jnp.repeat(w[:, 3584:3592], HEAD_DIM, axis=1), jnp.repeat(w[:, 3592:3600], HEAD_DIM, axis=1),
            w[:, 3584:3600], jnp.zeros((w.shape[0], LANES - 2 * N_DN_HEADS), w.dtype)]
    return jnp.concatenate(cols, axis=1).astype(BF16)


def _prep_kv(k_ref, v_ref, kb_scr, vt_scr, t, tq):
    kb_scr[...] = k_ref[0].astype(BF16)
    row_head = lax.broadcasted_iota(jnp.int32, (LANES, tq), 0) // HEAD_DIM
    for c in range(t // tq):
        vt = v_ref[0, c * tq:(c + 1) * tq, :].T
        for hl in range(2):
            vt_scr[hl, c] = jnp.where(row_head == hl, vt, 1.0).astype(BF16)


def _denominator(acc_ref, j, hl):
    r = (1 - hl) * HEAD_DIM
    return acc_ref[j, r:r + 1, :]


def _online_stats_t(j, sp, shift, m_scr):
    m_old = m_scr[j]
    m_new = jnp.maximum(m_old, jnp.max(sp, axis=0, keepdims=True) + shift)
    p = jnp.exp2(sp - (m_new - shift))
    m_scr[j] = m_new
    return p.astype(BF16), jnp.exp2(m_old - m_new)


def _diff_attn_body(lam_ref, q_ref, k_ref, v_ref, g_ref, o_ref,
                    kb_scr, vt_scr, qm_scr, sa_scr, sb_scr, m_scr, acc_scr, *, t, tq, out_scale):
    p = pl.program_id(1)
    qi = pl.program_id(2)

    @pl.when(qi == 0)
    def _():
        _prep_kv(k_ref, v_ref, kb_scr, vt_scr, t, tq)

    lane = lax.broadcasted_iota(jnp.int32, (tq, LANES), 1)
    q = q_ref[0] * (DIFF_HALF ** -0.5 * LOG2E)
    for j in range(4):
        lo = (j // 2) * HEAD_DIM + (j % 2) * DIFF_HALF
        qm_scr[j] = jnp.where((lane >= lo) & (lane < lo + DIFF_HALF), q, 0.0).astype(BF16)
    m_scr[...] = jnp.full(m_scr.shape, -jnp.inf, F32)
    acc_scr[...] = jnp.zeros(acc_scr.shape, F32)
    slopes = [jnp.where(p == 0, LOG2E * 2.0 ** -(2 * hl + 1), LOG2E * 2.0 ** -(2 * (hl + 2) + 1)).astype(F32)
              for hl in range(2)]
    krow = lax.broadcasted_iota(jnp.int32, (tq, tq), 0)
    qcol = lax.broadcasted_iota(jnp.int32, (tq, tq), 1)
    krow_f = krow.astype(F32)
    ramps = [slopes[hl] * krow_f for hl in range(2)]

    def scores(ki, s_ref):
        kt = kb_scr[pl.ds(pl.multiple_of(ki * tq, tq), tq), :]
        for j in range(4):
            s_ref[j] = _dot_nt(kt, qm_scr[j])

    def softmax_pv(ki, s_ref, masked):
        off = ((ki - qi) * tq).astype(F32)
        pa = []
        for j in range(4):
            sp = s_ref[j] + ramps[j // 2]
            if masked:
                sp = jnp.where(krow <= qcol, sp, -jnp.inf)
            pa.append(_online_stats_t(j, sp, slopes[j // 2] * off, m_scr))
        for j in range(4):
            acc_scr[j] = pa[j][1] * acc_scr[j] + _dot(vt_scr[j // 2, ki], pa[j][0])

    scores(0, sa_scr)

    def body(i, carry):
        ki = 2 * i
        scores(ki + 1, sb_scr)
        softmax_pv(ki, sa_scr, False)
        scores(ki + 2, sa_scr)
        softmax_pv(ki + 1, sb_scr, False)
        return carry

    lax.fori_loop(0, qi // 2, body, 0)

    @pl.when(qi % 2 == 1)
    def _():
        scores(qi, sb_scr)
        softmax_pv(qi - 1, sa_scr, False)
        softmax_pv(qi, sb_scr, True)

    @pl.when(qi % 2 == 0)
    def _():
        softmax_pv(qi, sa_scr, True)

    lam = lam_ref[0]
    outs = []
    for hl in range(2):
        r = slice(hl * HEAD_DIM, (hl + 1) * HEAD_DIM)
        o = (acc_scr[2 * hl, r, :] * (1.0 / _denominator(acc_scr, 2 * hl, hl))
             - lam * (acc_scr[2 * hl + 1, r, :] * (1.0 / _denominator(acc_scr, 2 * hl + 1, hl))))
        ms = jnp.sum(o * o, axis=0, keepdims=True) * (1.0 / HEAD_DIM)
        outs.append(o * lax.rsqrt(ms + EPS))
    ot = jnp.concatenate(outs, axis=0) * g_ref[...] * out_scale
    o_ref[0] = ot.T


def _diff_attn_prompt(q_soft, k_rows, v_rows, lam, subln_g, lam_init, tq=256):
    n, t, _ = q_soft.shape
    tq = min(tq, t)
    g2 = jnp.tile(subln_g.reshape(HEAD_DIM, 1), (2, 1))
    kern = functools.partial(_diff_attn_body, t=t, tq=tq, out_scale=1.0 - lam_init)
    return pl.pallas_call(
        kern,
        grid=(n, 2, t // tq),
        in_specs=[pl.BlockSpec(memory_space=pltpu.SMEM),
                  pl.BlockSpec((1, tq, LANES), lambda b, p, i: (b, i, p)),
                  pl.BlockSpec((1, t, LANES), lambda b, p, i: (b, 0, p)),
                  pl.BlockSpec((1, t, LANES), lambda b, p, i: (b, 0, p)),
                  _const_spec((LANES, 1))],
        out_specs=pl.BlockSpec((1, tq, LANES), lambda b, p, i: (b, i, p)),
        out_shape=jax.ShapeDtypeStruct((n, t, 2 * LANES), F32),
        scratch_shapes=[pltpu.VMEM((t, LANES), BF16), pltpu.VMEM((2, t // tq, LANES, tq), BF16),
                        pltpu.VMEM((4, tq, LANES), BF16), pltpu.VMEM((4, tq, tq), F32), pltpu.VMEM((4, tq, tq), F32),
                        pltpu.VMEM((4, 1, tq), F32),
                        pltpu.VMEM((4, LANES, tq), F32)],
        compiler_params=_cp(3),
        name="diff_attn_prompt",
    )(lam.reshape(1), q_soft, k_rows, v_rows, g2)


def _topk_select(gate, idx, n_valid, axis, size):
    gm = jnp.where(idx < n_valid, gate, -jnp.inf)
    sel = jnp.zeros(gate.shape, jnp.bool_)
    for _ in range(MOBA_TOPK):
        mx = jnp.max(gm, axis=axis, keepdims=True)
        is_max = (gm == mx) & (mx > -jnp.inf)
        first = jnp.min(jnp.where(is_max, idx, size), axis=axis, keepdims=True)
        pick = idx == first
        sel = sel | pick
        gm = jnp.where(pick, -jnp.inf, gm)
    return sel


def _moba_body(q_ref, k_ref, v_ref, o_ref, kb_scr, vt_scr, kmean_scr, qs_scr, sel_scr, sd_scr, sa_scr, sb_scr,
               m_scr, acc_scr, *, t, tq, nb):
    nbp = kmean_scr.shape[0]
    p = pl.program_id(1)
    qi = pl.program_id(2)

    @pl.when(qi == 0)
    def _():
        _prep_kv(k_ref, v_ref, kb_scr, vt_scr, t, tq)
        kmean_scr[...] = jnp.zeros(kmean_scr.shape, F32)
        for b in range(nb):
            kmean_scr[b:b + 1, :] = jnp.sum(k_ref[0, b * MOBA_BLOCK:(b + 1) * MOBA_BLOCK, :], axis=0,
                                            keepdims=True) * (1.0 / MOBA_BLOCK)

    lane = lax.broadcasted_iota(jnp.int32, (tq, LANES), 1)
    blk = lax.broadcasted_iota(jnp.int32, (nbp, tq), 0)
    q = q_ref[0]
    kmean = kmean_scr[...]
    for hl in range(2):
        qh = jnp.where((lane >= hl * HEAD_DIM) & (lane < (hl + 1) * HEAD_DIM), q, 0.0)
        gate_t = _dot_nt(kmean.astype(BF16), qh.astype(BF16))
        sel = _topk_select(gate_t, blk, qi, 0, nbp)
        sel_scr[hl] = jnp.where(sel, 0.0, NEG_BIG)
        qs_scr[hl] = (qh * (HEAD_DIM ** -0.5 * LOG2E)).astype(BF16)
    m_scr[...] = jnp.full(m_scr.shape, -jnp.inf, F32)
    acc_scr[...] = jnp.zeros(acc_scr.shape, F32)
    slopes = [jnp.where(p == 0, LOG2E * 2.0 ** -(2 * hl + 2), LOG2E * 2.0 ** -(2 * (hl + 2) + 2)).astype(F32)
              for hl in range(2)]
    krow = lax.broadcasted_iota(jnp.int32, (tq, tq), 0)
    qcol = lax.broadcasted_iota(jnp.int32, (tq, tq), 1)
    krow_f = krow.astype(F32)
    ramps = [slopes[hl] * krow_f for hl in range(2)]

    def scores(ki, s_ref):
        kt = kb_scr[pl.ds(pl.multiple_of(ki * tq, tq), tq), :]
        for hl in range(2):
            s_ref[hl] = _dot_nt(kt, qs_scr[hl])

    def softmax_pv(ki, s_ref, diagonal):
        off = ((ki - qi) * tq).astype(F32)
        pa = []
        for hl in range(2):
            sp = s_ref[hl] + ramps[hl]
            shift = slopes[hl] * off
            if diagonal:
                sp = jnp.where(krow <= qcol, sp, -jnp.inf)
            else:
                shift = shift + sel_scr[hl, pl.ds(ki, 1), :]
            pa.append(_online_stats_t(hl, sp, shift, m_scr))
        for hl in range(2):
            acc_scr[hl] = pa[hl][1] * acc_scr[hl] + _dot(vt_scr[hl, ki], pa[hl][0])

    scores(qi, sd_scr)
    scores(0, sa_scr)
    softmax_pv(qi, sd_scr, True)

    def body(i, carry):
        ki = 2 * i
        scores(ki + 1, sb_scr)
        softmax_pv(ki, sa_scr, False)
        scores(ki + 2, sa_scr)
        softmax_pv(ki + 1, sb_scr, False)
        return carry

    lax.fori_loop(0, qi // 2, body, 0)

    @pl.when(qi % 2 == 1)
    def _():
        softmax_pv(qi - 1, sa_scr, False)

    outs = []
    for hl in range(2):
        outs.append(acc_scr[hl, hl * HEAD_DIM:(hl + 1) * HEAD_DIM, :] * (1.0 / _denominator(acc_scr, hl, hl)))
    o_ref[0] = jnp.concatenate(outs, axis=0).T


def _moba_prompt(q_soft, k_rows, v_rows):
    n, t, _ = q_soft.shape
    tq = MOBA_BLOCK
    assert t % tq == 0 and t // tq <= LANES
    nb = t // tq
    nbp = -(-nb // 8) * 8
    kern = functools.partial(_moba_body, t=t, tq=tq, nb=nb)
    return pl.pallas_call(
        kern,
        grid=(n, 2, nb),
        in_specs=[pl.BlockSpec((1, tq, LANES), lambda b, p, i: (b, i, 2 + p)),
                  pl.BlockSpec((1, t, LANES), lambda b, p, i: (b, 0, 2 + p)),
                  pl.BlockSpec((1, t, LANES), lambda b, p, i: (b, 0, 2 + p))],
        out_specs=pl.BlockSpec((1, tq, LANES), lambda b, p, i: (b, i, p)),
        out_shape=jax.ShapeDtypeStruct((n, t, 2 * LANES), F32),
        scratch_shapes=[pltpu.VMEM((t, LANES), BF16), pltpu.VMEM((2, nb, LANES, tq), BF16),
                        pltpu.VMEM((nbp, LANES), F32),
                        pltpu.VMEM((2, tq, LANES), BF16), pltpu.VMEM((2, nbp, tq), F32),
                        pltpu.VMEM((2, tq, tq), F32), pltpu.VMEM((2, tq, tq), F32), pltpu.VMEM((2, tq, tq), F32),
                        pltpu.VMEM((2, 1, tq), F32),
                        pltpu.VMEM((2, LANES, tq), F32)],
        compiler_params=_cp(3),
        name="moba_prompt",
    )(q_soft, k_rows, v_rows)


def _bdiag(x, lane_lo):
    return jnp.concatenate([jnp.where(lane_lo, x, 0.0), jnp.where(lane_lo, 0.0, x)], axis=0)


def _gdn_prompt_body(qkv_ref, zg_ref, bx_ref, ax_ref, arow_ref, cw_ref, alogx_ref, dtbx_ref, alogr_ref, dtbr_ref,
                     gn_ref, bd_ref, ltri_ref, ubd_ref,
                     o_ref, sfin_ref, xbuf, s_scr, *, c, nseq):
    ci = pl.program_id(0)
    nc = pl.num_programs(0)

    @pl.when(ci == 0)
    def _():
        xbuf[:, 0:8, :] = jnp.zeros((nseq, 8, 3 * DN_W), F32)
        s_scr[...] = jnp.zeros(s_scr.shape, F32)

    ltri = ltri_ref[...]
    ubd = ubd_ref[...]
    bd = bd_ref[...]
    cw = cw_ref[...]
    row = lax.broadcasted_iota(jnp.int32, (c, LANES), 0)
    lane = lax.broadcasted_iota(jnp.int32, (c, LANES), 1)
    col = lane % HEAD_DIM
    lane_lo = lane < HEAD_DIM
    incl = row >= col
    strict = row > col
    eye2 = jnp.where(row == col, 1.0, 0.0)
    r128 = lax.broadcasted_iota(jnp.int32, (LANES, LANES), 0)
    c128 = lax.broadcasted_iota(jnp.int32, (LANES, LANES), 1)
    same_head = (r128 // HEAD_DIM) == (c128 // HEAD_DIM)
    steps = max(1, int(math.ceil(math.log2(c))) - 1)

    def split3_dot(x, lhs01=None, rhs01=None):
        x0, x1 = _split2(x)
        x2 = (x - x0.astype(F32) - x1.astype(F32)).astype(BF16)
        if lhs01 is not None:
            return _dot(lhs01, x0) + _dot(lhs01, x1) + _dot(lhs01, x2)
        return _dot(x0, rhs01) + _dot(x1, rhs01) + _dot(x2, rhs01)

    cq, beta_x, gc_x, gc_row = [], [], [], []
    for b in range(nseq):
        xbuf[b, 8:8 + c, :] = qkv_ref[b]
        y = (cw[3:4] * xbuf[b, 8:8 + c, :] + cw[2:3] * xbuf[b, 7:7 + c, :]
             + cw[1:2] * xbuf[b, 6:6 + c, :] + cw[0:1] * xbuf[b, 5:5 + c, :])
        xbuf[b, 0:8, :] = xbuf[b, c:c + 8, :]
        cq.append(_silu(y))
        beta_x.append(_sigmoid(bx_ref[b]))
        g_x = -jnp.exp(alogx_ref[...]) * _softplus(ax_ref[b] + dtbx_ref[...])
        g_row = -jnp.exp(alogr_ref[...]) * _softplus(arow_ref[b, 0] + dtbr_ref[...])
        gc_x.append(split3_dot(g_x, lhs01=ltri))
        gc_row.append(split3_dot(g_row, rhs01=ubd))

    chains = [(b, p) for b in range(nseq) for p in range(N_PAIRS)]
    pair = lambda arr, p, off=0: arr[:, off + p * LANES:off + (p + 1) * LANES]
    qp = [pair(cq[b], p) for b, p in chains]
    kp = [pair(cq[b], p, DN_W) for b, p in chains]
    vp = [pair(cq[b], p, 2 * DN_W) for b, p in chains]
    bexp = [pair(beta_x[b], p) for b, p in chains]
    gcx = [pair(gc_x[b], p) for b, p in chains]
    gcr = [gc_row[b][p:p + 1, :] for b, p in chains]
    n_ch = len(chains)
    rng = range(n_ch)

    ssq_q = [_mm2(qp[i] * qp[i], bd) for i in rng]
    ssq_k = [_mm2(kp[i] * kp[i], bd) for i in rng]
    qn = [qp[i] * lax.rsqrt(ssq_q[i] + EPS) * (HEAD_DIM ** -0.5) for i in rng]
    kn = [kp[i] * lax.rsqrt(ssq_k[i] + EPS) for i in rng]
    decay = [jnp.exp(jnp.where(incl, gcx[i] - gcr[i], -jnp.inf)) for i in rng]
    kq = [_mm3(jnp.concatenate([kn[i], qn[i]], axis=0), _bdiag(kn[i], lane_lo), _dot_nt) for i in rng]
    qk = [jnp.where(incl, kq[i][c:2 * c] * decay[i], 0.0) for i in rng]
    bpow = [-jnp.where(strict, bexp[i] * kq[i][0:c] * decay[i], 0.0) for i in rng]
    x = [eye2 + bpow[i] for i in rng]
    for _ in range(steps):
        bpow = [_mm3(bpow[i], _bdiag(bpow[i], lane_lo)) for i in rng]
        x = [x[i] + _mm3(x[i], _bdiag(bpow[i], lane_lo)) for i in rng]
    eg = [jnp.exp(gcx[i]) for i in rng]
    uw = [_mm3(x[i], jnp.concatenate([_bdiag(vp[i] * bexp[i], lane_lo), _bdiag(kn[i] * bexp[i] * eg[i], lane_lo)], axis=1))
          for i in rng]
    s_old = [s_scr[b, p] for b, p in chains]
    ws = [_mm3(jnp.concatenate([uw[i][:, LANES:2 * LANES], qn[i] * eg[i]], axis=0), s_old[i]) for i in rng]
    v_new = [uw[i][:, 0:LANES] - ws[i][0:c] for i in rng]
    glast = [gcx[i][c - 1:c, :] for i in rng]
    intra = [_mm3(qk[i], _bdiag(v_new[i], lane_lo)) for i in rng]
    upd = [_mm3(kn[i] * jnp.exp(glast[i] - gcx[i]), v_new[i], _dot_tn) for i in rng]
    o = [ws[i][c:2 * c] + intra[i] for i in rng]
    ms = [_mm2(o[i] * o[i], bd) * (1.0 / HEAD_DIM) for i in rng]
    for i, (b, p) in enumerate(chains):
        sl = slice(p * LANES, (p + 1) * LANES)
        s_scr[b, p] = s_old[i] * jnp.exp(glast[i]) + jnp.where(same_head, upd[i], 0.0)
        o_ref[b, :, sl] = o[i] * lax.rsqrt(ms[i] + EPS) * gn_ref[...] * _silu(zg_ref[b, :, sl])

    @pl.when(ci == nc - 1)
    def _():
        sfin_ref[...] = s_scr[...]


def _gdn_prompt(qkv, zg, b_x, a_x, ba, conv_w, a_log, dt_bias, norm_g, c=64):
    n, t, _ = qkv.shape
    nc = t // c
    a_raw = ba[..., N_DN_HEADS:2 * N_DN_HEADS]
    a_row = a_raw.reshape(n, nc, c, N_PAIRS, 2).transpose(0, 1, 3, 4, 2).reshape(n, nc, N_PAIRS, 2 * c)
    rep = lambda v: jnp.repeat(v.reshape(N_PAIRS, 2), c, axis=1)
    lanes = lambda v: jnp.repeat(v, HEAD_DIM).reshape(1, DN_W)
    i = np.arange(c)
    ltri = jnp.asarray((i[:, None] >= i[None, :]).astype(np.float32)).astype(BF16)
    j = np.arange(2 * c)
    ubd = jnp.asarray(((j[:, None] // c == j[None, :] // c) & (j[:, None] % c <= j[None, :] % c)).astype(np.float32))
    assert 2 * c == LANES
    kern = functools.partial(_gdn_prompt_body, c=c, nseq=n)
    tok = lambda w: pl.BlockSpec((n, c, w), lambda i: (0, i, 0))
    return pl.pallas_call(
        kern,
        grid=(nc,),
        in_specs=[tok(3 * DN_W), tok(DN_W), tok(DN_W), tok(DN_W),
                  pl.BlockSpec((n, 1, N_PAIRS, LANES), lambda i: (0, i, 0, 0)),
                  _const_spec((DN_CONV, 3 * DN_W)),
                  _const_spec((1, DN_W)), _const_spec((1, DN_W)),
                  _const_spec((N_PAIRS, LANES)), _const_spec((N_PAIRS, LANES)),
                  _const_spec((1, LANES)), _const_spec((LANES, LANES)),
                  _const_spec((c, c)), _const_spec((LANES, LANES))],
        out_specs=[tok(DN_W), _const_spec((n, N_PAIRS, LANES, LANES))],
        out_shape=[jax.ShapeDtypeStruct((n, t, DN_W), F32),
                   jax.ShapeDtypeStruct((n, N_PAIRS, LANES, LANES), F32)],
        scratch_shapes=[pltpu.VMEM((n, c + 8, 3 * DN_W), F32), pltpu.VMEM((n, N_PAIRS, LANES, LANES), F32)],
        compiler_params=_cp(1),
        name="gdn_prompt",
    )(qkv, zg, b_x, a_x, a_row, conv_w, lanes(a_log), lanes(dt_bias),
      rep(a_log), rep(dt_bias), jnp.tile(norm_g.reshape(1, HEAD_DIM), (1, 2)), _bd_ones(BF16), ltri, ubd.astype(BF16))


def _state_to_bd(s):
    n = s.shape[0]
    s = s.reshape(n, N_PAIRS, 2, HEAD_DIM, HEAD_DIM)
    z = jnp.zeros_like(s[:, :, 0])
    top = jnp.concatenate([s[:, :, 0], z], axis=-1)
    bot = jnp.concatenate([z, s[:, :, 1]], axis=-1)
    return jnp.concatenate([top, bot], axis=-2)


def _state_from_bd(sbd):
    n = sbd.shape[0]
    s = jnp.stack([sbd[:, :, :HEAD_DIM, :HEAD_DIM], sbd[:, :, HEAD_DIM:, HEAD_DIM:]], axis=2)
    return s.reshape(n, N_DN_HEADS, HEAD_DIM, HEAD_DIM)


def _gdn_decode_body(qkv_ref, cb_ref, zg_ref, bx_ref, ax_ref, cw_ref, alogx_ref, dtbx_ref, gn_ref, bd_ref,
                     s_ref, o_ref, snew_ref, cnew_ref):
    x = qkv_ref[...]
    buf = cb_ref[...]
    cw = cw_ref[...]
    y = cw[0:1] * buf[0:1] + cw[1:2] * buf[1:2] + cw[2:3] * buf[2:3] + cw[3:4] * x
    cnew_ref[...] = jnp.concatenate([buf[1:3], x], axis=0)
    cq = _silu(y)
    bexp = _sigmoid(bx_ref[...])
    eg = jnp.exp(-jnp.exp(alogx_ref[...]) * _softplus(ax_ref[...] + dtbx_ref[...]))
    bd = bd_ref[...]
    r128 = lax.broadcasted_iota(jnp.int32, (LANES, LANES), 0)
    c128 = lax.broadcasted_iota(jnp.int32, (LANES, LANES), 1)
    eye = r128 == c128
    rows8 = lambda v: jnp.broadcast_to(v, (8, v.shape[-1]))
    for p in range(N_PAIRS):
        sl = slice(p * LANES, (p + 1) * LANES)
        qp = cq[:, p * LANES:(p + 1) * LANES]
        kp = cq[:, DN_W + p * LANES:DN_W + (p + 1) * LANES]
        vp = cq[:, 2 * DN_W + p * LANES:2 * DN_W + (p + 1) * LANES]
        qn = qp * lax.rsqrt(_dot(rows8(qp * qp), bd, HI)[0:1] + EPS) * (HEAD_DIM ** -0.5)
        kn = kp * lax.rsqrt(_dot(rows8(kp * kp), bd, HI)[0:1] + EPS)
        s = s_ref[p] * eg[:, sl]
        ks = _dot(rows8(kn), s, HI)[0:1]
        delta = (vp - ks) * bexp[:, sl]
        kcol = _dot(jnp.where(eye, jnp.broadcast_to(kn, (LANES, LANES)), 0.0), bd, HI)
        s = s + kcol * delta
        snew_ref[p] = s
        o = _dot(rows8(qn), s, HI)[0:1]
        ms = _dot(rows8(o * o), bd, HI)[0:1] * (1.0 / HEAD_DIM)
        o_ref[:, sl] = o * lax.rsqrt(ms + EPS) * gn_ref[...] * _silu(zg_ref[:, sl])


def _gdn_decode(qkv, zg, b_x, a_x, conv_buf, s_bd, conv_w, a_log, dt_bias, norm_g):
    ns = qkv.shape[0]
    row = lambda w: pl.BlockSpec((None, 1, w), lambda i: (i, 0, 0))
    lanes = lambda v: jnp.repeat(v, HEAD_DIM).reshape(1, DN_W)
    return pl.pallas_call(
        _gdn_decode_body,
        grid=(ns,),
        in_specs=[row(3 * DN_W), pl.BlockSpec((None, DN_CONV - 1, 3 * DN_W), lambda i: (i, 0, 0)), row(DN_W),
                  row(DN_W), row(DN_W),
                  _const_spec((DN_CONV, 3 * DN_W)), _const_spec((1, DN_W)), _const_spec((1, DN_W)),
                  _const_spec((1, LANES)), _const_spec((LANES, LANES)),
                  pl.BlockSpec((None, N_PAIRS, LANES, LANES), lambda i: (i, 0, 0, 0))],
        out_specs=[row(DN_W), pl.BlockSpec((None, N_PAIRS, LANES, LANES), lambda i: (i, 0, 0, 0)),
                   pl.BlockSpec((None, DN_CONV - 1, 3 * DN_W), lambda i: (i, 0, 0))],
        out_shape=[jax.ShapeDtypeStruct((ns, 1, DN_W), F32),
                   jax.ShapeDtypeStruct((ns, N_PAIRS, LANES, LANES), F32),
                   jax.ShapeDtypeStruct((ns, DN_CONV - 1, 3 * DN_W), F32)],
        compiler_params=_cp(1),
        name="gdn_decode",
    )(qkv[:, None, :], conv_buf, zg[:, None, :], b_x[:, None, :], a_x[:, None, :], conv_w, lanes(a_log), lanes(dt_bias),
      jnp.tile(norm_g.reshape(1, HEAD_DIM), (1, 2)), _bd_ones(), s_bd)


def _rms(x, g):
    return x * lax.rsqrt(jnp.mean(x * x, axis=-1, keepdims=True) + EPS) * g


def _ffn_prompt_body(x_ref, oa_ref, ob_ref, oc_ref, wo_ref, g2_ref, wg_ref, wu_ref, cw_ref, cb_ref, wd_ref, gf_ref,
                     y_ref, st_ref, acc, h2, gbuf, carry, *, tm, final):
    i = pl.program_id(1)
    j = pl.program_id(2)
    nj = pl.num_programs(2)

    @pl.when(j == 0)
    def _():
        x1 = (x_ref[0] + _dot(oa_ref[0].astype(BF16), wo_ref[0:256, :]) + _dot(ob_ref[0].astype(BF16), wo_ref[256:512, :])
              + _dot(oc_ref[0].astype(BF16), wo_ref[512:1024, :]))
        acc[...] = x1
        h2[...] = _rms(x1, g2_ref[...]).astype(BF16)

    @pl.when(i == 0)
    def _():
        carry[j] = jnp.zeros(carry.shape[1:], F32)

    h = h2[...]
    g = _dot(h, wg_ref[...])
    u = _dot(h, wu_ref[...])
    gbuf[0:8, :] = carry[j]
    gbuf[8:8 + tm, :] = g
    cw = cw_ref[...]
    gc = cw[2:3] * g + cw[1:2] * gbuf[7:7 + tm, :] + cw[0:1] * gbuf[6:6 + tm, :] + cb_ref[...]
    carry[j] = gbuf[tm:tm + 8, :]
    st_ref[0, j] = gbuf[tm + 6:tm + 8, :]
    act = (_silu(gc) * u).astype(BF16)
    acc[...] += _dot(act, wd_ref[...])

    @pl.when(j == nj - 1)
    def _():
        if final:
            y_ref[0] = _rms(acc[...], gf_ref[...])
        else:
            y_ref[0] = acc[...]


def _ffn_prompt(x, o_a, o_b, o_c, w_out, norm2_g, w_ffn_in, conv_w, conv_b, w_ffn_out, normf_g, final, tm=1024, tf=512):
    n, t, d = x.shape
    tm = min(tm, t)
    nj = D_FF // tf
    kern = functools.partial(_ffn_prompt_body, tm=tm, final=final)
    rows = lambda w: pl.BlockSpec((1, tm, w), lambda b, i, j: (b, i, 0))
    y, st = pl.pallas_call(
        kern,
        grid=(n, t // tm, nj),
        in_specs=[rows(d), rows(256), rows(256), rows(512),
                  _const_spec((d, d)), _const_spec((1, d)),
                  pl.BlockSpec((d, tf), lambda b, i, j: (0, j)),
                  pl.BlockSpec((d, tf), lambda b, i, j: (0, j + nj)),
                  pl.BlockSpec((FFN_CONV, tf), lambda b, i, j: (0, j)),
                  pl.BlockSpec((1, tf), lambda b, i, j: (0, j)),
                  pl.BlockSpec((tf, d), lambda b, i, j: (j, 0)),
                  _const_spec((1, d))],
        out_specs=[rows(d), pl.BlockSpec((1, nj, FFN_CONV - 1, tf), lambda b, i, j: (b, 0, 0, 0))],
        out_shape=[jax.ShapeDtypeStruct((n, t, d), F32), jax.ShapeDtypeStruct((n, nj, FFN_CONV - 1, tf), F32)],
        scratch_shapes=[pltpu.VMEM((tm, d), F32), pltpu.VMEM((tm, d), BF16), pltpu.VMEM((tm + 8, tf), F32),
                        pltpu.VMEM((nj, 8, tf), F32)],
        compiler_params=_cp(3),
        name="ffn_prompt",
    )(x, o_a, o_b, o_c, w_out, norm2_g.reshape(1, d), w_ffn_in, w_ffn_in, conv_w, conv_b.reshape(1, D_FF),
      w_ffn_out, normf_g.reshape(1, d))
    return y, st.transpose(0, 2, 1, 3).reshape(n, FFN_CONV - 1, D_FF)


def _ffn_decode_body(x_ref, oa_ref, ob_ref, oc_ref, wo_ref, g2_ref, wg_ref, wu_ref, cw_ref, cb_ref, wd_ref, gf_ref,
                     prev_ref, y_ref, st_ref, acc, h2, *, final):
    j = pl.program_id(0)
    nj = pl.num_programs(0)

    @pl.when(j == 0)
    def _():
        x1 = (x_ref[...] + _dot(oa_ref[...].astype(BF16), wo_ref[0:256, :]) + _dot(ob_ref[...].astype(BF16), wo_ref[256:512, :])
              + _dot(oc_ref[...].astype(BF16), wo_ref[512:1024, :]))
        acc[...] = x1
        h2[...] = _rms(x1, g2_ref[...]).astype(BF16)

    h = h2[...]
    g = _dot(h, wg_ref[...])
    u = _dot(h, wu_ref[...])
    cw = cw_ref[...]
    gc = cw[2:3] * g + cw[1:2] * prev_ref[1] + cw[0:1] * prev_ref[0] + cb_ref[...]
    st_ref[0] = prev_ref[1]
    st_ref[1] = g
    act = (_silu(gc) * u).astype(BF16)
    acc[...] += _dot(act, wd_ref[...])

    @pl.when(j == nj - 1)
    def _():
        if final:
            y_ref[...] = _rms(acc[...], gf_ref[...])
        else:
            y_ref[...] = acc[...]


def _ffn_decode(x, o_a, o_b, o_c, prev, w_out, norm2_g, w_ffn_in, conv_w, conv_b, w_ffn_out, normf_g, final, tf=512):
    ns, d = x.shape
    nj = D_FF // tf
    kern = functools.partial(_ffn_decode_body, final=final)
    return pl.pallas_call(
        kern,
        grid=(nj,),
        in_specs=[_const_spec((ns, d)), _const_spec((ns, 256)), _const_spec((ns, 256)), _const_spec((ns, 512)),
                  _const_spec((d, d)), _const_spec((1, d)),
                  pl.BlockSpec((d, tf), lambda j: (0, j)),
                  pl.BlockSpec((d, tf), lambda j: (0, j + nj)),
                  pl.BlockSpec((FFN_CONV, tf), lambda j: (0, j)),
                  pl.BlockSpec((1, tf), lambda j: (0, j)),
                  pl.BlockSpec((tf, d), lambda j: (j, 0)),
                  _const_spec((1, d)),
                  pl.BlockSpec((FFN_CONV - 1, ns, tf), lambda j: (0, 0, j))],
        out_specs=[_const_spec((ns, d)), pl.BlockSpec((FFN_CONV - 1, ns, tf), lambda j: (0, 0, j))],
        out_shape=[jax.ShapeDtypeStruct((ns, d), F32), jax.ShapeDtypeStruct((FFN_CONV - 1, ns, D_FF), F32)],
        scratch_shapes=[pltpu.VMEM((ns, d), F32), pltpu.VMEM((ns, d), BF16)],
        compiler_params=_cp(1),
        name="ffn_decode",
    )(x, o_a, o_b, o_c, w_out, norm2_g.reshape(1, d), w_ffn_in, w_ffn_in, conv_w, conv_b.reshape(1, D_FF),
      w_ffn_out, normf_g.reshape(1, d), prev)


ROWS = 16
PAGES_PER_STEP = 8


def _decode_consts():
    j = np.arange(ROWS)[:, None]
    lane = np.arange(SOFT_W)[None, :]
    is_diff = j < 8
    is_moba = (j >= 8) & (j < 12)
    q_lo = np.where(is_diff, (j // 2) * HEAD_DIM + (j % 2) * DIFF_HALF, 256 + (j - 8) * HEAD_DIM)
    q_w = np.where(is_diff, DIFF_HALF, HEAD_DIM)
    qmask = (lane >= q_lo) & (lane < q_lo + q_w) & (is_diff | is_moba)
    qscale = np.where(is_diff, DIFF_HALF ** -0.5, HEAD_DIM ** -0.5) * LOG2E
    v_lo = np.where(is_diff, (j // 2) * HEAD_DIM, 256 + (j - 8) * HEAD_DIM)
    vmask = (lane >= v_lo) & (lane < v_lo + HEAD_DIM) & (is_diff | is_moba)
    head = np.where(is_diff, j // 2, j - 8)
    slope = np.where(is_diff, 2.0 ** -(2 * head + 1), np.where(is_moba, 2.0 ** -(2 * head + 2), 0.0)) * LOG2E
    coef_a = np.where((is_diff & (j % 2 == 0)) | is_moba, 1.0, 0.0)
    coef_b = np.where(is_diff & (j % 2 == 1), -1.0, 0.0)
    bc = lambda a: jnp.asarray(np.broadcast_to(a, (ROWS, LANES)).astype(np.float32))
    i = np.arange(SOFT_W)
    bd512 = (i[:, None] // HEAD_DIM == i[None, :] // HEAD_DIM).astype(np.float32)
    return (jnp.asarray((qmask * qscale).astype(np.float32)), jnp.asarray((qmask & is_moba).astype(np.float32)),
            jnp.asarray(vmask.astype(np.float32)), bc(slope), bc(coef_a), bc(coef_b), jnp.asarray(bd512))


def _decode_attn_body(pt_ref, lam_ref, q_ref, ks_ref, vs_ref, *rest, nblk, npg, page, past_len, out_scale):
    del pt_ref
    k_refs = rest[0:npg]
    v_refs = rest[npg:2 * npg]
    qmask_ref, gmask_ref, vmask_ref, slope_ref, ca_ref, cb_ref, bd_ref, g_ref, o_ref = rest[2 * npg:2 * npg + 9]
    m_scr, l_scr, acc_scr, mblk, lblk, gblk, accblk = rest[2 * npg + 9:]
    b = pl.program_id(1)
    nstep = pl.num_programs(1)
    tk = 2 * page

    @pl.when(b == 0)
    def _():
        m_scr[...] = jnp.full(m_scr.shape, -jnp.inf, F32)
        l_scr[...] = jnp.zeros(l_scr.shape, F32)
        acc_scr[...] = jnp.zeros(acc_scr.shape, F32)
        mblk[...] = jnp.full(mblk.shape, -jnp.inf, F32)
        lblk[...] = jnp.zeros(lblk.shape, F32)
        gblk[...] = jnp.full(gblk.shape, -jnp.inf, F32)

    qs = q_ref[...] * qmask_ref[...]
    qsb = qs.astype(BF16)
    qgb = (q_ref[...] * gmask_ref[...]).astype(BF16)
    slope = slope_ref[:, 0:1]
    lane = lax.broadcasted_iota(jnp.int32, (ROWS, LANES), 1)
    tok = lax.broadcasted_iota(jnp.int32, (1, tk), 1)

    nu = npg // 2
    raw = [jnp.concatenate([_dot(qsb, k_refs[2 * u][...].astype(BF16)), _dot(qsb, k_refs[2 * u + 1][...].astype(BF16))],
                           axis=1) for u in range(nu)]
    mbs, lbs, pes = [], [], []
    for u in range(nu):
        dist = (past_len - ((b * nu + u) * tk + tok)).astype(F32)
        s = raw[u] - slope * dist
        mb = jnp.max(s, axis=-1, keepdims=True)
        pexp = jnp.exp2(s - mb)
        mbs.append(mb)
        lbs.append(jnp.sum(pexp, axis=-1, keepdims=True))
        pes.append(pexp.astype(BF16))
    accbs = [_dot_nt(pes[u][:, 0:page], v_refs[2 * u][...].astype(BF16))
             + _dot_nt(pes[u][:, page:tk], v_refs[2 * u + 1][...].astype(BF16)) for u in range(nu)]
    kmeans = [jnp.sum(k_refs[2 * u][...] + k_refs[2 * u + 1][...], axis=-1, keepdims=True) * (1.0 / tk)
              for u in range(nu)]
    gates = [_dot(qgb, kmeans[u].astype(BF16)) for u in range(nu)]

    for u in range(nu):
        bb = b * nu + u
        mb, lb, accb = mbs[u], lbs[u], accbs[u]
        m_old = m_scr[:, 0:1]
        m_new = jnp.maximum(m_old, mb)
        a_old = jnp.exp2(m_old - m_new)
        a_blk = jnp.exp2(mb - m_new)
        l_scr[...] = jnp.broadcast_to(a_old * l_scr[:, 0:1] + a_blk * lb, l_scr.shape)
        acc_scr[...] = a_old * acc_scr[...] + a_blk * accb
        m_scr[...] = jnp.broadcast_to(m_new, m_scr.shape)
        hit = lane == bb
        mblk[...] = jnp.where(hit, mb, mblk[...])
        lblk[...] = jnp.where(hit, lb, lblk[...])
        gblk[...] = jnp.where(hit, gates[u], gblk[...])
        accblk[bb] = accb

    @pl.when(b == nstep - 1)
    def _():
        s_self = jnp.sum(qs * ks_ref[...], axis=-1, keepdims=True)
        vs = vs_ref[...]
        m_o = m_scr[:, 0:1]
        m_d = jnp.maximum(m_o, s_self)
        a_o = jnp.exp2(m_o - m_d)
        p_d = jnp.exp2(s_self - m_d)
        o_d = (a_o * acc_scr[...] + p_d * vs) / (a_o * l_scr[:, 0:1] + p_d)
        sel = _topk_select(gblk[...], lane, nblk, 1, LANES)
        mm = mblk[...]
        m_f = jnp.maximum(jnp.max(jnp.where(sel, mm, -jnp.inf), axis=-1, keepdims=True), s_self)
        wgt = jnp.where(sel, jnp.exp2(mm - m_f), 0.0)
        p_m = jnp.exp2(s_self - m_f)
        l_m = jnp.sum(wgt * lblk[...], axis=-1, keepdims=True) + p_m
        acc_m = p_m * vs
        for blk in range(nblk):
            acc_m = acc_m + wgt[:, blk:blk + 1] * accblk[blk]
        o_m = acc_m / l_m
        rowi = lax.broadcasted_iota(jnp.int32, (ROWS, SOFT_W), 0)
        o_all = jnp.where(rowi < 8, o_d, o_m)
        coef = ca_ref[:, 0:1] + lam_ref[0] * cb_ref[:, 0:1]
        o_row = jnp.sum(coef * vmask_ref[...] * o_all, axis=0, keepdims=True)
        ms = _dot(jnp.broadcast_to(o_row * o_row, (8, SOFT_W)), bd_ref[...], HI)[0:1] * (1.0 / HEAD_DIM)
        o_norm = o_row * lax.rsqrt(ms + EPS) * g_ref[...] * out_scale
        lane5 = lax.broadcasted_iota(jnp.int32, (1, SOFT_W), 1)
        o_ref[...] = jnp.where(lane5 < 256, o_norm, o_row)


def _decode_attn(layer, q_soft, k_self, v_self, cache_k, cache_v, page_table, lam, subln_g, lam_init):
    ns = q_soft.shape[0]
    n_pages = page_table.shape[1]
    page = cache_k.shape[3]
    npg = PAGES_PER_STEP
    assert 2 * page == MOBA_BLOCK and n_pages % npg == 0 and cache_k.shape[2] == SOFT_W
    nblk = n_pages // 2
    assert nblk <= LANES
    kern = functools.partial(_decode_attn_body, nblk=nblk, npg=npg, page=page, past_len=n_pages * page,
                             out_scale=1.0 - lam_init)
    row = pl.BlockSpec((None, 1, SOFT_W), lambda s, b, pt: (s, 0, 0))
    pg = lambda off: pl.BlockSpec((None, None, SOFT_W, page), lambda s, b, pt: (layer, pt[s, npg * b + off], 0, 0))
    cst = lambda shp: pl.BlockSpec(shp, lambda s, b, pt: (0,) * len(shp))
    g4 = jnp.concatenate([jnp.tile(subln_g.reshape(1, HEAD_DIM), (1, 4)), jnp.ones((1, 256), F32)], axis=1)
    grid_spec = pltpu.PrefetchScalarGridSpec(
        num_scalar_prefetch=1,
        grid=(ns, n_pages // npg),
        in_specs=([pl.BlockSpec(memory_space=pltpu.SMEM), row, row, row]
                  + [pg(u) for u in range(npg)] + [pg(u) for u in range(npg)]
                  + [cst((ROWS, SOFT_W)), cst((ROWS, SOFT_W)), cst((ROWS, SOFT_W)), cst((ROWS, LANES)), cst((ROWS, LANES)),
                     cst((ROWS, LANES)), cst((SOFT_W, SOFT_W)), cst((1, SOFT_W))]),
        out_specs=row,
        scratch_shapes=[pltpu.VMEM((ROWS, LANES), F32), pltpu.VMEM((ROWS, LANES), F32), pltpu.VMEM((ROWS, SOFT_W), F32),
                        pltpu.VMEM((ROWS, LANES), F32), pltpu.VMEM((ROWS, LANES), F32), pltpu.VMEM((ROWS, LANES), F32),
                        pltpu.VMEM((nblk, ROWS, SOFT_W), F32)],
    )
    out = pl.pallas_call(
        kern,
        grid_spec=grid_spec,
        out_shape=jax.ShapeDtypeStruct((ns, 1, SOFT_W), F32),
        compiler_params=_cp(2),
        name="decode_attn",
    )(page_table, lam.reshape(1), q_soft[:, None, :], k_self[:, None, :], v_self[:, None, :],
      *([cache_k] * npg), *([cache_v] * npg), *_decode_consts(), g4)
    return out[:, 0, :]


def _pages_transposed(cache):
    d, n_pool, page = cache.shape[0:3]
    return jnp.transpose(cache, (0, 1, 3, 4, 2)).reshape(d, n_pool, SOFT_W, page)


def _lam(l, lam_q1, lam_k1, lam_q2, lam_k2):
    lam_init = 0.8 - 0.6 * math.exp(-0.3 * l)
    lam = (jnp.exp(jnp.sum(lam_q1[l] * lam_k1[l])) - jnp.exp(jnp.sum(lam_q2[l] * lam_k2[l])) + lam_init)
    return lam.astype(F32), lam_init


def _prompt_layer(x, w, lam, lam_init, final):
    n, t, d = x.shape
    outs = _proj_in(x.reshape(n * t, d), w["norm1"], w["w_in"])
    qs, kr, vr, qkv, zg, b_x, a_x, ba = [a.reshape(n, t, a.shape[-1]) for a in outs]
    o_a = _diff_attn_prompt(qs, kr, vr, lam, w["subln"], lam_init)
    o_b = _moba_prompt(qs, kr, vr)
    o_c, s_bd = _gdn_prompt(qkv, zg, b_x, a_x, ba, w["dn_conv_w"], w["dn_a_log"], w["dn_dt_bias"], w["dn_norm"])
    y, ffn_new = _ffn_prompt(x, o_a, o_b, o_c, w["w_out"], w["norm2"], w["ffn_in"], w["ffn_conv_w"], w["ffn_conv_b"],
                             w["ffn_out"], w["normf"], final)
    rows = (kr.reshape(n, t, 8, HEAD_DIM), vr.reshape(n, t, 8, HEAD_DIM), _state_from_bd(s_bd),
            qkv[:, t - (DN_CONV - 1):, :], ffn_new)
    return y, rows


def _sample_layer(l, x, w, lam, lam_init, final, cache_k, cache_v, page_table, state_dn, conv_qkv, conv_ffn):
    ns, d = x.shape
    qs, kr, vr, qkv, zg, b_x, a_x, _ = _proj_in(x, w["norm1"], w["w_in"])
    o_ab = _decode_attn(l, qs, kr, vr, cache_k, cache_v, page_table, lam, w["subln"], lam_init)
    o_c, s_bd, conv_new = _gdn_decode(qkv, zg, b_x, a_x, conv_qkv, _state_to_bd(state_dn), w["dn_conv_w"], w["dn_a_log"],
                                      w["dn_dt_bias"], w["dn_norm"])
    y, ffn_new = _ffn_decode(x, o_ab[:, 0:256], o_ab[:, 256:512], o_c[:, 0, :], jnp.swapaxes(conv_ffn, 0, 1),
                             w["w_out"], w["norm2"], w["ffn_in"], w["ffn_conv_w"], w["ffn_conv_b"], w["ffn_out"],
                             w["normf"], final)
    rows = (kr.reshape(ns, 1, 8, HEAD_DIM), vr.reshape(ns, 1, 8, HEAD_DIM), _state_from_bd(s_bd), conv_new,
            jnp.swapaxes(ffn_new, 0, 1))
    return y, rows


def kernel(x_prompt, x_sample, cache_k, cache_v, page_table, state_dn, state_conv_qkv, state_conv_ffn, norm1_g, norm2_g, normf_g, w_in, w_out, lam_q1, lam_k1, lam_q2, lam_k2, subln_g, dn_conv_w, dn_a_log, dn_dt_bias, dn_norm_g, ffn_w_in, ffn_conv_w, ffn_conv_b, ffn_w_out):
    depth = w_in.shape[0]
    cache_k = _pages_transposed(cache_k)
    cache_v = _pages_transposed(cache_v)
    xp = x_prompt
    xs = x_sample.reshape(x_sample.shape[0], x_sample.shape[2])
    rows_p, rows_s = [], []
    for l in range(depth):
        w = {"norm1": norm1_g[l], "w_in": _permute_w_in(w_in[l]), "subln": subln_g[l], "dn_conv_w": dn_conv_w[l],
             "dn_a_log": dn_a_log[l], "dn_dt_bias": dn_dt_bias[l], "dn_norm": dn_norm_g[l],
             "w_out": w_out[l].astype(BF16), "norm2": norm2_g[l], "ffn_in": ffn_w_in[l].astype(BF16),
             "ffn_conv_w": ffn_conv_w[l], "ffn_conv_b": ffn_conv_b[l], "ffn_out": ffn_w_out[l].astype(BF16),
             "normf": normf_g}
        lam, lam_init = _lam(l, lam_q1, lam_k1, lam_q2, lam_k2)
        final = l == depth - 1
        xp, rp = _prompt_layer(xp, w, lam, lam_init, final)
        xs, rs = _sample_layer(l, xs, w, lam, lam_init, final, cache_k, cache_v, page_table, state_dn[l],
                               state_conv_qkv[l], state_conv_ffn[l])
        rows_p.append(rp)
        rows_s.append(rs)
    stack = lambda rows, i: jnp.stack([r[i] for r in rows], axis=0)
    y_sample = xs.reshape(x_sample.shape)
    return (xp, y_sample,
            stack(rows_p, 0), stack(rows_p, 1), stack(rows_p, 2), stack(rows_p, 3), stack(rows_p, 4),
            stack(rows_s, 0), stack(rows_s, 1), stack(rows_s, 2), stack(rows_s, 3), stack(rows_s, 4))
```

```python
import functools
import math

import numpy as np
import jax
import jax.numpy as jnp
from jax import lax
from jax.experimental import pallas as pl
from jax.experimental.pallas import tpu as pltpu

F32 = jnp.float32
BF16 = jnp.bfloat16
HI = lax.Precision.HIGHEST

LANES = 128
HEAD_DIM = 64
DIFF_HALF = 32
N_DIFF_HEADS = 4
N_MOBA_HEADS = 4
N_SOFT_HEADS = 8
N_DN_HEADS = 8
N_PAIRS = 4
SOFT_W = 512
DN_W = 512
D_FF = 3584
MOBA_BLOCK = 256
MOBA_TOPK = 3
DN_CONV = 4
FFN_CONV = 3
EPS = 1e-6
LOG2E = 1.4426950408889634
NEG_BIG = -1e30
Z_W = 3584 + 2 * DN_W + LANES
VMEM_LIMIT = 56 * 1024 * 1024


def _cp(n_axes, vmem=VMEM_LIMIT):
    return pltpu.CompilerParams(dimension_semantics=("arbitrary",) * n_axes, vmem_limit_bytes=vmem)


def _sigmoid(x):
    return 1.0 / (1.0 + jnp.exp(-x))


def _silu(x):
    return x * _sigmoid(x)


def _softplus(x):
    return jnp.maximum(x, 0.0) + jnp.log(1.0 + jnp.exp(-jnp.abs(x)))


def _dot(a, b, prec=None):
    return jnp.dot(a, b, preferred_element_type=F32, precision=prec)


def _dot_nt(a, b, prec=None):
    return lax.dot_general(a, b, (((1,), (1,)), ((), ())), preferred_element_type=F32, precision=prec)


def _dot_tn(a, b, prec=None):
    return lax.dot_general(a, b, (((0,), (0,)), ((), ())), preferred_element_type=F32, precision=prec)


def _split2(x):
    hi = x.astype(BF16)
    return hi, (x - hi.astype(F32)).astype(BF16)


def _mm3(a, b, dot=_dot):
    ah, al = _split2(a)
    bh, bl = _split2(b)
    return dot(ah, bh) + dot(ah, bl) + dot(al, bh)


def _mm2(a, b01, dot=_dot):
    ah, al = _split2(a)
    return dot(ah, b01) + dot(al, b01)


def _const_spec(shape):
    nd = len(shape)
    return pl.BlockSpec(shape, lambda *_: (0,) * nd)


def _bd_ones(dtype=F32):
    i = np.arange(LANES)
    return jnp.asarray((i[:, None] // HEAD_DIM == i[None, :] // HEAD_DIM).astype(np.float32)).astype(dtype)


_PROJ_WIDTHS = (512, 512, 512, 1536, 512, 512, 512, LANES)


def _proj_in_body(x_ref, g_ref, w_ref, *out_refs, transposed_kv):
    x = x_ref[...]
    ms = jnp.mean(x * x, axis=-1, keepdims=True)
    h = (x * lax.rsqrt(ms + EPS) * g_ref[...]).astype(BF16)
    lo = 0
    for idx, w in enumerate(_PROJ_WIDTHS):
        z = _dot(h, w_ref[:, lo:lo + w])
        out_refs[idx][...] = z
        if transposed_kv and idx in (1, 2):
            out_refs[len(_PROJ_WIDTHS) + idx - 1][0] = z.T
        lo += w


def _proj_in(x2d, g, w_perm, seq_len=None):
    m, d = x2d.shape
    tm = min(m, 256)
    out_specs = [pl.BlockSpec((tm, w), lambda i: (i, 0)) for w in _PROJ_WIDTHS]
    out_shape = [jax.ShapeDtypeStruct((m, w), F32) for w in _PROJ_WIDTHS]
    if seq_len is not None:
        tps = seq_len // tm
        out_specs += [pl.BlockSpec((1, SOFT_W, tm), lambda i: (i // tps, 0, i % tps))] * 2
        out_shape += [jax.ShapeDtypeStruct((m // seq_len, SOFT_W, seq_len), F32)] * 2
    return pl.pallas_call(
        functools.partial(_proj_in_body, transposed_kv=seq_len is not None),
        grid=(m // tm,),
        in_specs=[pl.BlockSpec((tm, d), lambda i: (i, 0)), _const_spec((1, d)), _const_spec((d, Z_W))],
        out_specs=out_specs,
        out_shape=out_shape,
        compiler_params=_cp(1),
        name="proj_in",
    )(x2d, g.reshape(1, d), w_perm)


def _permute_w_in(w):
    cols = [w[:, 0:256], w[:, 768:1024], w[:, 256:512], w[:, 1024:1280], w[:, 512:768], w[:, 1280:1536],
            w[:, 1536:3584], jnp.repeat(w[:, 3584:3592], HEAD_DIM, axis=1), jnp.repeat(w[:, 3592:3600], HEAD_DIM, axis=1),
            w[:, 3584:3600], jnp.zeros((w.shape[0], LANES - 2 * N_DN_HEADS), w.dtype)]
    return jnp.concatenate(cols, axis=1).astype(BF16)


def _prep_kv(k_ref, v_ref, kb_scr, vt_scr, t, tq):
    kb_scr[...] = k_ref[0].astype(BF16)
    row_head = lax.broadcasted_iota(jnp.int32, (LANES, tq), 0) // HEAD_DIM
    for c in range(t // tq):
        vt = v_ref[0, c * tq:(c + 1) * tq, :].T
        for hl in range(2):
            vt_scr[hl, c] = jnp.where(row_head == hl, vt, 1.0).astype(BF16)


def _denominator(acc_ref, j, hl):
    r = (1 - hl) * HEAD_DIM
    return acc_ref[j, r:r + 1, :]


def _online_stats_t(j, sp, shift, m_scr):
    m_old = m_scr[j]
    m_new = jnp.maximum(m_old, jnp.max(sp, axis=0, keepdims=True) + shift)
    p = jnp.exp2(sp - (m_new - shift))
    m_scr[j] = m_new
    return p.astype(BF16), jnp.exp2(m_old - m_new)


def _diff_attn_body(lam_ref, q_ref, k_ref, v_ref, g_ref, o_ref,
                    kb_scr, vt_scr, qm_scr, sa_scr, sb_scr, m_scr, acc_scr, *, t, tq, out_scale):
    p = pl.program_id(1)
    qi = pl.program_id(2)

    @pl.when(qi == 0)
    def _():
        _prep_kv(k_ref, v_ref, kb_scr, vt_scr, t, tq)

    lane = lax.broadcasted_iota(jnp.int32, (tq, LANES), 1)
    q = q_ref[0] * (DIFF_HALF ** -0.5 * LOG2E)
    for j in range(4):
        lo = (j // 2) * HEAD_DIM + (j % 2) * DIFF_HALF
        qm_scr[j] = jnp.where((lane >= lo) & (lane < lo + DIFF_HALF), q, 0.0).astype(BF16)
    m_scr[...] = jnp.full(m_scr.shape, -jnp.inf, F32)
    acc_scr[...] = jnp.zeros(acc_scr.shape, F32)
    slopes = [jnp.where(p == 0, LOG2E * 2.0 ** -(2 * hl + 1), LOG2E * 2.0 ** -(2 * (hl + 2) + 1)).astype(F32)
              for hl in range(2)]
    krow = lax.broadcasted_iota(jnp.int32, (tq, tq), 0)
    qcol = lax.broadcasted_iota(jnp.int32, (tq, tq), 1)
    krow_f = krow.astype(F32)
    ramps = [slopes[hl] * krow_f for hl in range(2)]

    def scores(ki, s_ref):
        kt = kb_scr[pl.ds(pl.multiple_of(ki * tq, tq), tq), :]
        for j in range(4):
            s_ref[j] = _dot_nt(kt, qm_scr[j])

    def softmax_pv(ki, s_ref, masked):
        off = ((ki - qi) * tq).astype(F32)
        pa = []
        for j in range(4):
            sp = s_ref[j] + ramps[j // 2]
            if masked:
                sp = jnp.where(krow <= qcol, sp, -jnp.inf)
            pa.append(_online_stats_t(j, sp, slopes[j // 2] * off, m_scr))
        for j in range(4):
            acc_scr[j] = pa[j][1] * acc_scr[j] + _dot(vt_scr[j // 2, ki], pa[j][0])

    scores(0, sa_scr)

    def body(i, carry):
        ki = 2 * i
        scores(ki + 1, sb_scr)
        softmax_pv(ki, sa_scr, False)
        scores(ki + 2, sa_scr)
        softmax_pv(ki + 1, sb_scr, False)
        return carry

    lax.fori_loop(0, qi // 2, body, 0)

    @pl.when(qi % 2 == 1)
    def _():
        scores(qi, sb_scr)
        softmax_pv(qi - 1, sa_scr, False)
        softmax_pv(qi, sb_scr, True)

    @pl.when(qi % 2 == 0)
    def _():
        softmax_pv(qi, sa_scr, True)

    lam = lam_ref[0]
    outs = []
    for hl in range(2):
        r = slice(hl * HEAD_DIM, (hl + 1) * HEAD_DIM)
        o = (acc_scr[2 * hl, r, :] * (1.0 / _denominator(acc_scr, 2 * hl, hl))
             - lam * (acc_scr[2 * hl + 1, r, :] * (1.0 / _denominator(acc_scr, 2 * hl + 1, hl))))
        ms = jnp.sum(o * o, axis=0, keepdims=True) * (1.0 / HEAD_DIM)
        outs.append(o * lax.rsqrt(ms + EPS))
    ot = jnp.concatenate(outs, axis=0) * g_ref[...] * out_scale
    o_ref[0] = ot.T


def _diff_attn_prompt(q_soft, k_rows, v_rows, lam, subln_g, lam_init, tq=256):
    n, t, _ = q_soft.shape
    tq = min(tq, t)
    g2 = jnp.tile(subln_g.reshape(HEAD_DIM, 1), (2, 1))
    kern = functools.partial(_diff_attn_body, t=t, tq=tq, out_scale=1.0 - lam_init)
    return pl.pallas_call(
        kern,
        grid=(n, 2, t // tq),
        in_specs=[pl.BlockSpec(memory_space=pltpu.SMEM),
                  pl.BlockSpec((1, tq, LANES), lambda b, p, i: (b, i, p)),
                  pl.BlockSpec((1, t, LANES), lambda b, p, i: (b, 0, p)),
                  pl.BlockSpec((1, t, LANES), lambda b, p, i: (b, 0, p)),
                  _const_spec((LANES, 1))],
        out_specs=pl.BlockSpec((1, tq, LANES), lambda b, p, i: (b, i, p)),
        out_shape=jax.ShapeDtypeStruct((n, t, 2 * LANES), F32),
        scratch_shapes=[pltpu.VMEM((t, LANES), BF16), pltpu.VMEM((2, t // tq, LANES, tq), BF16),
                        pltpu.VMEM((4, tq, LANES), BF16), pltpu.VMEM((4, tq, tq), F32), pltpu.VMEM((4, tq, tq), F32),
                        pltpu.VMEM((4, 1, tq), F32),
                        pltpu.VMEM((4, LANES, tq), F32)],
        compiler_params=_cp(3),
        name="diff_attn_prompt",
    )(lam.reshape(1), q_soft, k_rows, v_rows, g2)


def _topk_select(gate, idx, n_valid, axis, size):
    gm = jnp.where(idx < n_valid, gate, -jnp.inf)
    sel = jnp.zeros(gate.shape, jnp.bool_)
    for _ in range(MOBA_TOPK):
        mx = jnp.max(gm, axis=axis, keepdims=True)
        is_max = (gm == mx) & (mx > -jnp.inf)
        first = jnp.min(jnp.where(is_max, idx, size), axis=axis, keepdims=True)
        pick = idx == first
        sel = sel | pick
        gm = jnp.where(pick, -jnp.inf, gm)
    return sel


def _moba_body(q_ref, k_ref, v_ref, o_ref, kb_scr, vt_scr, kmean_scr, qs_scr, sel_scr, sd_scr, sa_scr, sb_scr,
               m_scr, acc_scr, *, t, tq, nb):
    nbp = kmean_scr.shape[0]
    p = pl.program_id(1)
    qi = pl.program_id(2)

    @pl.when(qi == 0)
    def _():
        _prep_kv(k_ref, v_ref, kb_scr, vt_scr, t, tq)
        kmean_scr[...] = jnp.zeros(kmean_scr.shape, F32)
        for b in range(nb):
            kmean_scr[b:b + 1, :] = jnp.sum(k_ref[0, b * MOBA_BLOCK:(b + 1) * MOBA_BLOCK, :], axis=0,
                                            keepdims=True) * (1.0 / MOBA_BLOCK)

    lane = lax.broadcasted_iota(jnp.int32, (tq, LANES), 1)
    blk = lax.broadcasted_iota(jnp.int32, (nbp, tq), 0)
    q = q_ref[0]
    kmean = kmean_scr[...]
    for hl in range(2):
        qh = jnp.where((lane >= hl * HEAD_DIM) & (lane < (hl + 1) * HEAD_DIM), q, 0.0)
        gate_t = _dot_nt(kmean.astype(BF16), qh.astype(BF16))
        sel = _topk_select(gate_t, blk, qi, 0, nbp)
        sel_scr[hl] = jnp.where(sel, 0.0, NEG_BIG)
        qs_scr[hl] = (qh * (HEAD_DIM ** -0.5 * LOG2E)).astype(BF16)
    m_scr[...] = jnp.full(m_scr.shape, -jnp.inf, F32)
    acc_scr[...] = jnp.zeros(acc_scr.shape, F32)
    slopes = [jnp.where(p == 0, LOG2E * 2.0 ** -(2 * hl + 2), LOG2E * 2.0 ** -(2 * (hl + 2) + 2)).astype(F32)
              for hl in range(2)]
    krow = lax.broadcasted_iota(jnp.int32, (tq, tq), 0)
    qcol = lax.broadcasted_iota(jnp.int32, (tq, tq), 1)
    krow_f = krow.astype(F32)
    ramps = [slopes[hl] * krow_f for hl in range(2)]

    def scores(ki, s_ref):
        kt = kb_scr[pl.ds(pl.multiple_of(ki * tq, tq), tq), :]
        for hl in range(2):
            s_ref[hl] = _dot_nt(kt, qs_scr[hl])

    def softmax_pv(ki, s_ref, diagonal):
        off = ((ki - qi) * tq).astype(F32)
        pa = []
        for hl in range(2):
            sp = s_ref[hl] + ramps[hl]
            shift = slopes[hl] * off
            if diagonal:
                sp = jnp.where(krow <= qcol, sp, -jnp.inf)
            else:
                shift = shift + sel_scr[hl, pl.ds(ki, 1), :]
            pa.append(_online_stats_t(hl, sp, shift, m_scr))
        for hl in range(2):
            acc_scr[hl] = pa[hl][1] * acc_scr[hl] + _dot(vt_scr[hl, ki], pa[hl][0])

    scores(qi, sd_scr)
    scores(0, sa_scr)
    softmax_pv(qi, sd_scr, True)

    def body(i, carry):
        ki = 2 * i
        scores(ki + 1, sb_scr)
        softmax_pv(ki, sa_scr, False)
        scores(ki + 2, sa_scr)
        softmax_pv(ki + 1, sb_scr, False)
        return carry

    lax.fori_loop(0, qi // 2, body, 0)

    @pl.when(qi % 2 == 1)
    def _():
        softmax_pv(qi - 1, sa_scr, False)

    outs = []
    for hl in range(2):
        outs.append(acc_scr[hl, hl * HEAD_DIM:(hl + 1) * HEAD_DIM, :] * (1.0 / _denominator(acc_scr, hl, hl)))
    o_ref[0] = jnp.concatenate(outs, axis=0).T


def _moba_prompt(q_soft, k_rows, v_rows):
    n, t, _ = q_soft.shape
    tq = MOBA_BLOCK
    assert t % tq == 0 and t // tq <= LANES
    nb = t // tq
    nbp = -(-nb // 8) * 8
    kern = functools.partial(_moba_body, t=t, tq=tq, nb=nb)
    return pl.pallas_call(
        kern,
        grid=(n, 2, nb),
        in_specs=[pl.BlockSpec((1, tq, LANES), lambda b, p, i: (b, i, 2 + p)),
                  pl.BlockSpec((1, t, LANES), lambda b, p, i: (b, 0, 2 + p)),
                  pl.BlockSpec((1, t, LANES), lambda b, p, i: (b, 0, 2 + p))],
        out_specs=pl.BlockSpec((1, tq, LANES), lambda b, p, i: (b, i, p)),
        out_shape=jax.ShapeDtypeStruct((n, t, 2 * LANES), F32),
        scratch_shapes=[pltpu.VMEM((t, LANES), BF16), pltpu.VMEM((2, nb, LANES, tq), BF16),
                        pltpu.VMEM((nbp, LANES), F32),
                        pltpu.VMEM((2, tq, LANES), BF16), pltpu.VMEM((2, nbp, tq), F32),
                        pltpu.VMEM((2, tq, tq), F32), pltpu.VMEM((2, tq, tq), F32), pltpu.VMEM((2, tq, tq), F32),
                        pltpu.VMEM((2, 1, tq), F32),
                        pltpu.VMEM((2, LANES, tq), F32)],
        compiler_params=_cp(3),
        name="moba_prompt",
    )(q_soft, k_rows, v_rows)


def _bdiag(x, lane_lo):
    return jnp.concatenate([jnp.where(lane_lo, x, 0.0), jnp.where(lane_lo, 0.0, x)], axis=0)


def _gdn_prompt_body(qkv_ref, zg_ref, bx_ref, ax_ref, arow_ref, cw_ref, alogx_ref, dtbx_ref, alogr_ref, dtbr_ref,
                     gn_ref, bd_ref, ltri_ref, ubd_ref,
                     o_ref, sfin_ref, xbuf, s_scr, *, c, nseq):
    ci = pl.program_id(0)
    nc = pl.num_programs(0)

    @pl.when(ci == 0)
    def _():
        xbuf[:, 0:8, :] = jnp.zeros((nseq, 8, 3 * DN_W), F32)
        s_scr[...] = jnp.zeros(s_scr.shape, F32)

    ltri = ltri_ref[...]
    ubd = ubd_ref[...]
    bd = bd_ref[...]
    cw = cw_ref[...]
    row = lax.broadcasted_iota(jnp.int32, (c, LANES), 0)
    lane = lax.broadcasted_iota(jnp.int32, (c, LANES), 1)
    col = lane % HEAD_DIM
    lane_lo = lane < HEAD_DIM
    incl = row >= col
    strict = row > col
    eye2 = jnp.where(row == col, 1.0, 0.0)
    r128 = lax.broadcasted_iota(jnp.int32, (LANES, LANES), 0)
    c128 = lax.broadcasted_iota(jnp.int32, (LANES, LANES), 1)
    same_head = (r128 // HEAD_DIM) == (c128 // HEAD_DIM)
    steps = max(1, int(math.ceil(math.log2(c))) - 1)

    def split3_dot(x, lhs01=None, rhs01=None):
        x0, x1 = _split2(x)
        x2 = (x - x0.astype(F32) - x1.astype(F32)).astype(BF16)
        if lhs01 is not None:
            return _dot(lhs01, x0) + _dot(lhs01, x1) + _dot(lhs01, x2)
        return _dot(x0, rhs01) + _dot(x1, rhs01) + _dot(x2, rhs01)

    cq, beta_x, gc_x, gc_row = [], [], [], []
    for b in range(nseq):
        xbuf[b, 8:8 + c, :] = qkv_ref[b]
        y = (cw[3:4] * xbuf[b, 8:8 + c, :] + cw[2:3] * xbuf[b, 7:7 + c, :]
             + cw[1:2] * xbuf[b, 6:6 + c, :] + cw[0:1] * xbuf[b, 5:5 + c, :])
        xbuf[b, 0:8, :] = xbuf[b, c:c + 8, :]
        cq.append(_silu(y))
        beta_x.append(_sigmoid(bx_ref[b]))
        g_x = -jnp.exp(alogx_ref[...]) * _softplus(ax_ref[b] + dtbx_ref[...])
        g_row = -jnp.exp(alogr_ref[...]) * _softplus(arow_ref[b, 0] + dtbr_ref[...])
        gc_x.append(split3_dot(g_x, lhs01=ltri))
        gc_row.append(split3_dot(g_row, rhs01=ubd))

    chains = [(b, p) for b in range(nseq) for p in range(N_PAIRS)]
    pair = lambda arr, p, off=0: arr[:, off + p * LANES:off + (p + 1) * LANES]
    qp = [pair(cq[b], p) for b, p in chains]
    kp = [pair(cq[b], p, DN_W) for b, p in chains]
    vp = [pair(cq[b], p, 2 * DN_W) for b, p in chains]
    bexp = [pair(beta_x[b], p) for b, p in chains]
    gcx = [pair(gc_x[b], p) for b, p in chains]
    gcr = [gc_row[b][p:p + 1, :] for b, p in chains]
    n_ch = len(chains)
    rng = range(n_ch)

    ssq_q = [_mm2(qp[i] * qp[i], bd) for i in rng]
    ssq_k = [_mm2(kp[i] * kp[i], bd) for i in rng]
    qn = [qp[i] * lax.rsqrt(ssq_q[i] + EPS) * (HEAD_DIM ** -0.5) for i in rng]
    kn = [kp[i] * lax.rsqrt(ssq_k[i] + EPS) for i in rng]
    decay = [jnp.exp(jnp.where(incl, gcx[i] - gcr[i], -jnp.inf)) for i in rng]
    kq = [_mm3(jnp.concatenate([kn[i], qn[i]], axis=0), _bdiag(kn[i], lane_lo), _dot_nt) for i in rng]
    qk = [jnp.where(incl, kq[i][c:2 * c] * decay[i], 0.0) for i in rng]
    bpow = [-jnp.where(strict, bexp[i] * kq[i][0:c] * decay[i], 0.0) for i in rng]
    x = [eye2 + bpow[i] for i in rng]
    def mm3_parts(a, b_parts):
        ah, al = _split2(a)
        bh, bl = b_parts
        return _dot(ah, bh) + _dot(ah, bl) + _dot(al, bh)

    pow_bd = [_split2(_bdiag(bpow[i], lane_lo)) for i in rng]
    for _ in range(steps):
        bpow = [mm3_parts(bpow[i], pow_bd[i]) for i in rng]
        pow_bd = [_split2(_bdiag(bpow[i], lane_lo)) for i in rng]
        x = [x[i] + mm3_parts(x[i], pow_bd[i]) for i in rng]
    eg = [jnp.exp(gcx[i]) for i in rng]
    uw = [_mm3(x[i], jnp.concatenate([_bdiag(vp[i] * bexp[i], lane_lo), _bdiag(kn[i] * bexp[i] * eg[i], lane_lo)], axis=1))
          for i in rng]
    s_old = [s_scr[b, p] for b, p in chains]
    ws = [_mm3(jnp.concatenate([uw[i][:, LANES:2 * LANES], qn[i] * eg[i]], axis=0), s_old[i]) for i in rng]
    v_new = [uw[i][:, 0:LANES] - ws[i][0:c] for i in rng]
    glast = [gcx[i][c - 1:c, :] for i in rng]
    intra = [_mm3(qk[i], _bdiag(v_new[i], lane_lo)) for i in rng]
    upd = [_mm3(kn[i] * jnp.exp(glast[i] - gcx[i]), v_new[i], _dot_tn) for i in rng]
    o = [ws[i][c:2 * c] + intra[i] for i in rng]
    ms = [_mm2(o[i] * o[i], bd) * (1.0 / HEAD_DIM) for i in rng]
    for i, (b, p) in enumerate(chains):
        sl = slice(p * LANES, (p + 1) * LANES)
        s_scr[b, p] = s_old[i] * jnp.exp(glast[i]) + jnp.where(same_head, upd[i], 0.0)
        o_ref[b, :, sl] = o[i] * lax.rsqrt(ms[i] + EPS) * gn_ref[...] * _silu(zg_ref[b, :, sl])

    @pl.when(ci == nc - 1)
    def _():
        sfin_ref[...] = s_scr[...]


def _gdn_prompt(qkv, zg, b_x, a_x, ba, conv_w, a_log, dt_bias, norm_g, c=64):
    n, t, _ = qkv.shape
    nc = t // c
    a_raw = ba[..., N_DN_HEADS:2 * N_DN_HEADS]
    a_row = a_raw.reshape(n, nc, c, N_PAIRS, 2).transpose(0, 1, 3, 4, 2).reshape(n, nc, N_PAIRS, 2 * c)
    rep = lambda v: jnp.repeat(v.reshape(N_PAIRS, 2), c, axis=1)
    lanes = lambda v: jnp.repeat(v, HEAD_DIM).reshape(1, DN_W)
    i = np.arange(c)
    ltri = jnp.asarray((i[:, None] >= i[None, :]).astype(np.float32)).astype(BF16)
    j = np.arange(2 * c)
    ubd = jnp.asarray(((j[:, None] // c == j[None, :] // c) & (j[:, None] % c <= j[None, :] % c)).astype(np.float32))
    assert 2 * c == LANES
    kern = functools.partial(_gdn_prompt_body, c=c, nseq=n)
    tok = lambda w: pl.BlockSpec((n, c, w), lambda i: (0, i, 0))
    return pl.pallas_call(
        kern,
        grid=(nc,),
        in_specs=[tok(3 * DN_W), tok(DN_W), tok(DN_W), tok(DN_W),
                  pl.BlockSpec((n, 1, N_PAIRS, LANES), lambda i: (0, i, 0, 0)),
                  _const_spec((DN_CONV, 3 * DN_W)),
                  _const_spec((1, DN_W)), _const_spec((1, DN_W)),
                  _const_spec((N_PAIRS, LANES)), _const_spec((N_PAIRS, LANES)),
                  _const_spec((1, LANES)), _const_spec((LANES, LANES)),
                  _const_spec((c, c)), _const_spec((LANES, LANES))],
        out_specs=[tok(DN_W), _const_spec((n, N_PAIRS, LANES, LANES))],
        out_shape=[jax.ShapeDtypeStruct((n, t, DN_W), F32),
                   jax.ShapeDtypeStruct((n, N_PAIRS, LANES, LANES), F32)],
        scratch_shapes=[pltpu.VMEM((n, c + 8, 3 * DN_W), F32), pltpu.VMEM((n, N_PAIRS, LANES, LANES), F32)],
        compiler_params=_cp(1),
        name="gdn_prompt",
    )(qkv, zg, b_x, a_x, a_row, conv_w, lanes(a_log), lanes(dt_bias),
      rep(a_log), rep(dt_bias), jnp.tile(norm_g.reshape(1, HEAD_DIM), (1, 2)), _bd_ones(BF16), ltri, ubd.astype(BF16))


def _state_to_bd(s):
    n = s.shape[0]
    s = s.reshape(n, N_PAIRS, 2, HEAD_DIM, HEAD_DIM)
    z = jnp.zeros_like(s[:, :, 0])
    top = jnp.concatenate([s[:, :, 0], z], axis=-1)
    bot = jnp.concatenate([z, s[:, :, 1]], axis=-1)
    return jnp.concatenate([top, bot], axis=-2)


def _state_from_bd(sbd):
    n = sbd.shape[0]
    s = jnp.stack([sbd[:, :, :HEAD_DIM, :HEAD_DIM], sbd[:, :, HEAD_DIM:, HEAD_DIM:]], axis=2)
    return s.reshape(n, N_DN_HEADS, HEAD_DIM, HEAD_DIM)


def _gdn_decode_body(qkv_ref, cb_ref, zg_ref, bx_ref, ax_ref, cw_ref, alogx_ref, dtbx_ref, gn_ref, bd_ref,
                     s_ref, o_ref, snew_ref, cnew_ref):
    x = qkv_ref[...]
    buf = cb_ref[...]
    cw = cw_ref[...]
    y = cw[0:1] * buf[0:1] + cw[1:2] * buf[1:2] + cw[2:3] * buf[2:3] + cw[3:4] * x
    cnew_ref[...] = jnp.concatenate([buf[1:3], x], axis=0)
    cq = _silu(y)
    bexp = _sigmoid(bx_ref[...])
    eg = jnp.exp(-jnp.exp(alogx_ref[...]) * _softplus(ax_ref[...] + dtbx_ref[...]))
    bd = bd_ref[...]
    r128 = lax.broadcasted_iota(jnp.int32, (LANES, LANES), 0)
    c128 = lax.broadcasted_iota(jnp.int32, (LANES, LANES), 1)
    eye = r128 == c128
    rows8 = lambda v: jnp.broadcast_to(v, (8, v.shape[-1]))
    pr = range(N_PAIRS)
    sl = [slice(p * LANES, (p + 1) * LANES) for p in pr]
    qp = [cq[:, p * LANES:(p + 1) * LANES] for p in pr]
    kp = [cq[:, DN_W + p * LANES:DN_W + (p + 1) * LANES] for p in pr]
    vp = [cq[:, 2 * DN_W + p * LANES:2 * DN_W + (p + 1) * LANES] for p in pr]
    ssq_q = [_mm2(rows8(qp[p] * qp[p]), bd)[0:1] for p in pr]
    ssq_k = [_mm2(rows8(kp[p] * kp[p]), bd)[0:1] for p in pr]
    qn = [qp[p] * lax.rsqrt(ssq_q[p] + EPS) * (HEAD_DIM ** -0.5) for p in pr]
    kn = [kp[p] * lax.rsqrt(ssq_k[p] + EPS) for p in pr]
    s0 = [s_ref[p] * eg[:, sl[p]] for p in pr]
    ks = [_mm3(rows8(kn[p]), s0[p])[0:1] for p in pr]
    kcol = [_mm2(jnp.where(eye, jnp.broadcast_to(kn[p], (LANES, LANES)), 0.0), bd) for p in pr]
    s1 = [s0[p] + kcol[p] * ((vp[p] - ks[p]) * bexp[:, sl[p]]) for p in pr]
    o = [_mm3(rows8(qn[p]), s1[p])[0:1] for p in pr]
    ms = [_mm2(rows8(o[p] * o[p]), bd)[0:1] * (1.0 / HEAD_DIM) for p in pr]
    for p in pr:
        snew_ref[p] = s1[p]
        o_ref[:, sl[p]] = o[p] * lax.rsqrt(ms[p] + EPS) * gn_ref[...] * _silu(zg_ref[:, sl[p]])


def _gdn_decode(qkv, zg, b_x, a_x, conv_buf, s_bd, conv_w, a_log, dt_bias, norm_g):
    ns = qkv.shape[0]
    row = lambda w: pl.BlockSpec((None, 1, w), lambda i: (i, 0, 0))
    lanes = lambda v: jnp.repeat(v, HEAD_DIM).reshape(1, DN_W)
    return pl.pallas_call(
        _gdn_decode_body,
        grid=(ns,),
        in_specs=[row(3 * DN_W), pl.BlockSpec((None, DN_CONV - 1, 3 * DN_W), lambda i: (i, 0, 0)), row(DN_W),
                  row(DN_W), row(DN_W),
                  _const_spec((DN_CONV, 3 * DN_W)), _const_spec((1, DN_W)), _const_spec((1, DN_W)),
                  _const_spec((1, LANES)), _const_spec((LANES, LANES)),
                  pl.BlockSpec((None, N_PAIRS, LANES, LANES), lambda i: (i, 0, 0, 0))],
        out_specs=[row(DN_W), pl.BlockSpec((None, N_PAIRS, LANES, LANES), lambda i: (i, 0, 0, 0)),
                   pl.BlockSpec((None, DN_CONV - 1, 3 * DN_W), lambda i: (i, 0, 0))],
        out_shape=[jax.ShapeDtypeStruct((ns, 1, DN_W), F32),
                   jax.ShapeDtypeStruct((ns, N_PAIRS, LANES, LANES), F32),
                   jax.ShapeDtypeStruct((ns, DN_CONV - 1, 3 * DN_W), F32)],
        compiler_params=_cp(1),
        name="gdn_decode",
    )(qkv[:, None, :], conv_buf, zg[:, None, :], b_x[:, None, :], a_x[:, None, :], conv_w, lanes(a_log), lanes(dt_bias),
      jnp.tile(norm_g.reshape(1, HEAD_DIM), (1, 2)), _bd_ones(BF16), s_bd)


def _rms(x, g):
    return x * lax.rsqrt(jnp.mean(x * x, axis=-1, keepdims=True) + EPS) * g


def _ffn_prompt_body(x_ref, oa_ref, ob_ref, oc_ref, wo_ref, g2_ref, wg_ref, wu_ref, cw_ref, cb_ref, wd_ref, gf_ref,
                     y_ref, st_ref, acc, h2, gbuf, carry, *, tm, final):
    i = pl.program_id(1)
    j = pl.program_id(2)
    nj = pl.num_programs(2)

    @pl.when(j == 0)
    def _():
        x1 = (x_ref[0] + _dot(oa_ref[0].astype(BF16), wo_ref[0:256, :]) + _dot(ob_ref[0].astype(BF16), wo_ref[256:512, :])
              + _dot(oc_ref[0].astype(BF16), wo_ref[512:1024, :]))
        acc[...] = x1
        h2[...] = _rms(x1, g2_ref[...]).astype(BF16)

    @pl.when(i == 0)
    def _():
        carry[j] = jnp.zeros(carry.shape[1:], F32)

    h = h2[...]
    g = _dot(h, wg_ref[...])
    u = _dot(h, wu_ref[...])
    gbuf[0:8, :] = carry[j]
    gbuf[8:8 + tm, :] = g
    cw = cw_ref[...]
    gc = cw[2:3] * g + cw[1:2] * gbuf[7:7 + tm, :] + cw[0:1] * gbuf[6:6 + tm, :] + cb_ref[...]
    carry[j] = gbuf[tm:tm + 8, :]
    st_ref[0, j] = gbuf[tm + 6:tm + 8, :]
    act = (_silu(gc) * u).astype(BF16)
    acc[...] += _dot(act, wd_ref[...])

    @pl.when(j == nj - 1)
    def _():
        if final:
            y_ref[0] = _rms(acc[...], gf_ref[...])
        else:
            y_ref[0] = acc[...]


def _ffn_prompt(x, o_a, o_b, o_c, w_out, norm2_g, w_ffn_in, conv_w, conv_b, w_ffn_out, normf_g, final, tm=1024, tf=512):
    n, t, d = x.shape
    tm = min(tm, t)
    assert t % tm == 0
    nj = D_FF // tf
    kern = functools.partial(_ffn_prompt_body, tm=tm, final=final)
    rows = lambda w: pl.BlockSpec((1, tm, w), lambda b, i, j: (b, i, 0))
    y, st = pl.pallas_call(
        kern,
        grid=(n, t // tm, nj),
        in_specs=[rows(d), rows(256), rows(256), rows(512),
                  _const_spec((d, d)), _const_spec((1, d)),
                  pl.BlockSpec((d, tf), lambda b, i, j: (0, j)),
                  pl.BlockSpec((d, tf), lambda b, i, j: (0, j + nj)),
                  pl.BlockSpec((FFN_CONV, tf), lambda b, i, j: (0, j)),
                  pl.BlockSpec((1, tf), lambda b, i, j: (0, j)),
                  pl.BlockSpec((tf, d), lambda b, i, j: (j, 0)),
                  _const_spec((1, d))],
        out_specs=[rows(d), pl.BlockSpec((1, nj, FFN_CONV - 1, tf), lambda b, i, j: (b, 0, 0, 0))],
        out_shape=[jax.ShapeDtypeStruct((n, t, d), F32), jax.ShapeDtypeStruct((n, nj, FFN_CONV - 1, tf), F32)],
        scratch_shapes=[pltpu.VMEM((tm, d), F32), pltpu.VMEM((tm, d), BF16), pltpu.VMEM((tm + 8, tf), F32),
                        pltpu.VMEM((nj, 8, tf), F32)],
        compiler_params=_cp(3),
        name="ffn_prompt",
    )(x, o_a, o_b, o_c, w_out, norm2_g.reshape(1, d), w_ffn_in, w_ffn_in, conv_w, conv_b.reshape(1, D_FF),
      w_ffn_out, normf_g.reshape(1, d))
    return y, st.transpose(0, 2, 1, 3).reshape(n, FFN_CONV - 1, D_FF)


def _ffn_decode_body(x_ref, oa_ref, ob_ref, oc_ref, wo_ref, g2_ref, wg_ref, wu_ref, cw_ref, cb_ref, wd_ref, gf_ref,
                     prev_ref, y_ref, st_ref, acc, h2, *, final):
    j = pl.program_id(0)
    nj = pl.num_programs(0)

    @pl.when(j == 0)
    def _():
        x1 = (x_ref[...] + _dot(oa_ref[...].astype(BF16), wo_ref[0:256, :]) + _dot(ob_ref[...].astype(BF16), wo_ref[256:512, :])
              + _dot(oc_ref[...].astype(BF16), wo_ref[512:1024, :]))
        acc[...] = x1
        h2[...] = _rms(x1, g2_ref[...]).astype(BF16)

    h = h2[...]
    g = _dot(h, wg_ref[...])
    u = _dot(h, wu_ref[...])
    cw = cw_ref[...]
    gc = cw[2:3] * g + cw[1:2] * prev_ref[1] + cw[0:1] * prev_ref[0] + cb_ref[...]
    st_ref[0] = prev_ref[1]
    st_ref[1] = g
    act = (_silu(gc) * u).astype(BF16)
    acc[...] += _dot(act, wd_ref[...])

    @pl.when(j == nj - 1)
    def _():
        if final:
            y_ref[...] = _rms(acc[...], gf_ref[...])
        else:
            y_ref[...] = acc[...]


def _ffn_decode(x, o_a, o_b, o_c, prev, w_out, norm2_g, w_ffn_in, conv_w, conv_b, w_ffn_out, normf_g, final, tf=512):
    ns, d = x.shape
    nj = D_FF // tf
    kern = functools.partial(_ffn_decode_body, final=final)
    return pl.pallas_call(
        kern,
        grid=(nj,),
        in_specs=[_const_spec((ns, d)), _const_spec((ns, 256)), _const_spec((ns, 256)), _const_spec((ns, 512)),
                  _const_spec((d, d)), _const_spec((1, d)),
                  pl.BlockSpec((d, tf), lambda j: (0, j)),
                  pl.BlockSpec((d, tf), lambda j: (0, j + nj)),
                  pl.BlockSpec((FFN_CONV, tf), lambda j: (0, j)),
                  pl.BlockSpec((1, tf), lambda j: (0, j)),
                  pl.BlockSpec((tf, d), lambda j: (j, 0)),
                  _const_spec((1, d)),
                  pl.BlockSpec((FFN_CONV - 1, ns, tf), lambda j: (0, 0, j))],
        out_specs=[_const_spec((ns, d)), pl.BlockSpec((FFN_CONV - 1, ns, tf), lambda j: (0, 0, j))],
        out_shape=[jax.ShapeDtypeStruct((ns, d), F32), jax.ShapeDtypeStruct((FFN_CONV - 1, ns, D_FF), F32)],
        scratch_shapes=[pltpu.VMEM((ns, d), F32), pltpu.VMEM((ns, d), BF16)],
        compiler_params=_cp(1),
        name="ffn_decode",
    )(x, o_a, o_b, o_c, w_out, norm2_g.reshape(1, d), w_ffn_in, w_ffn_in, conv_w, conv_b.reshape(1, D_FF),
      w_ffn_out, normf_g.reshape(1, d), prev)


ROWS = 16
PAGES_PER_STEP = 8


def _decode_consts():
    j = np.arange(ROWS)[:, None]
    lane = np.arange(SOFT_W)[None, :]
    is_diff = j < 8
    is_moba = (j >= 8) & (j < 12)
    q_lo = np.where(is_diff, (j // 2) * HEAD_DIM + (j % 2) * DIFF_HALF, 256 + (j - 8) * HEAD_DIM)
    q_w = np.where(is_diff, DIFF_HALF, HEAD_DIM)
    qmask = (lane >= q_lo) & (lane < q_lo + q_w) & (is_diff | is_moba)
    qscale = np.where(is_diff, DIFF_HALF ** -0.5, HEAD_DIM ** -0.5) * LOG2E
    v_lo = np.where(is_diff, (j // 2) * HEAD_DIM, 256 + (j - 8) * HEAD_DIM)
    vmask = (lane >= v_lo) & (lane < v_lo + HEAD_DIM) & (is_diff | is_moba)
    head = np.where(is_diff, j // 2, j - 8)
    slope = np.where(is_diff, 2.0 ** -(2 * head + 1), np.where(is_moba, 2.0 ** -(2 * head + 2), 0.0)) * LOG2E
    coef_a = np.where((is_diff & (j % 2 == 0)) | is_moba, 1.0, 0.0)
    coef_b = np.where(is_diff & (j % 2 == 1), -1.0, 0.0)
    bc = lambda a: jnp.asarray(np.broadcast_to(a, (ROWS, LANES)).astype(np.float32))
    i = np.arange(SOFT_W)
    bd512 = (i[:, None] // HEAD_DIM == i[None, :] // HEAD_DIM).astype(np.float32)
    return (jnp.asarray((qmask * qscale).astype(np.float32)), jnp.asarray((qmask & is_moba).astype(np.float32)),
            jnp.asarray(vmask.astype(np.float32)), bc(slope), bc(coef_a), bc(coef_b), jnp.asarray(bd512))


def _decode_attn_body(pt_ref, lam_ref, q_ref, ks_ref, vs_ref, *rest, nblk, npg, page, past_len, out_scale):
    del pt_ref
    k_refs = rest[0:npg]
    v_refs = rest[npg:2 * npg]
    qmask_ref, gmask_ref, vmask_ref, slope_ref, ca_ref, cb_ref, bd_ref, g_ref, o_ref = rest[2 * npg:2 * npg + 9]
    m_scr, l_scr, acc_scr, mblk, lblk, gblk, accblk = rest[2 * npg + 9:]
    b = pl.program_id(1)
    nstep = pl.num_programs(1)
    tk = 2 * page

    @pl.when(b == 0)
    def _():
        m_scr[...] = jnp.full(m_scr.shape, -jnp.inf, F32)
        l_scr[...] = jnp.zeros(l_scr.shape, F32)
        acc_scr[...] = jnp.zeros(acc_scr.shape, F32)
        mblk[...] = jnp.full(mblk.shape, -jnp.inf, F32)
        lblk[...] = jnp.zeros(lblk.shape, F32)
        gblk[...] = jnp.full(gblk.shape, -jnp.inf, F32)

    qs = q_ref[...] * qmask_ref[...]
    qsb = qs.astype(BF16)
    qgb = (q_ref[...] * gmask_ref[...]).astype(BF16)
    slope = slope_ref[:, 0:1]
    lane = lax.broadcasted_iota(jnp.int32, (ROWS, LANES), 1)
    tok = lax.broadcasted_iota(jnp.int32, (1, tk), 1)

    nu = npg // 2
    raw = [jnp.concatenate([_dot(qsb, k_refs[2 * u][...].astype(BF16)), _dot(qsb, k_refs[2 * u + 1][...].astype(BF16))],
                           axis=1) for u in range(nu)]
    mbs, lbs, pes = [], [], []
    for u in range(nu):
        dist = (past_len - ((b * nu + u) * tk + tok)).astype(F32)
        s = raw[u] - slope * dist
        mb = jnp.max(s, axis=-1, keepdims=True)
        pexp = jnp.exp2(s - mb)
        mbs.append(mb)
        lbs.append(jnp.sum(pexp, axis=-1, keepdims=True))
        pes.append(pexp.astype(BF16))
    accbs = [_dot_nt(pes[u][:, 0:page], v_refs[2 * u][...].astype(BF16))
             + _dot_nt(pes[u][:, page:tk], v_refs[2 * u + 1][...].astype(BF16)) for u in range(nu)]
    kmeans = [jnp.sum(k_refs[2 * u][...] + k_refs[2 * u + 1][...], axis=-1, keepdims=True) * (1.0 / tk)
              for u in range(nu)]
    gates = [_dot(qgb, kmeans[u].astype(BF16)) for u in range(nu)]

    for u in range(nu):
        bb = b * nu + u
        mb, lb, accb = mbs[u], lbs[u], accbs[u]
        m_old = m_scr[:, 0:1]
        m_new = jnp.maximum(m_old, mb)
        a_old = jnp.exp2(m_old - m_new)
        a_blk = jnp.exp2(mb - m_new)
        l_scr[...] = jnp.broadcast_to(a_old * l_scr[:, 0:1] + a_blk * lb, l_scr.shape)
        acc_scr[...] = a_old * acc_scr[...] + a_blk * accb
        m_scr[...] = jnp.broadcast_to(m_new, m_scr.shape)
        hit = lane == bb
        mblk[...] = jnp.where(hit, mb, mblk[...])
        lblk[...] = jnp.where(hit, lb, lblk[...])
        gblk[...] = jnp.where(hit, gates[u], gblk[...])
        accblk[bb] = accb

    @pl.when(b == nstep - 1)
    def _():
        s_self = jnp.sum(qs * ks_ref[...], axis=-1, keepdims=True)
        vs = vs_ref[...]
        m_o = m_scr[:, 0:1]
        m_d = jnp.maximum(m_o, s_self)
        a_o = jnp.exp2(m_o - m_d)
        p_d = jnp.exp2(s_self - m_d)
        o_d = (a_o * acc_scr[...] + p_d * vs) / (a_o * l_scr[:, 0:1] + p_d)
        sel = _topk_select(gblk[...], lane, nblk, 1, LANES)
        mm = mblk[...]
        m_f = jnp.maximum(jnp.max(jnp.where(sel, mm, -jnp.inf), axis=-1, keepdims=True), s_self)
        wgt = jnp.where(sel, jnp.exp2(mm - m_f), 0.0)
        p_m = jnp.exp2(s_self - m_f)
        l_m = jnp.sum(wgt * lblk[...], axis=-1, keepdims=True) + p_m
        acc_m = p_m * vs
        for blk in range(nblk):
            acc_m = acc_m + wgt[:, blk:blk + 1] * accblk[blk]
        o_m = acc_m / l_m
        rowi = lax.broadcasted_iota(jnp.int32, (ROWS, SOFT_W), 0)
        o_all = jnp.where(rowi < 8, o_d, o_m)
        coef = ca_ref[:, 0:1] + lam_ref[0] * cb_ref[:, 0:1]
        o_row = jnp.sum(coef * vmask_ref[...] * o_all, axis=0, keepdims=True)
        ms = _dot(jnp.broadcast_to(o_row * o_row, (8, SOFT_W)), bd_ref[...], HI)[0:1] * (1.0 / HEAD_DIM)
        o_norm = o_row * lax.rsqrt(ms + EPS) * g_ref[...] * out_scale
        lane5 = lax.broadcasted_iota(jnp.int32, (1, SOFT_W), 1)
        o_ref[...] = jnp.where(lane5 < 256, o_norm, o_row)


def _decode_attn(layer, q_soft, k_self, v_self, cache_k, cache_v, page_table, lam, subln_g, lam_init):
    ns = q_soft.shape[0]
    n_pages = page_table.shape[1]
    page = cache_k.shape[3]
    npg = PAGES_PER_STEP
    assert 2 * page == MOBA_BLOCK and n_pages % npg == 0 and cache_k.shape[2] == SOFT_W
    nblk = n_pages // 2
    assert nblk <= LANES
    kern = functools.partial(_decode_attn_body, nblk=nblk, npg=npg, page=page, past_len=n_pages * page,
                             out_scale=1.0 - lam_init)
    row = pl.BlockSpec((None, 1, SOFT_W), lambda s, b, pt: (s, 0, 0))
    pg = lambda off: pl.BlockSpec((None, None, SOFT_W, page), lambda s, b, pt: (layer, pt[s, npg * b + off], 0, 0))
    cst = lambda shp: pl.BlockSpec(shp, lambda s, b, pt: (0,) * len(shp))
    g4 = jnp.concatenate([jnp.tile(subln_g.reshape(1, HEAD_DIM), (1, 4)), jnp.ones((1, 256), F32)], axis=1)
    grid_spec = pltpu.PrefetchScalarGridSpec(
        num_scalar_prefetch=1,
        grid=(ns, n_pages // npg),
        in_specs=([pl.BlockSpec(memory_space=pltpu.SMEM), row, row, row]
                  + [pg(u) for u in range(npg)] + [pg(u) for u in range(npg)]
                  + [cst((ROWS, SOFT_W)), cst((ROWS, SOFT_W)), cst((ROWS, SOFT_W)), cst((ROWS, LANES)), cst((ROWS, LANES)),
                     cst((ROWS, LANES)), cst((SOFT_W, SOFT_W)), cst((1, SOFT_W))]),
        out_specs=row,
        scratch_shapes=[pltpu.VMEM((ROWS, LANES), F32), pltpu.VMEM((ROWS, LANES), F32), pltpu.VMEM((ROWS, SOFT_W), F32),
                        pltpu.VMEM((ROWS, LANES), F32), pltpu.VMEM((ROWS, LANES), F32), pltpu.VMEM((ROWS, LANES), F32),
                        pltpu.VMEM((nblk, ROWS, SOFT_W), F32)],
    )
    out = pl.pallas_call(
        kern,
        grid_spec=grid_spec,
        out_shape=jax.ShapeDtypeStruct((ns, 1, SOFT_W), F32),
        compiler_params=_cp(2),
        name="decode_attn",
    )(page_table, lam.reshape(1), q_soft[:, None, :], k_self[:, None, :], v_self[:, None, :],
      *([cache_k] * npg), *([cache_v] * npg), *_decode_consts(), g4)
    return out[:, 0, :]


def _pages_transposed(cache):
    d, n_pool, page = cache.shape[0:3]
    return jnp.transpose(cache, (0, 1, 3, 4, 2)).reshape(d, n_pool, SOFT_W, page)


def _lam(l, lam_q1, lam_k1, lam_q2, lam_k2):
    lam_init = 0.8 - 0.6 * math.exp(-0.3 * l)
    lam = (jnp.exp(jnp.sum(lam_q1[l] * lam_k1[l])) - jnp.exp(jnp.sum(lam_q2[l] * lam_k2[l])) + lam_init)
    return lam.astype(F32), lam_init


def _prompt_layer(x, w, lam, lam_init, final):
    n, t, d = x.shape
    outs = _proj_in(x.reshape(n * t, d), w["norm1"], w["w_in"], seq_len=t)
    qs, kr, vr, qkv, zg, b_x, a_x, ba = [a.reshape(n, t, a.shape[-1]) for a in outs[:len(_PROJ_WIDTHS)]]
    rows_t = lambda a: jnp.transpose(a.reshape(n, N_SOFT_HEADS, HEAD_DIM, t), (0, 3, 1, 2))
    o_a = _diff_attn_prompt(qs, kr, vr, lam, w["subln"], lam_init)
    o_b = _moba_prompt(qs, kr, vr)
    o_c, s_bd = _gdn_prompt(qkv, zg, b_x, a_x, ba, w["dn_conv_w"], w["dn_a_log"], w["dn_dt_bias"], w["dn_norm"])
    y, ffn_new = _ffn_prompt(x, o_a, o_b, o_c, w["w_out"], w["norm2"], w["ffn_in"], w["ffn_conv_w"], w["ffn_conv_b"],
                             w["ffn_out"], w["normf"], final)
    rows = (rows_t(outs[-2]), rows_t(outs[-1]), _state_from_bd(s_bd),
            qkv[:, t - (DN_CONV - 1):, :], ffn_new)
    return y, rows


def _sample_layer(l, x, w, lam, lam_init, final, cache_k, cache_v, page_table, state_dn, conv_qkv, conv_ffn):
    ns, d = x.shape
    qs, kr, vr, qkv, zg, b_x, a_x, _ = _proj_in(x, w["norm1"], w["w_in"])
    o_ab = _decode_attn(l, qs, kr, vr, cache_k, cache_v, page_table, lam, w["subln"], lam_init)
    o_c, s_bd, conv_new = _gdn_decode(qkv, zg, b_x, a_x, conv_qkv, _state_to_bd(state_dn), w["dn_conv_w"], w["dn_a_log"],
                                      w["dn_dt_bias"], w["dn_norm"])
    y, ffn_new = _ffn_decode(x, o_ab[:, 0:256], o_ab[:, 256:512], o_c[:, 0, :], jnp.swapaxes(conv_ffn, 0, 1),
                             w["w_out"], w["norm2"], w["ffn_in"], w["ffn_conv_w"], w["ffn_conv_b"], w["ffn_out"],
                             w["normf"], final)
    rows = (kr.reshape(ns, 1, 8, HEAD_DIM), vr.reshape(ns, 1, 8, HEAD_DIM), _state_from_bd(s_bd), conv_new,
            jnp.swapaxes(ffn_new, 0, 1))
    return y, rows


def kernel(x_prompt, x_sample, cache_k, cache_v, page_table, state_dn, state_conv_qkv, state_conv_ffn, norm1_g, norm2_g, normf_g, w_in, w_out, lam_q1, lam_k1, lam_q2, lam_k2, subln_g, dn_conv_w, dn_a_log, dn_dt_bias, dn_norm_g, ffn_w_in, ffn_conv_w, ffn_conv_b, ffn_w_out):
    depth = w_in.shape[0]
    cache_k = _pages_transposed(cache_k)
    cache_v = _pages_transposed(cache_v)
    xp = x_prompt
    xs = x_sample.reshape(x_sample.shape[0], x_sample.shape[2])
    rows_p, rows_s = [], []
    for l in range(depth):
        w = {"norm1": norm1_g[l], "w_in": _permute_w_in(w_in[l]), "subln": subln_g[l], "dn_conv_w": dn_conv_w[l],
             "dn_a_log": dn_a_log[l], "dn_dt_bias": dn_dt_bias[l], "dn_norm": dn_norm_g[l],
             "w_out": w_out[l].astype(BF16), "norm2": norm2_g[l], "ffn_in": ffn_w_in[l].astype(BF16),
             "ffn_conv_w": ffn_conv_w[l], "ffn_conv_b": ffn_conv_b[l], "ffn_out": ffn_w_out[l].astype(BF16),
             "normf": normf_g}
        lam, lam_init = _lam(l, lam_q1, lam_k1, lam_q2, lam_k2)
        final = l == depth - 1
        xp, rp = _prompt_layer(xp, w, lam, lam_init, final)
        xs, rs = _sample_layer(l, xs, w, lam, lam_init, final, cache_k, cache_v, page_table, state_dn[l],
                               state_conv_qkv[l], state_conv_ffn[l])
        rows_p.append(rp)
        rows_s.append(rs)
    stack = lambda rows, i: jnp.stack([r[i] for r in rows], axis=0)
    y_sample = xs.reshape(x_sample.shape)
    return (xp, y_sample,
            stack(rows_p, 0), stack(rows_p, 1), stack(rows_p, 2), stack(rows_p, 3), stack(rows_p, 4),
            stack(rows_s, 0), stack(rows_s, 1), stack(rows_s, 2), stack(rows_s, 3), stack(rows_s, 4))
```

```python
import functools
import math

import numpy as np
import jax
import jax.numpy as jnp
from jax import lax
from jax.experimental import pallas as pl
from jax.experimental.pallas import tpu as pltpu

F32 = jnp.float32
BF16 = jnp.bfloat16
HI = lax.Precision.HIGHEST

LANES = 128
HEAD_DIM = 64
DIFF_HALF = 32
N_DIFF_HEADS = 4
N_MOBA_HEADS = 4
N_SOFT_HEADS = 8
N_DN_HEADS = 8
N_PAIRS = 4
SOFT_W = 512
DN_W = 512
D_FF = 3584
MOBA_BLOCK = 256
MOBA_TOPK = 3
DN_CONV = 4
FFN_CONV = 3
EPS = 1e-6
LOG2E = 1.4426950408889634
NEG_BIG = -1e30
Z_W = 3584 + 2 * DN_W + LANES
VMEM_LIMIT = 56 * 1024 * 1024


def _cp(n_axes, vmem=VMEM_LIMIT):
    return pltpu.CompilerParams(dimension_semantics=("arbitrary",) * n_axes, vmem_limit_bytes=vmem)


def _sigmoid(x):
    return 1.0 / (1.0 + jnp.exp(-x))


def _silu(x):
    return x * _sigmoid(x)


def _softplus(x):
    return jnp.maximum(x, 0.0) + jnp.log(1.0 + jnp.exp(-jnp.abs(x)))


def _dot(a, b, prec=None):
    return jnp.dot(a, b, preferred_element_type=F32, precision=prec)


def _dot_nt(a, b, prec=None):
    return lax.dot_general(a, b, (((1,), (1,)), ((), ())), preferred_element_type=F32, precision=prec)


def _dot_tn(a, b, prec=None):
    return lax.dot_general(a, b, (((0,), (0,)), ((), ())), preferred_element_type=F32, precision=prec)


def _split2(x):
    hi = x.astype(BF16)
    return hi, (x - hi.astype(F32)).astype(BF16)


def _mm3(a, b, dot=_dot):
    ah, al = _split2(a)
    bh, bl = _split2(b)
    return dot(ah, bh) + dot(ah, bl) + dot(al, bh)


def _mm2(a, b01, dot=_dot):
    ah, al = _split2(a)
    return dot(ah, b01) + dot(al, b01)


def _const_spec(shape):
    nd = len(shape)
    return pl.BlockSpec(shape, lambda *_: (0,) * nd)


def _bd_ones(dtype=F32):
    i = np.arange(LANES)
    return jnp.asarray((i[:, None] // HEAD_DIM == i[None, :] // HEAD_DIM).astype(np.float32)).astype(dtype)


_PROJ_WIDTHS = (512, 512, 512, 1536, 512, 512, 512, LANES)


def _proj_in_body(x_ref, g_ref, w_ref, *out_refs, transposed_kv):
    x = x_ref[...]
    ms = jnp.mean(x * x, axis=-1, keepdims=True)
    h = (x * lax.rsqrt(ms + EPS) * g_ref[...]).astype(BF16)
    lo = 0
    for idx, w in enumerate(_PROJ_WIDTHS):
        z = _dot(h, w_ref[:, lo:lo + w])
        out_refs[idx][...] = z
        if transposed_kv and idx in (1, 2):
            out_refs[len(_PROJ_WIDTHS) + idx - 1][0] = z.T
        lo += w


def _proj_in(x2d, g, w_perm, seq_len=None):
    m, d = x2d.shape
    tm = min(m, 256)
    out_specs = [pl.BlockSpec((tm, w), lambda i: (i, 0)) for w in _PROJ_WIDTHS]
    out_shape = [jax.ShapeDtypeStruct((m, w), F32) for w in _PROJ_WIDTHS]
    if seq_len is not None:
        tps = seq_len // tm
        out_specs += [pl.BlockSpec((1, SOFT_W, tm), lambda i: (i // tps, 0, i % tps))] * 2
        out_shape += [jax.ShapeDtypeStruct((m // seq_len, SOFT_W, seq_len), F32)] * 2
    return pl.pallas_call(
        functools.partial(_proj_in_body, transposed_kv=seq_len is not None),
        grid=(m // tm,),
        in_specs=[pl.BlockSpec((tm, d), lambda i: (i, 0)), _const_spec((1, d)), _const_spec((d, Z_W))],
        out_specs=out_specs,
        out_shape=out_shape,
        compiler_params=_cp(1),
        name="proj_in",
    )(x2d, g.reshape(1, d), w_perm)


def _permute_w_in(w):
    cols = [w[:, 0:256], w[:, 768:1024], w[:, 256:512], w[:, 1024:1280], w[:, 512:768], w[:, 1280:1536],
            w[:, 1536:3584], jnp.repeat(w[:, 3584:3592], HEAD_DIM, axis=1), jnp.repeat(w[:, 3592:3600], HEAD_DIM, axis=1),
            w[:, 3584:3600], jnp.zeros((w.shape[0], LANES - 2 * N_DN_HEADS), w.dtype)]
    return jnp.concatenate(cols, axis=1).astype(BF16)


def _prep_kv(k_ref, v_ref, kb_scr, vt_scr, t, tq):
    kb_scr[...] = k_ref[0].astype(BF16)
    row_head = lax.broadcasted_iota(jnp.int32, (LANES, tq), 0) // HEAD_DIM
    for c in range(t // tq):
        vt = v_ref[0, c * tq:(c + 1) * tq, :].T
        for hl in range(2):
            vt_scr[hl, c] = jnp.where(row_head == hl, vt, 1.0).astype(BF16)


def _denominator(acc_ref, j, hl):
    r = (1 - hl) * HEAD_DIM
    return acc_ref[j, r:r + 1, :]


def _online_stats_t(j, sp, shift, m_scr):
    m_old = m_scr[j]
    m_new = jnp.maximum(m_old, jnp.max(sp, axis=0, keepdims=True) + shift)
    p = jnp.exp2(sp - (m_new - shift))
    m_scr[j] = m_new
    return p.astype(BF16), jnp.exp2(m_old - m_new)


def _diff_attn_body(lam_ref, q_ref, k_ref, v_ref, g_ref, o_ref,
                    kb_scr, vt_scr, qm_scr, sa_scr, sb_scr, m_scr, acc_scr, *, t, tq, out_scale):
    p = pl.program_id(1)
    qi = pl.program_id(2)

    lane = lax.broadcasted_iota(jnp.int32, (tq, LANES), 1)

    @pl.when(qi == 0)
    def _():
        row_head = lax.broadcasted_iota(jnp.int32, (LANES, tq), 0) // HEAD_DIM
        for c in range(t // tq):
            kt = k_ref[0, c * tq:(c + 1) * tq, :]
            vt = v_ref[0, c * tq:(c + 1) * tq, :].T
            for hl in range(2):
                halves = [jnp.where((lane >= hl * HEAD_DIM + c2 * DIFF_HALF) & (lane < hl * HEAD_DIM + (c2 + 1) * DIFF_HALF),
                                    kt, 0.0).astype(BF16) for c2 in range(2)]
                kb_scr[hl, c] = jnp.concatenate(halves, axis=0)
                vt_scr[hl, c] = jnp.where(row_head == hl, vt, 1.0).astype(BF16)

    q = q_ref[0] * (DIFF_HALF ** -0.5 * LOG2E)
    for hl in range(2):
        qm_scr[hl] = jnp.where((lane >= hl * HEAD_DIM) & (lane < (hl + 1) * HEAD_DIM), q, 0.0).astype(BF16)
    m_scr[...] = jnp.full(m_scr.shape, -jnp.inf, F32)
    acc_scr[...] = jnp.zeros(acc_scr.shape, F32)
    slopes = [jnp.where(p == 0, LOG2E * 2.0 ** -(2 * hl + 1), LOG2E * 2.0 ** -(2 * (hl + 2) + 1)).astype(F32)
              for hl in range(2)]
    krow = lax.broadcasted_iota(jnp.int32, (tq, tq), 0)
    qcol = lax.broadcasted_iota(jnp.int32, (tq, tq), 1)
    krow_f = krow.astype(F32)
    ramps = [slopes[hl] * krow_f for hl in range(2)]

    def scores(ki, s_ref):
        for hl in range(2):
            s_ref[hl] = _dot_nt(kb_scr[hl, ki], qm_scr[hl])

    def softmax_pv(ki, s_ref, masked):
        off = ((ki - qi) * tq).astype(F32)
        pa = []
        for j in range(4):
            sp = s_ref[j // 2, (j % 2) * tq:(j % 2 + 1) * tq, :] + ramps[j // 2]
            if masked:
                sp = jnp.where(krow <= qcol, sp, -jnp.inf)
            pa.append(_online_stats_t(j, sp, slopes[j // 2] * off, m_scr))
        for j in range(4):
            acc_scr[j] = pa[j][1] * acc_scr[j] + _dot(vt_scr[j // 2, ki], pa[j][0])

    scores(0, sa_scr)

    def body(i, carry):
        ki = 2 * i
        scores(ki + 1, sb_scr)
        softmax_pv(ki, sa_scr, False)
        scores(ki + 2, sa_scr)
        softmax_pv(ki + 1, sb_scr, False)
        return carry

    lax.fori_loop(0, qi // 2, body, 0)

    @pl.when(qi % 2 == 1)
    def _():
        scores(qi, sb_scr)
        softmax_pv(qi - 1, sa_scr, False)
        softmax_pv(qi, sb_scr, True)

    @pl.when(qi % 2 == 0)
    def _():
        softmax_pv(qi, sa_scr, True)

    lam = lam_ref[0]
    outs = []
    for hl in range(2):
        r = slice(hl * HEAD_DIM, (hl + 1) * HEAD_DIM)
        o = (acc_scr[2 * hl, r, :] * (1.0 / _denominator(acc_scr, 2 * hl, hl))
             - lam * (acc_scr[2 * hl + 1, r, :] * (1.0 / _denominator(acc_scr, 2 * hl + 1, hl))))
        ms = jnp.sum(o * o, axis=0, keepdims=True) * (1.0 / HEAD_DIM)
        outs.append(o * lax.rsqrt(ms + EPS))
    ot = jnp.concatenate(outs, axis=0) * g_ref[...] * out_scale
    o_ref[0] = ot.T


def _diff_attn_prompt(q_soft, k_rows, v_rows, lam, subln_g, lam_init, tq=256):
    n, t, _ = q_soft.shape
    tq = min(tq, t)
    g2 = jnp.tile(subln_g.reshape(HEAD_DIM, 1), (2, 1))
    kern = functools.partial(_diff_attn_body, t=t, tq=tq, out_scale=1.0 - lam_init)
    return pl.pallas_call(
        kern,
        grid=(n, 2, t // tq),
        in_specs=[pl.BlockSpec(memory_space=pltpu.SMEM),
                  pl.BlockSpec((1, tq, LANES), lambda b, p, i: (b, i, p)),
                  pl.BlockSpec((1, t, LANES), lambda b, p, i: (b, 0, p)),
                  pl.BlockSpec((1, t, LANES), lambda b, p, i: (b, 0, p)),
                  _const_spec((LANES, 1))],
        out_specs=pl.BlockSpec((1, tq, LANES), lambda b, p, i: (b, i, p)),
        out_shape=jax.ShapeDtypeStruct((n, t, 2 * LANES), F32),
        scratch_shapes=[pltpu.VMEM((2, t // tq, 2 * tq, LANES), BF16), pltpu.VMEM((2, t // tq, LANES, tq), BF16),
                        pltpu.VMEM((2, tq, LANES), BF16), pltpu.VMEM((2, 2 * tq, tq), F32), pltpu.VMEM((2, 2 * tq, tq), F32),
                        pltpu.VMEM((4, 1, tq), F32),
                        pltpu.VMEM((4, LANES, tq), F32)],
        compiler_params=_cp(3),
        name="diff_attn_prompt",
    )(lam.reshape(1), q_soft, k_rows, v_rows, g2)


def _topk_select(gate, idx, n_valid, axis, size):
    gm = jnp.where(idx < n_valid, gate, -jnp.inf)
    sel = jnp.zeros(gate.shape, jnp.bool_)
    for _ in range(MOBA_TOPK):
        mx = jnp.max(gm, axis=axis, keepdims=True)
        is_max = (gm == mx) & (mx > -jnp.inf)
        first = jnp.min(jnp.where(is_max, idx, size), axis=axis, keepdims=True)
        pick = idx == first
        sel = sel | pick
        gm = jnp.where(pick, -jnp.inf, gm)
    return sel


def _moba_body(q_ref, k_ref, v_ref, o_ref, kb_scr, vt_scr, kmean_scr, qs_scr, sel_scr, sd_scr, sa_scr, sb_scr,
               m_scr, acc_scr, *, t, tq, nb):
    nbp = kmean_scr.shape[0]
    p = pl.program_id(1)
    qi = pl.program_id(2)

    @pl.when(qi == 0)
    def _():
        _prep_kv(k_ref, v_ref, kb_scr, vt_scr, t, tq)
        kmean_scr[...] = jnp.zeros(kmean_scr.shape, F32)
        for b in range(nb):
            kmean_scr[b:b + 1, :] = jnp.sum(k_ref[0, b * MOBA_BLOCK:(b + 1) * MOBA_BLOCK, :], axis=0,
                                            keepdims=True) * (1.0 / MOBA_BLOCK)

    lane = lax.broadcasted_iota(jnp.int32, (tq, LANES), 1)
    blk = lax.broadcasted_iota(jnp.int32, (nbp, tq), 0)
    q = q_ref[0]
    kmean = kmean_scr[...]
    for hl in range(2):
        qh = jnp.where((lane >= hl * HEAD_DIM) & (lane < (hl + 1) * HEAD_DIM), q, 0.0)
        gate_t = _dot_nt(kmean.astype(BF16), qh.astype(BF16))
        sel = _topk_select(gate_t, blk, qi, 0, nbp)
        sel_scr[hl] = jnp.where(sel, 0.0, NEG_BIG)
        qs_scr[hl] = (qh * (HEAD_DIM ** -0.5 * LOG2E)).astype(BF16)
    m_scr[...] = jnp.full(m_scr.shape, -jnp.inf, F32)
    acc_scr[...] = jnp.zeros(acc_scr.shape, F32)
    slopes = [jnp.where(p == 0, LOG2E * 2.0 ** -(2 * hl + 2), LOG2E * 2.0 ** -(2 * (hl + 2) + 2)).astype(F32)
              for hl in range(2)]
    krow = lax.broadcasted_iota(jnp.int32, (tq, tq), 0)
    qcol = lax.broadcasted_iota(jnp.int32, (tq, tq), 1)
    krow_f = krow.astype(F32)
    ramps = [slopes[hl] * krow_f for hl in range(2)]

    def scores(ki, s_ref):
        kt = kb_scr[pl.ds(pl.multiple_of(ki * tq, tq), tq), :]
        for hl in range(2):
            s_ref[hl] = _dot_nt(kt, qs_scr[hl])

    def softmax_pv(ki, s_ref, diagonal):
        off = ((ki - qi) * tq).astype(F32)
        pa = []
        for hl in range(2):
            sp = s_ref[hl] + ramps[hl]
            shift = slopes[hl] * off
            if diagonal:
                sp = jnp.where(krow <= qcol, sp, -jnp.inf)
            else:
                shift = shift + sel_scr[hl, pl.ds(ki, 1), :]
            pa.append(_online_stats_t(hl, sp, shift, m_scr))
        for hl in range(2):
            acc_scr[hl] = pa[hl][1] * acc_scr[hl] + _dot(vt_scr[hl, ki], pa[hl][0])

    scores(qi, sd_scr)
    scores(0, sa_scr)
    softmax_pv(qi, sd_scr, True)

    def body(i, carry):
        ki = 2 * i
        scores(ki + 1, sb_scr)
        softmax_pv(ki, sa_scr, False)
        scores(ki + 2, sa_scr)
        softmax_pv(ki + 1, sb_scr, False)
        return carry

    lax.fori_loop(0, qi // 2, body, 0)

    @pl.when(qi % 2 == 1)
    def _():
        softmax_pv(qi - 1, sa_scr, False)

    outs = []
    for hl in range(2):
        outs.append(acc_scr[hl, hl * HEAD_DIM:(hl + 1) * HEAD_DIM, :] * (1.0 / _denominator(acc_scr, hl, hl)))
    o_ref[0] = jnp.concatenate(outs, axis=0).T


def _moba_prompt(q_soft, k_rows, v_rows):
    n, t, _ = q_soft.shape
    tq = MOBA_BLOCK
    assert t % tq == 0 and t // tq <= LANES
    nb = t // tq
    nbp = -(-nb // 8) * 8
    kern = functools.partial(_moba_body, t=t, tq=tq, nb=nb)
    return pl.pallas_call(
        kern,
        grid=(n, 2, nb),
        in_specs=[pl.BlockSpec((1, tq, LANES), lambda b, p, i: (b, i, 2 + p)),
                  pl.BlockSpec((1, t, LANES), lambda b, p, i: (b, 0, 2 + p)),
                  pl.BlockSpec((1, t, LANES), lambda b, p, i: (b, 0, 2 + p))],
        out_specs=pl.BlockSpec((1, tq, LANES), lambda b, p, i: (b, i, p)),
        out_shape=jax.ShapeDtypeStruct((n, t, 2 * LANES), F32),
        scratch_shapes=[pltpu.VMEM((t, LANES), BF16), pltpu.VMEM((2, nb, LANES, tq), BF16),
                        pltpu.VMEM((nbp, LANES), F32),
                        pltpu.VMEM((2, tq, LANES), BF16), pltpu.VMEM((2, nbp, tq), F32),
                        pltpu.VMEM((2, tq, tq), F32), pltpu.VMEM((2, tq, tq), F32), pltpu.VMEM((2, tq, tq), F32),
                        pltpu.VMEM((2, 1, tq), F32),
                        pltpu.VMEM((2, LANES, tq), F32)],
        compiler_params=_cp(3),
        name="moba_prompt",
    )(q_soft, k_rows, v_rows)


def _bdiag(x, lane_lo):
    return jnp.concatenate([jnp.where(lane_lo, x, 0.0), jnp.where(lane_lo, 0.0, x)], axis=0)


def _gdn_prompt_body(qkv_ref, zg_ref, bx_ref, ax_ref, arow_ref, cw_ref, alogx_ref, dtbx_ref, alogr_ref, dtbr_ref,
                     gn_ref, bd_ref, ltri_ref, ubd_ref,
                     o_ref, sfin_ref, xbuf, s_scr, *, c, nseq):
    ci = pl.program_id(0)
    nc = pl.num_programs(0)

    @pl.when(ci == 0)
    def _():
        xbuf[:, 0:8, :] = jnp.zeros((nseq, 8, 3 * DN_W), F32)
        s_scr[...] = jnp.zeros(s_scr.shape, F32)

    ltri = ltri_ref[...]
    ubd = ubd_ref[...]
    bd = bd_ref[...]
    cw = cw_ref[...]
    row = lax.broadcasted_iota(jnp.int32, (c, LANES), 0)
    lane = lax.broadcasted_iota(jnp.int32, (c, LANES), 1)
    col = lane % HEAD_DIM
    lane_lo = lane < HEAD_DIM
    incl = row >= col
    strict = row > col
    eye2 = jnp.where(row == col, 1.0, 0.0)
    r128 = lax.broadcasted_iota(jnp.int32, (LANES, LANES), 0)
    c128 = lax.broadcasted_iota(jnp.int32, (LANES, LANES), 1)
    same_head = (r128 // HEAD_DIM) == (c128 // HEAD_DIM)
    steps = max(1, int(math.ceil(math.log2(c))) - 1)

    def split3_dot(x, lhs01=None, rhs01=None):
        x0, x1 = _split2(x)
        x2 = (x - x0.astype(F32) - x1.astype(F32)).astype(BF16)
        if lhs01 is not None:
            return _dot(lhs01, x0) + _dot(lhs01, x1) + _dot(lhs01, x2)
        return _dot(x0, rhs01) + _dot(x1, rhs01) + _dot(x2, rhs01)

    cq, beta_x, gc_x, gc_row = [], [], [], []
    for b in range(nseq):
        xbuf[b, 8:8 + c, :] = qkv_ref[b]
        y = (cw[3:4] * xbuf[b, 8:8 + c, :] + cw[2:3] * xbuf[b, 7:7 + c, :]
             + cw[1:2] * xbuf[b, 6:6 + c, :] + cw[0:1] * xbuf[b, 5:5 + c, :])
        xbuf[b, 0:8, :] = xbuf[b, c:c + 8, :]
        cq.append(_silu(y))
        beta_x.append(_sigmoid(bx_ref[b]))
        g_x = -jnp.exp(alogx_ref[...]) * _softplus(ax_ref[b] + dtbx_ref[...])
        g_row = -jnp.exp(alogr_ref[...]) * _softplus(arow_ref[b, 0] + dtbr_ref[...])
        gc_x.append(split3_dot(g_x, lhs01=ltri))
        gc_row.append(split3_dot(g_row, rhs01=ubd))

    chains = [(b, p) for b in range(nseq) for p in range(N_PAIRS)]
    pair = lambda arr, p, off=0: arr[:, off + p * LANES:off + (p + 1) * LANES]
    qp = [pair(cq[b], p) for b, p in chains]
    kp = [pair(cq[b], p, DN_W) for b, p in chains]
    vp = [pair(cq[b], p, 2 * DN_W) for b, p in chains]
    bexp = [pair(beta_x[b], p) for b, p in chains]
    gcx = [pair(gc_x[b], p) for b, p in chains]
    gcr = [gc_row[b][p:p + 1, :] for b, p in chains]
    n_ch = len(chains)
    rng = range(n_ch)

    ssq_q = [_mm2(qp[i] * qp[i], bd) for i in rng]
    ssq_k = [_mm2(kp[i] * kp[i], bd) for i in rng]
    qn = [qp[i] * lax.rsqrt(ssq_q[i] + EPS) * (HEAD_DIM ** -0.5) for i in rng]
    kn = [kp[i] * lax.rsqrt(ssq_k[i] + EPS) for i in rng]
    decay = [jnp.exp(jnp.where(incl, gcx[i] - gcr[i], -jnp.inf)) for i in rng]
    kq = [_mm3(jnp.concatenate([kn[i], qn[i]], axis=0), _bdiag(kn[i], lane_lo), _dot_nt) for i in rng]
    qk = [jnp.where(incl, kq[i][c:2 * c] * decay[i], 0.0) for i in rng]
    bpow = [-jnp.where(strict, bexp[i] * kq[i][0:c] * decay[i], 0.0) for i in rng]
    x = [eye2 + bpow[i] for i in rng]
    def mm3_parts(a, b_parts):
        ah, al = _split2(a)
        bh, bl = b_parts
        return _dot(ah, bh) + _dot(ah, bl) + _dot(al, bh)

    pow_bd = [_split2(_bdiag(bpow[i], lane_lo)) for i in rng]
    for _ in range(steps):
        bpow = [mm3_parts(bpow[i], pow_bd[i]) for i in rng]
        pow_bd = [_split2(_bdiag(bpow[i], lane_lo)) for i in rng]
        x = [x[i] + mm3_parts(x[i], pow_bd[i]) for i in rng]
    eg = [jnp.exp(gcx[i]) for i in rng]
    uw = [_mm3(x[i], jnp.concatenate([_bdiag(vp[i] * bexp[i], lane_lo), _bdiag(kn[i] * bexp[i] * eg[i], lane_lo)], axis=1))
          for i in rng]
    s_old = [s_scr[b, p] for b, p in chains]
    ws = [_mm3(jnp.concatenate([uw[i][:, LANES:2 * LANES], qn[i] * eg[i]], axis=0), s_old[i]) for i in rng]
    v_new = [uw[i][:, 0:LANES] - ws[i][0:c] for i in rng]
    glast = [gcx[i][c - 1:c, :] for i in rng]
    intra = [_mm3(qk[i], _bdiag(v_new[i], lane_lo)) for i in rng]
    upd = [_mm3(kn[i] * jnp.exp(glast[i] - gcx[i]), v_new[i], _dot_tn) for i in rng]
    o = [ws[i][c:2 * c] + intra[i] for i in rng]
    ms = [_mm2(o[i] * o[i], bd) * (1.0 / HEAD_DIM) for i in rng]
    for i, (b, p) in enumerate(chains):
        sl = slice(p * LANES, (p + 1) * LANES)
        s_scr[b, p] = s_old[i] * jnp.exp(glast[i]) + jnp.where(same_head, upd[i], 0.0)
        o_ref[b, :, sl] = o[i] * lax.rsqrt(ms[i] + EPS) * gn_ref[...] * _silu(zg_ref[b, :, sl])

    @pl.when(ci == nc - 1)
    def _():
        sfin_ref[...] = s_scr[...]


def _gdn_prompt(qkv, zg, b_x, a_x, ba, conv_w, a_log, dt_bias, norm_g, c=64):
    n, t, _ = qkv.shape
    nc = t // c
    a_raw = ba[..., N_DN_HEADS:2 * N_DN_HEADS]
    a_row = a_raw.reshape(n, nc, c, N_PAIRS, 2).transpose(0, 1, 3, 4, 2).reshape(n, nc, N_PAIRS, 2 * c)
    rep = lambda v: jnp.repeat(v.reshape(N_PAIRS, 2), c, axis=1)
    lanes = lambda v: jnp.repeat(v, HEAD_DIM).reshape(1, DN_W)
    i = np.arange(c)
    ltri = jnp.asarray((i[:, None] >= i[None, :]).astype(np.float32)).astype(BF16)
    j = np.arange(2 * c)
    ubd = jnp.asarray(((j[:, None] // c == j[None, :] // c) & (j[:, None] % c <= j[None, :] % c)).astype(np.float32))
    assert 2 * c == LANES
    kern = functools.partial(_gdn_prompt_body, c=c, nseq=n)
    tok = lambda w: pl.BlockSpec((n, c, w), lambda i: (0, i, 0))
    return pl.pallas_call(
        kern,
        grid=(nc,),
        in_specs=[tok(3 * DN_W), tok(DN_W), tok(DN_W), tok(DN_W),
                  pl.BlockSpec((n, 1, N_PAIRS, LANES), lambda i: (0, i, 0, 0)),
                  _const_spec((DN_CONV, 3 * DN_W)),
                  _const_spec((1, DN_W)), _const_spec((1, DN_W)),
                  _const_spec((N_PAIRS, LANES)), _const_spec((N_PAIRS, LANES)),
                  _const_spec((1, LANES)), _const_spec((LANES, LANES)),
                  _const_spec((c, c)), _const_spec((LANES, LANES))],
        out_specs=[tok(DN_W), _const_spec((n, N_PAIRS, LANES, LANES))],
        out_shape=[jax.ShapeDtypeStruct((n, t, DN_W), F32),
                   jax.ShapeDtypeStruct((n, N_PAIRS, LANES, LANES), F32)],
        scratch_shapes=[pltpu.VMEM((n, c + 8, 3 * DN_W), F32), pltpu.VMEM((n, N_PAIRS, LANES, LANES), F32)],
        compiler_params=_cp(1),
        name="gdn_prompt",
    )(qkv, zg, b_x, a_x, a_row, conv_w, lanes(a_log), lanes(dt_bias),
      rep(a_log), rep(dt_bias), jnp.tile(norm_g.reshape(1, HEAD_DIM), (1, 2)), _bd_ones(BF16), ltri, ubd.astype(BF16))


def _state_to_bd(s):
    n = s.shape[0]
    s = s.reshape(n, N_PAIRS, 2, HEAD_DIM, HEAD_DIM)
    z = jnp.zeros_like(s[:, :, 0])
    top = jnp.concatenate([s[:, :, 0], z], axis=-1)
    bot = jnp.concatenate([z, s[:, :, 1]], axis=-1)
    return jnp.concatenate([top, bot], axis=-2)


def _state_from_bd(sbd):
    n = sbd.shape[0]
    s = jnp.stack([sbd[:, :, :HEAD_DIM, :HEAD_DIM], sbd[:, :, HEAD_DIM:, HEAD_DIM:]], axis=2)
    return s.reshape(n, N_DN_HEADS, HEAD_DIM, HEAD_DIM)


def _gdn_decode_body(qkv_ref, cb_ref, zg_ref, bx_ref, ax_ref, cw_ref, alogx_ref, dtbx_ref, gn_ref, bd_ref,
                     s_ref, o_ref, snew_ref, cnew_ref):
    x = qkv_ref[...]
    buf = cb_ref[...]
    cw = cw_ref[...]
    y = cw[0:1] * buf[0:1] + cw[1:2] * buf[1:2] + cw[2:3] * buf[2:3] + cw[3:4] * x
    cnew_ref[...] = jnp.concatenate([buf[1:3], x], axis=0)
    cq = _silu(y)
    bexp = _sigmoid(bx_ref[...])
    eg = jnp.exp(-jnp.exp(alogx_ref[...]) * _softplus(ax_ref[...] + dtbx_ref[...]))
    bd = bd_ref[...]
    r128 = lax.broadcasted_iota(jnp.int32, (LANES, LANES), 0)
    c128 = lax.broadcasted_iota(jnp.int32, (LANES, LANES), 1)
    eye = r128 == c128
    rows8 = lambda v: jnp.broadcast_to(v, (8, v.shape[-1]))
    pr = range(N_PAIRS)
    sl = [slice(p * LANES, (p + 1) * LANES) for p in pr]
    qp = [cq[:, p * LANES:(p + 1) * LANES] for p in pr]
    kp = [cq[:, DN_W + p * LANES:DN_W + (p + 1) * LANES] for p in pr]
    vp = [cq[:, 2 * DN_W + p * LANES:2 * DN_W + (p + 1) * LANES] for p in pr]
    ssq_q = [_mm2(rows8(qp[p] * qp[p]), bd)[0:1] for p in pr]
    ssq_k = [_mm2(rows8(kp[p] * kp[p]), bd)[0:1] for p in pr]
    qn = [qp[p] * lax.rsqrt(ssq_q[p] + EPS) * (HEAD_DIM ** -0.5) for p in pr]
    kn = [kp[p] * lax.rsqrt(ssq_k[p] + EPS) for p in pr]
    s0 = [s_ref[p] * eg[:, sl[p]] for p in pr]
    ks = [_mm3(rows8(kn[p]), s0[p])[0:1] for p in pr]
    kcol = [_mm2(jnp.where(eye, jnp.broadcast_to(kn[p], (LANES, LANES)), 0.0), bd) for p in pr]
    s1 = [s0[p] + kcol[p] * ((vp[p] - ks[p]) * bexp[:, sl[p]]) for p in pr]
    o = [_mm3(rows8(qn[p]), s1[p])[0:1] for p in pr]
    ms = [_mm2(rows8(o[p] * o[p]), bd)[0:1] * (1.0 / HEAD_DIM) for p in pr]
    for p in pr:
        snew_ref[p] = s1[p]
        o_ref[:, sl[p]] = o[p] * lax.rsqrt(ms[p] + EPS) * gn_ref[...] * _silu(zg_ref[:, sl[p]])


def _gdn_decode(qkv, zg, b_x, a_x, conv_buf, s_bd, conv_w, a_log, dt_bias, norm_g):
    ns = qkv.shape[0]
    row = lambda w: pl.BlockSpec((None, 1, w), lambda i: (i, 0, 0))
    lanes = lambda v: jnp.repeat(v, HEAD_DIM).reshape(1, DN_W)
    return pl.pallas_call(
        _gdn_decode_body,
        grid=(ns,),
        in_specs=[row(3 * DN_W), pl.BlockSpec((None, DN_CONV - 1, 3 * DN_W), lambda i: (i, 0, 0)), row(DN_W),
                  row(DN_W), row(DN_W),
                  _const_spec((DN_CONV, 3 * DN_W)), _const_spec((1, DN_W)), _const_spec((1, DN_W)),
                  _const_spec((1, LANES)), _const_spec((LANES, LANES)),
                  pl.BlockSpec((None, N_PAIRS, LANES, LANES), lambda i: (i, 0, 0, 0))],
        out_specs=[row(DN_W), pl.BlockSpec((None, N_PAIRS, LANES, LANES), lambda i: (i, 0, 0, 0)),
                   pl.BlockSpec((None, DN_CONV - 1, 3 * DN_W), lambda i: (i, 0, 0))],
        out_shape=[jax.ShapeDtypeStruct((ns, 1, DN_W), F32),
                   jax.ShapeDtypeStruct((ns, N_PAIRS, LANES, LANES), F32),
                   jax.ShapeDtypeStruct((ns, DN_CONV - 1, 3 * DN_W), F32)],
        compiler_params=_cp(1),
        name="gdn_decode",
    )(qkv[:, None, :], conv_buf, zg[:, None, :], b_x[:, None, :], a_x[:, None, :], conv_w, lanes(a_log), lanes(dt_bias),
      jnp.tile(norm_g.reshape(1, HEAD_DIM), (1, 2)), _bd_ones(BF16), s_bd)


def _rms(x, g):
    return x * lax.rsqrt(jnp.mean(x * x, axis=-1, keepdims=True) + EPS) * g


def _ffn_prompt_body(x_ref, oa_ref, ob_ref, oc_ref, wo_ref, g2_ref, wg_ref, wu_ref, cw_ref, cb_ref, wd_ref, gf_ref,
                     y_ref, st_ref, acc, h2, gbuf, carry, *, tm, final):
    i = pl.program_id(1)
    j = pl.program_id(2)
    nj = pl.num_programs(2)

    @pl.when(j == 0)
    def _():
        x1 = (x_ref[0] + _dot(oa_ref[0].astype(BF16), wo_ref[0:256, :]) + _dot(ob_ref[0].astype(BF16), wo_ref[256:512, :])
              + _dot(oc_ref[0].astype(BF16), wo_ref[512:1024, :]))
        acc[...] = x1
        h2[...] = _rms(x1, g2_ref[...]).astype(BF16)

    @pl.when(i == 0)
    def _():
        carry[j] = jnp.zeros(carry.shape[1:], F32)

    h = h2[...]
    g = _dot(h, wg_ref[...])
    u = _dot(h, wu_ref[...])
    gbuf[0:8, :] = carry[j]
    gbuf[8:8 + tm, :] = g
    cw = cw_ref[...]
    gc = cw[2:3] * g + cw[1:2] * gbuf[7:7 + tm, :] + cw[0:1] * gbuf[6:6 + tm, :] + cb_ref[...]
    carry[j] = gbuf[tm:tm + 8, :]
    st_ref[0, j] = gbuf[tm + 6:tm + 8, :]
    act = (_silu(gc) * u).astype(BF16)
    acc[...] += _dot(act, wd_ref[...])

    @pl.when(j == nj - 1)
    def _():
        if final:
            y_ref[0] = _rms(acc[...], gf_ref[...])
        else:
            y_ref[0] = acc[...]


def _ffn_prompt(x, o_a, o_b, o_c, w_out, norm2_g, w_ffn_in, conv_w, conv_b, w_ffn_out, normf_g, final, tm=1024, tf=512):
    n, t, d = x.shape
    tm = min(tm, t)
    assert t % tm == 0
    nj = D_FF // tf
    kern = functools.partial(_ffn_prompt_body, tm=tm, final=final)
    rows = lambda w: pl.BlockSpec((1, tm, w), lambda b, i, j: (b, i, 0))
    y, st = pl.pallas_call(
        kern,
        grid=(n, t // tm, nj),
        in_specs=[rows(d), rows(256), rows(256), rows(512),
                  _const_spec((d, d)), _const_spec((1, d)),
                  pl.BlockSpec((d, tf), lambda b, i, j: (0, j)),
                  pl.BlockSpec((d, tf), lambda b, i, j: (0, j + nj)),
                  pl.BlockSpec((FFN_CONV, tf), lambda b, i, j: (0, j)),
                  pl.BlockSpec((1, tf), lambda b, i, j: (0, j)),
                  pl.BlockSpec((tf, d), lambda b, i, j: (j, 0)),
                  _const_spec((1, d))],
        out_specs=[rows(d), pl.BlockSpec((1, nj, FFN_CONV - 1, tf), lambda b, i, j: (b, 0, 0, 0))],
        out_shape=[jax.ShapeDtypeStruct((n, t, d), F32), jax.ShapeDtypeStruct((n, nj, FFN_CONV - 1, tf), F32)],
        scratch_shapes=[pltpu.VMEM((tm, d), F32), pltpu.VMEM((tm, d), BF16), pltpu.VMEM((tm + 8, tf), F32),
                        pltpu.VMEM((nj, 8, tf), F32)],
        compiler_params=_cp(3),
        name="ffn_prompt",
    )(x, o_a, o_b, o_c, w_out, norm2_g.reshape(1, d), w_ffn_in, w_ffn_in, conv_w, conv_b.reshape(1, D_FF),
      w_ffn_out, normf_g.reshape(1, d))
    return y, st.transpose(0, 2, 1, 3).reshape(n, FFN_CONV - 1, D_FF)


def _ffn_decode_body(x_ref, oa_ref, ob_ref, oc_ref, wo_ref, g2_ref, wg_ref, wu_ref, cw_ref, cb_ref, wd_ref, gf_ref,
                     prev_ref, y_ref, st_ref, acc, h2, *, final):
    j = pl.program_id(0)
    nj = pl.num_programs(0)

    @pl.when(j == 0)
    def _():
        x1 = (x_ref[...] + _dot(oa_ref[...].astype(BF16), wo_ref[0:256, :]) + _dot(ob_ref[...].astype(BF16), wo_ref[256:512, :])
              + _dot(oc_ref[...].astype(BF16), wo_ref[512:1024, :]))
        acc[...] = x1
        h2[...] = _rms(x1, g2_ref[...]).astype(BF16)

    h = h2[...]
    g = _dot(h, wg_ref[...])
    u = _dot(h, wu_ref[...])
    cw = cw_ref[...]
    gc = cw[2:3] * g + cw[1:2] * prev_ref[1] + cw[0:1] * prev_ref[0] + cb_ref[...]
    st_ref[0] = prev_ref[1]
    st_ref[1] = g
    act = (_silu(gc) * u).astype(BF16)
    acc[...] += _dot(act, wd_ref[...])

    @pl.when(j == nj - 1)
    def _():
        if final:
            y_ref[...] = _rms(acc[...], gf_ref[...])
        else:
            y_ref[...] = acc[...]


def _ffn_decode(x, o_a, o_b, o_c, prev, w_out, norm2_g, w_ffn_in, conv_w, conv_b, w_ffn_out, normf_g, final, tf=512):
    ns, d = x.shape
    nj = D_FF // tf
    kern = functools.partial(_ffn_decode_body, final=final)
    return pl.pallas_call(
        kern,
        grid=(nj,),
        in_specs=[_const_spec((ns, d)), _const_spec((ns, 256)), _const_spec((ns, 256)), _const_spec((ns, 512)),
                  _const_spec((d, d)), _const_spec((1, d)),
                  pl.BlockSpec((d, tf), lambda j: (0, j)),
                  pl.BlockSpec((d, tf), lambda j: (0, j + nj)),
                  pl.BlockSpec((FFN_CONV, tf), lambda j: (0, j)),
                  pl.BlockSpec((1, tf), lambda j: (0, j)),
                  pl.BlockSpec((tf, d), lambda j: (j, 0)),
                  _const_spec((1, d)),
                  pl.BlockSpec((FFN_CONV - 1, ns, tf), lambda j: (0, 0, j))],
        out_specs=[_const_spec((ns, d)), pl.BlockSpec((FFN_CONV - 1, ns, tf), lambda j: (0, 0, j))],
        out_shape=[jax.ShapeDtypeStruct((ns, d), F32), jax.ShapeDtypeStruct((FFN_CONV - 1, ns, D_FF), F32)],
        scratch_shapes=[pltpu.VMEM((ns, d), F32), pltpu.VMEM((ns, d), BF16)],
        compiler_params=_cp(1),
        name="ffn_decode",
    )(x, o_a, o_b, o_c, w_out, norm2_g.reshape(1, d), w_ffn_in, w_ffn_in, conv_w, conv_b.reshape(1, D_FF),
      w_ffn_out, normf_g.reshape(1, d), prev)


ROWS = 16
PAGES_PER_STEP = 16


def _decode_consts():
    j = np.arange(ROWS)[:, None]
    lane = np.arange(SOFT_W)[None, :]
    is_diff = j < 8
    is_moba = (j >= 8) & (j < 12)
    q_lo = np.where(is_diff, (j // 2) * HEAD_DIM + (j % 2) * DIFF_HALF, 256 + (j - 8) * HEAD_DIM)
    q_w = np.where(is_diff, DIFF_HALF, HEAD_DIM)
    qmask = (lane >= q_lo) & (lane < q_lo + q_w) & (is_diff | is_moba)
    qscale = np.where(is_diff, DIFF_HALF ** -0.5, HEAD_DIM ** -0.5) * LOG2E
    v_lo = np.where(is_diff, (j // 2) * HEAD_DIM, 256 + (j - 8) * HEAD_DIM)
    vmask = (lane >= v_lo) & (lane < v_lo + HEAD_DIM) & (is_diff | is_moba)
    head = np.where(is_diff, j // 2, j - 8)
    slope = np.where(is_diff, 2.0 ** -(2 * head + 1), np.where(is_moba, 2.0 ** -(2 * head + 2), 0.0)) * LOG2E
    coef_a = np.where((is_diff & (j % 2 == 0)) | is_moba, 1.0, 0.0)
    coef_b = np.where(is_diff & (j % 2 == 1), -1.0, 0.0)
    bc = lambda a: jnp.asarray(np.broadcast_to(a, (ROWS, LANES)).astype(np.float32))
    i = np.arange(SOFT_W)
    bd512 = (i[:, None] // HEAD_DIM == i[None, :] // HEAD_DIM).astype(np.float32)
    return (jnp.asarray((qmask * qscale).astype(np.float32)), jnp.asarray((qmask & is_moba).astype(np.float32)),
            jnp.asarray(vmask.astype(np.float32)), bc(slope), bc(coef_a), bc(coef_b), jnp.asarray(bd512))


def _decode_attn_body(pt_ref, lam_ref, q_ref, ks_ref, vs_ref, *rest, nblk, npg, page, past_len, out_scale):
    del pt_ref
    k_refs = rest[0:npg]
    v_refs = rest[npg:2 * npg]
    qmask_ref, gmask_ref, vmask_ref, slope_ref, ca_ref, cb_ref, bd_ref, g_ref, o_ref = rest[2 * npg:2 * npg + 9]
    m_scr, l_scr, acc_scr, mblk, lblk, gblk, accblk = rest[2 * npg + 9:]
    b = pl.program_id(1)
    nstep = pl.num_programs(1)
    tk = 2 * page

    @pl.when(b == 0)
    def _():
        m_scr[...] = jnp.full(m_scr.shape, -jnp.inf, F32)
        l_scr[...] = jnp.zeros(l_scr.shape, F32)
        acc_scr[...] = jnp.zeros(acc_scr.shape, F32)
        mblk[...] = jnp.full(mblk.shape, -jnp.inf, F32)
        lblk[...] = jnp.zeros(lblk.shape, F32)
        gblk[...] = jnp.full(gblk.shape, -jnp.inf, F32)

    qs = q_ref[...] * qmask_ref[...]
    qsb = qs.astype(BF16)
    qgb = (q_ref[...] * gmask_ref[...]).astype(BF16)
    slope = slope_ref[:, 0:1]
    lane = lax.broadcasted_iota(jnp.int32, (ROWS, LANES), 1)
    tok = lax.broadcasted_iota(jnp.int32, (1, tk), 1)

    nu = npg // 2
    raw = [jnp.concatenate([_dot(qsb, k_refs[2 * u][...].astype(BF16)), _dot(qsb, k_refs[2 * u + 1][...].astype(BF16))],
                           axis=1) for u in range(nu)]
    mbs, lbs, pes = [], [], []
    for u in range(nu):
        dist = (past_len - ((b * nu + u) * tk + tok)).astype(F32)
        s = raw[u] - slope * dist
        mb = jnp.max(s, axis=-1, keepdims=True)
        pexp = jnp.exp2(s - mb)
        mbs.append(mb)
        lbs.append(jnp.sum(pexp, axis=-1, keepdims=True))
        pes.append(pexp.astype(BF16))
    accbs = [_dot_nt(pes[u][:, 0:page], v_refs[2 * u][...].astype(BF16))
             + _dot_nt(pes[u][:, page:tk], v_refs[2 * u + 1][...].astype(BF16)) for u in range(nu)]
    kmeans = [jnp.sum(k_refs[2 * u][...] + k_refs[2 * u + 1][...], axis=-1, keepdims=True) * (1.0 / tk)
              for u in range(nu)]
    gates = [_dot(qgb, kmeans[u].astype(BF16)) for u in range(nu)]

    for u in range(nu):
        bb = b * nu + u
        mb, lb, accb = mbs[u], lbs[u], accbs[u]
        m_old = m_scr[:, 0:1]
        m_new = jnp.maximum(m_old, mb)
        a_old = jnp.exp2(m_old - m_new)
        a_blk = jnp.exp2(mb - m_new)
        l_scr[...] = jnp.broadcast_to(a_old * l_scr[:, 0:1] + a_blk * lb, l_scr.shape)
        acc_scr[...] = a_old * acc_scr[...] + a_blk * accb
        m_scr[...] = jnp.broadcast_to(m_new, m_scr.shape)
        hit = lane == bb
        mblk[...] = jnp.where(hit, mb, mblk[...])
        lblk[...] = jnp.where(hit, lb, lblk[...])
        gblk[...] = jnp.where(hit, gates[u], gblk[...])
        accblk[bb] = accb

    @pl.when(b == nstep - 1)
    def _():
        s_self = jnp.sum(qs * ks_ref[...], axis=-1, keepdims=True)
        vs = vs_ref[...]
        m_o = m_scr[:, 0:1]
        m_d = jnp.maximum(m_o, s_self)
        a_o = jnp.exp2(m_o - m_d)
        p_d = jnp.exp2(s_self - m_d)
        o_d = (a_o * acc_scr[...] + p_d * vs) / (a_o * l_scr[:, 0:1] + p_d)
        sel = _topk_select(gblk[...], lane, nblk, 1, LANES)
        mm = mblk[...]
        m_f = jnp.maximum(jnp.max(jnp.where(sel, mm, -jnp.inf), axis=-1, keepdims=True), s_self)
        wgt = jnp.where(sel, jnp.exp2(mm - m_f), 0.0)
        p_m = jnp.exp2(s_self - m_f)
        l_m = jnp.sum(wgt * lblk[...], axis=-1, keepdims=True) + p_m
        acc_m = p_m * vs
        for blk in range(nblk):
            acc_m = acc_m + wgt[:, blk:blk + 1] * accblk[blk]
        o_m = acc_m / l_m
        rowi = lax.broadcasted_iota(jnp.int32, (ROWS, SOFT_W), 0)
        o_all = jnp.where(rowi < 8, o_d, o_m)
        coef = ca_ref[:, 0:1] + lam_ref[0] * cb_ref[:, 0:1]
        o_row = jnp.sum(coef * vmask_ref[...] * o_all, axis=0, keepdims=True)
        ms = _dot(jnp.broadcast_to(o_row * o_row, (8, SOFT_W)), bd_ref[...], HI)[0:1] * (1.0 / HEAD_DIM)
        o_norm = o_row * lax.rsqrt(ms + EPS) * g_ref[...] * out_scale
        lane5 = lax.broadcasted_iota(jnp.int32, (1, SOFT_W), 1)
        o_ref[...] = jnp.where(lane5 < 256, o_norm, o_row)


def _decode_attn(layer, q_soft, k_self, v_self, cache_k, cache_v, page_table, lam, subln_g, lam_init):
    ns = q_soft.shape[0]
    n_pages = page_table.shape[1]
    page = cache_k.shape[3]
    npg = PAGES_PER_STEP
    assert 2 * page == MOBA_BLOCK and n_pages % npg == 0 and cache_k.shape[2] == SOFT_W
    nblk = n_pages // 2
    assert nblk <= LANES
    kern = functools.partial(_decode_attn_body, nblk=nblk, npg=npg, page=page, past_len=n_pages * page,
                             out_scale=1.0 - lam_init)
    row = pl.BlockSpec((None, 1, SOFT_W), lambda s, b, pt: (s, 0, 0))
    pg = lambda off: pl.BlockSpec((None, None, SOFT_W, page), lambda s, b, pt: (layer, pt[s, npg * b + off], 0, 0))
    cst = lambda shp: pl.BlockSpec(shp, lambda s, b, pt: (0,) * len(shp))
    g4 = jnp.concatenate([jnp.tile(subln_g.reshape(1, HEAD_DIM), (1, 4)), jnp.ones((1, 256), F32)], axis=1)
    grid_spec = pltpu.PrefetchScalarGridSpec(
        num_scalar_prefetch=1,
        grid=(ns, n_pages // npg),
        in_specs=([pl.BlockSpec(memory_space=pltpu.SMEM), row, row, row]
                  + [pg(u) for u in range(npg)] + [pg(u) for u in range(npg)]
                  + [cst((ROWS, SOFT_W)), cst((ROWS, SOFT_W)), cst((ROWS, SOFT_W)), cst((ROWS, LANES)), cst((ROWS, LANES)),
                     cst((ROWS, LANES)), cst((SOFT_W, SOFT_W)), cst((1, SOFT_W))]),
        out_specs=row,
        scratch_shapes=[pltpu.VMEM((ROWS, LANES), F32), pltpu.VMEM((ROWS, LANES), F32), pltpu.VMEM((ROWS, SOFT_W), F32),
                        pltpu.VMEM((ROWS, LANES), F32), pltpu.VMEM((ROWS, LANES), F32), pltpu.VMEM((ROWS, LANES), F32),
                        pltpu.VMEM((nblk, ROWS, SOFT_W), F32)],
    )
    out = pl.pallas_call(
        kern,
        grid_spec=grid_spec,
        out_shape=jax.ShapeDtypeStruct((ns, 1, SOFT_W), F32),
        compiler_params=_cp(2),
        name="decode_attn",
    )(page_table, lam.reshape(1), q_soft[:, None, :], k_self[:, None, :], v_self[:, None, :],
      *([cache_k] * npg), *([cache_v] * npg), *_decode_consts(), g4)
    return out[:, 0, :]


def _pages_transposed(cache):
    d, n_pool, page = cache.shape[0:3]
    return jnp.transpose(cache, (0, 1, 3, 4, 2)).reshape(d, n_pool, SOFT_W, page)


def _lam(l, lam_q1, lam_k1, lam_q2, lam_k2):
    lam_init = 0.8 - 0.6 * math.exp(-0.3 * l)
    lam = (jnp.exp(jnp.sum(lam_q1[l] * lam_k1[l])) - jnp.exp(jnp.sum(lam_q2[l] * lam_k2[l])) + lam_init)
    return lam.astype(F32), lam_init


def _prompt_layer(x, w, lam, lam_init, final):
    n, t, d = x.shape
    outs = _proj_in(x.reshape(n * t, d), w["norm1"], w["w_in"], seq_len=t)
    qs, kr, vr, qkv, zg, b_x, a_x, ba = [a.reshape(n, t, a.shape[-1]) for a in outs[:len(_PROJ_WIDTHS)]]
    rows_t = lambda a: jnp.transpose(a.reshape(n, N_SOFT_HEADS, HEAD_DIM, t), (0, 3, 1, 2))
    o_a = _diff_attn_prompt(qs, kr, vr, lam, w["subln"], lam_init)
    o_b = _moba_prompt(qs, kr, vr)
    o_c, s_bd = _gdn_prompt(qkv, zg, b_x, a_x, ba, w["dn_conv_w"], w["dn_a_log"], w["dn_dt_bias"], w["dn_norm"])
    y, ffn_new = _ffn_prompt(x, o_a, o_b, o_c, w["w_out"], w["norm2"], w["ffn_in"], w["ffn_conv_w"], w["ffn_conv_b"],
                             w["ffn_out"], w["normf"], final)
    rows = (rows_t(outs[-2]), rows_t(outs[-1]), _state_from_bd(s_bd),
            qkv[:, t - (DN_CONV - 1):, :], ffn_new)
    return y, rows


def _sample_layer(l, x, w, lam, lam_init, final, cache_k, cache_v, page_table, state_dn, conv_qkv, conv_ffn):
    ns, d = x.shape
    qs, kr, vr, qkv, zg, b_x, a_x, _ = _proj_in(x, w["norm1"], w["w_in"])
    o_ab = _decode_attn(l, qs, kr, vr, cache_k, cache_v, page_table, lam, w["subln"], lam_init)
    o_c, s_bd, conv_new = _gdn_decode(qkv, zg, b_x, a_x, conv_qkv, _state_to_bd(state_dn), w["dn_conv_w"], w["dn_a_log"],
                                      w["dn_dt_bias"], w["dn_norm"])
    y, ffn_new = _ffn_decode(x, o_ab[:, 0:256], o_ab[:, 256:512], o_c[:, 0, :], jnp.swapaxes(conv_ffn, 0, 1),
                             w["w_out"], w["norm2"], w["ffn_in"], w["ffn_conv_w"], w["ffn_conv_b"], w["ffn_out"],
                             w["normf"], final)
    rows = (kr.reshape(ns, 1, 8, HEAD_DIM), vr.reshape(ns, 1, 8, HEAD_DIM), _state_from_bd(s_bd), conv_new,
            jnp.swapaxes(ffn_new, 0, 1))
    return y, rows


def kernel(x_prompt, x_sample, cache_k, cache_v, page_table, state_dn, state_conv_qkv, state_conv_ffn, norm1_g, norm2_g, normf_g, w_in, w_out, lam_q1, lam_k1, lam_q2, lam_k2, subln_g, dn_conv_w, dn_a_log, dn_dt_bias, dn_norm_g, ffn_w_in, ffn_conv_w, ffn_conv_b, ffn_w_out):
    depth = w_in.shape[0]
    cache_k = _pages_transposed(cache_k)
    cache_v = _pages_transposed(cache_v)
    xp = x_prompt
    xs = x_sample.reshape(x_sample.shape[0], x_sample.shape[2])
    rows_p, rows_s = [], []
    for l in range(depth):
        w = {"norm1": norm1_g[l], "w_in": _permute_w_in(w_in[l]), "subln": subln_g[l], "dn_conv_w": dn_conv_w[l],
             "dn_a_log": dn_a_log[l], "dn_dt_bias": dn_dt_bias[l], "dn_norm": dn_norm_g[l],
             "w_out": w_out[l].astype(BF16), "norm2": norm2_g[l], "ffn_in": ffn_w_in[l].astype(BF16),
             "ffn_conv_w": ffn_conv_w[l], "ffn_conv_b": ffn_conv_b[l], "ffn_out": ffn_w_out[l].astype(BF16),
             "normf": normf_g}
        lam, lam_init = _lam(l, lam_q1, lam_k1, lam_q2, lam_k2)
        final = l == depth - 1
        xp, rp = _prompt_layer(xp, w, lam, lam_init, final)
        xs, rs = _sample_layer(l, xs, w, lam, lam_init, final, cache_k, cache_v, page_table, state_dn[l],
                               state_conv_qkv[l], state_conv_ffn[l])
        rows_p.append(rp)
        rows_s.append(rs)
    stack = lambda rows, i: jnp.stack([r[i] for r in rows], axis=0)
    y_sample = xs.reshape(x_sample.shape)
    return (xp, y_sample,
            stack(rows_p, 0), stack(rows_p, 1), stack(rows_p, 2), stack(rows_p, 3), stack(rows_p, 4),
            stack(rows_s, 0), stack(rows_s, 1), stack(rows_s, 2), stack(rows_s, 3), stack(rows_s, 4))
```

```python
import functools
import math

import numpy as np
import jax
import jax.numpy as jnp
from jax import lax
from jax.experimental import pallas as pl
from jax.experimental.pallas import tpu as pltpu

F32 = jnp.float32
BF16 = jnp.bfloat16
HI = lax.Precision.HIGHEST

LANES = 128
HEAD_DIM = 64
DIFF_HALF = 32
N_DIFF_HEADS = 4
N_MOBA_HEADS = 4
N_SOFT_HEADS = 8
N_DN_HEADS = 8
N_PAIRS = 4
SOFT_W = 512
DN_W = 512
D_FF = 3584
MOBA_BLOCK = 256
MOBA_TOPK = 3
DN_CONV = 4
FFN_CONV = 3
EPS = 1e-6
LOG2E = 1.4426950408889634
NEG_BIG = -1e30
Z_W = 3584 + 2 * DN_W + LANES
VMEM_LIMIT = 56 * 1024 * 1024


def _cp(n_axes, vmem=VMEM_LIMIT):
    return pltpu.CompilerParams(dimension_semantics=("arbitrary",) * n_axes, vmem_limit_bytes=vmem)


def _sigmoid(x):
    return 1.0 / (1.0 + jnp.exp(-x))


def _silu(x):
    return x * _sigmoid(x)


def _softplus(x):
    return jnp.maximum(x, 0.0) + jnp.log(1.0 + jnp.exp(-jnp.abs(x)))


def _dot(a, b, prec=None):
    return jnp.dot(a, b, preferred_element_type=F32, precision=prec)


def _dot_nt(a, b, prec=None):
    return lax.dot_general(a, b, (((1,), (1,)), ((), ())), preferred_element_type=F32, precision=prec)


def _dot_tn(a, b, prec=None):
    return lax.dot_general(a, b, (((0,), (0,)), ((), ())), preferred_element_type=F32, precision=prec)


def _split2(x):
    hi = x.astype(BF16)
    return hi, (x - hi.astype(F32)).astype(BF16)


def _mm3(a, b, dot=_dot):
    ah, al = _split2(a)
    bh, bl = _split2(b)
    return dot(ah, bh) + dot(ah, bl) + dot(al, bh)


def _mm2(a, b01, dot=_dot):
    ah, al = _split2(a)
    return dot(ah, b01) + dot(al, b01)


def _const_spec(shape):
    nd = len(shape)
    return pl.BlockSpec(shape, lambda *_: (0,) * nd)


def _bd_ones(dtype=F32):
    i = np.arange(LANES)
    return jnp.asarray((i[:, None] // HEAD_DIM == i[None, :] // HEAD_DIM).astype(np.float32)).astype(dtype)


_PROJ_WIDTHS = (512, 512, 512, 1536, 512, 512, 512, LANES)


def _proj_in_body(x_ref, g_ref, w_ref, *out_refs, transposed_kv):
    x = x_ref[...]
    ms = jnp.mean(x * x, axis=-1, keepdims=True)
    h = (x * lax.rsqrt(ms + EPS) * g_ref[...]).astype(BF16)
    lo = 0
    for idx, w in enumerate(_PROJ_WIDTHS):
        z = _dot(h, w_ref[:, lo:lo + w])
        out_refs[idx][...] = z
        if transposed_kv and idx in (1, 2):
            out_refs[len(_PROJ_WIDTHS) + idx - 1][0] = z.T
        lo += w


def _proj_in(x2d, g, w_perm, seq_len=None):
    m, d = x2d.shape
    tm = min(m, 256)
    out_specs = [pl.BlockSpec((tm, w), lambda i: (i, 0)) for w in _PROJ_WIDTHS]
    out_shape = [jax.ShapeDtypeStruct((m, w), F32) for w in _PROJ_WIDTHS]
    if seq_len is not None:
        tps = seq_len // tm
        out_specs += [pl.BlockSpec((1, SOFT_W, tm), lambda i: (i // tps, 0, i % tps))] * 2
        out_shape += [jax.ShapeDtypeStruct((m // seq_len, SOFT_W, seq_len), F32)] * 2
    return pl.pallas_call(
        functools.partial(_proj_in_body, transposed_kv=seq_len is not None),
        grid=(m // tm,),
        in_specs=[pl.BlockSpec((tm, d), lambda i: (i, 0)), _const_spec((1, d)), _const_spec((d, Z_W))],
        out_specs=out_specs,
        out_shape=out_shape,
        compiler_params=_cp(1),
        name="proj_in",
    )(x2d, g.reshape(1, d), w_perm)


def _permute_w_in(w):
    cols = [w[:, 0:256], w[:, 768:1024], w[:, 256:512], w[:, 1024:1280], w[:, 512:768], w[:, 1280:1536],
            w[:, 1536:3584], jnp.repeat(w[:, 3584:3592], HEAD_DIM, axis=1), jnp.repeat(w[:, 3592:3600], HEAD_DIM, axis=1),
            w[:, 3584:3600], jnp.zeros((w.shape[0], LANES - 2 * N_DN_HEADS), w.dtype)]
    return jnp.concatenate(cols, axis=1).astype(BF16)


def _prep_kv(k_ref, v_ref, kb_scr, vt_scr, t, tq):
    kb_scr[...] = k_ref[0].astype(BF16)
    row_head = lax.broadcasted_iota(jnp.int32, (LANES, tq), 0) // HEAD_DIM
    for c in range(t // tq):
        vt = v_ref[0, c * tq:(c + 1) * tq, :].T
        for hl in range(2):
            vt_scr[hl, c] = jnp.where(row_head == hl, vt, 1.0).astype(BF16)


def _denominator(acc_ref, j, hl):
    r = (1 - hl) * HEAD_DIM
    return acc_ref[j, r:r + 1, :]


def _online_stats_t(j, sp, shift, m_scr):
    m_old = m_scr[j]
    m_new = jnp.maximum(m_old, jnp.max(sp, axis=0, keepdims=True) + shift)
    p = jnp.exp2(sp - (m_new - shift))
    m_scr[j] = m_new
    return p.astype(BF16), jnp.exp2(m_old - m_new)


def _diff_attn_body(lam_ref, q_ref, k_ref, v_ref, g_ref, o_ref,
                    kb_scr, vt_scr, qm_scr, sa_scr, sb_scr, m_scr, acc_scr, *, t, tq, out_scale):
    p = pl.program_id(1)
    qi = pl.program_id(2)

    lane = lax.broadcasted_iota(jnp.int32, (tq, LANES), 1)

    @pl.when(qi == 0)
    def _():
        row_head = lax.broadcasted_iota(jnp.int32, (LANES, tq), 0) // HEAD_DIM
        for c in range(t // tq):
            kt = k_ref[0, c * tq:(c + 1) * tq, :]
            vt = v_ref[0, c * tq:(c + 1) * tq, :].T
            for hl in range(2):
                halves = [jnp.where((lane >= hl * HEAD_DIM + c2 * DIFF_HALF) & (lane < hl * HEAD_DIM + (c2 + 1) * DIFF_HALF),
                                    kt, 0.0).astype(BF16) for c2 in range(2)]
                kb_scr[hl, c] = jnp.concatenate(halves, axis=0)
                vt_scr[hl, c] = jnp.where(row_head == hl, vt, 1.0).astype(BF16)

    q = q_ref[0] * (DIFF_HALF ** -0.5 * LOG2E)
    for hl in range(2):
        qm_scr[hl] = jnp.where((lane >= hl * HEAD_DIM) & (lane < (hl + 1) * HEAD_DIM), q, 0.0).astype(BF16)
    m_scr[...] = jnp.full(m_scr.shape, -jnp.inf, F32)
    acc_scr[...] = jnp.zeros(acc_scr.shape, F32)
    slopes = [jnp.where(p == 0, LOG2E * 2.0 ** -(2 * hl + 1), LOG2E * 2.0 ** -(2 * (hl + 2) + 1)).astype(F32)
              for hl in range(2)]
    krow = lax.broadcasted_iota(jnp.int32, (tq, tq), 0)
    qcol = lax.broadcasted_iota(jnp.int32, (tq, tq), 1)
    krow_f = krow.astype(F32)
    ramps = [slopes[hl] * krow_f for hl in range(2)]

    def scores(ki, s_ref):
        for hl in range(2):
            s_ref[hl] = _dot_nt(kb_scr[hl, ki], qm_scr[hl])

    def softmax_pv(ki, s_ref, masked):
        off = ((ki - qi) * tq).astype(F32)
        pa = []
        for j in range(4):
            sp = s_ref[j // 2, (j % 2) * tq:(j % 2 + 1) * tq, :] + ramps[j // 2]
            if masked:
                sp = jnp.where(krow <= qcol, sp, -jnp.inf)
            pa.append(_online_stats_t(j, sp, slopes[j // 2] * off, m_scr))
        for j in range(4):
            acc_scr[j] = pa[j][1] * acc_scr[j] + _dot(vt_scr[j // 2, ki], pa[j][0])

    scores(0, sa_scr)

    def body(i, carry):
        ki = 2 * i
        scores(ki + 1, sb_scr)
        softmax_pv(ki, sa_scr, False)
        scores(ki + 2, sa_scr)
        softmax_pv(ki + 1, sb_scr, False)
        return carry

    lax.fori_loop(0, qi // 2, body, 0)

    @pl.when(qi % 2 == 1)
    def _():
        scores(qi, sb_scr)
        softmax_pv(qi - 1, sa_scr, False)
        softmax_pv(qi, sb_scr, True)

    @pl.when(qi % 2 == 0)
    def _():
        softmax_pv(qi, sa_scr, True)

    lam = lam_ref[0]
    outs = []
    for hl in range(2):
        r = slice(hl * HEAD_DIM, (hl + 1) * HEAD_DIM)
        o = (acc_scr[2 * hl, r, :] * (1.0 / _denominator(acc_scr, 2 * hl, hl))
             - lam * (acc_scr[2 * hl + 1, r, :] * (1.0 / _denominator(acc_scr, 2 * hl + 1, hl))))
        ms = jnp.sum(o * o, axis=0, keepdims=True) * (1.0 / HEAD_DIM)
        outs.append(o * lax.rsqrt(ms + EPS))
    ot = jnp.concatenate(outs, axis=0) * g_ref[...] * out_scale
    o_ref[0] = ot.T


def _diff_attn_prompt(q_soft, k_rows, v_rows, lam, subln_g, lam_init, tq=256):
    n, t, _ = q_soft.shape
    tq = min(tq, t)
    g2 = jnp.tile(subln_g.reshape(HEAD_DIM, 1), (2, 1))
    kern = functools.partial(_diff_attn_body, t=t, tq=tq, out_scale=1.0 - lam_init)
    return pl.pallas_call(
        kern,
        grid=(n, 2, t // tq),
        in_specs=[pl.BlockSpec(memory_space=pltpu.SMEM),
                  pl.BlockSpec((1, tq, LANES), lambda b, p, i: (b, i, p)),
                  pl.BlockSpec((1, t, LANES), lambda b, p, i: (b, 0, p)),
                  pl.BlockSpec((1, t, LANES), lambda b, p, i: (b, 0, p)),
                  _const_spec((LANES, 1))],
        out_specs=pl.BlockSpec((1, tq, LANES), lambda b, p, i: (b, i, p)),
        out_shape=jax.ShapeDtypeStruct((n, t, 2 * LANES), F32),
        scratch_shapes=[pltpu.VMEM((2, t // tq, 2 * tq, LANES), BF16), pltpu.VMEM((2, t // tq, LANES, tq), BF16),
                        pltpu.VMEM((2, tq, LANES), BF16), pltpu.VMEM((2, 2 * tq, tq), F32), pltpu.VMEM((2, 2 * tq, tq), F32),
                        pltpu.VMEM((4, 1, tq), F32),
                        pltpu.VMEM((4, LANES, tq), F32)],
        compiler_params=_cp(3),
        name="diff_attn_prompt",
    )(lam.reshape(1), q_soft, k_rows, v_rows, g2)


def _topk_select(gate, idx, n_valid, axis, size):
    gm = jnp.where(idx < n_valid, gate, -jnp.inf)
    sel = jnp.zeros(gate.shape, jnp.bool_)
    for _ in range(MOBA_TOPK):
        mx = jnp.max(gm, axis=axis, keepdims=True)
        is_max = (gm == mx) & (mx > -jnp.inf)
        first = jnp.min(jnp.where(is_max, idx, size), axis=axis, keepdims=True)
        pick = idx == first
        sel = sel | pick
        gm = jnp.where(pick, -jnp.inf, gm)
    return sel


def _moba_body(q_ref, k_ref, v_ref, o_ref, kb_scr, vt_scr, kmean_scr, qs_scr, sel_scr, sd_scr, sa_scr, sb_scr,
               m_scr, acc_scr, *, t, tq, nb):
    nbp = kmean_scr.shape[0]
    p = pl.program_id(1)
    qi = pl.program_id(2)

    @pl.when(qi == 0)
    def _():
        _prep_kv(k_ref, v_ref, kb_scr, vt_scr, t, tq)
        kmean_scr[...] = jnp.zeros(kmean_scr.shape, F32)
        for b in range(nb):
            kmean_scr[b:b + 1, :] = jnp.sum(k_ref[0, b * MOBA_BLOCK:(b + 1) * MOBA_BLOCK, :], axis=0,
                                            keepdims=True) * (1.0 / MOBA_BLOCK)

    lane = lax.broadcasted_iota(jnp.int32, (tq, LANES), 1)
    blk = lax.broadcasted_iota(jnp.int32, (nbp, tq), 0)
    q = q_ref[0]
    kmean = kmean_scr[...]
    for hl in range(2):
        qh = jnp.where((lane >= hl * HEAD_DIM) & (lane < (hl + 1) * HEAD_DIM), q, 0.0)
        gate_t = _dot_nt(kmean.astype(BF16), qh.astype(BF16))
        sel = _topk_select(gate_t, blk, qi, 0, nbp)
        sel_scr[hl] = jnp.where(sel, 0.0, NEG_BIG)
        qs_scr[hl] = (qh * (HEAD_DIM ** -0.5 * LOG2E)).astype(BF16)
    m_scr[...] = jnp.full(m_scr.shape, -jnp.inf, F32)
    acc_scr[...] = jnp.zeros(acc_scr.shape, F32)
    slopes = [jnp.where(p == 0, LOG2E * 2.0 ** -(2 * hl + 2), LOG2E * 2.0 ** -(2 * (hl + 2) + 2)).astype(F32)
              for hl in range(2)]
    krow = lax.broadcasted_iota(jnp.int32, (tq, tq), 0)
    qcol = lax.broadcasted_iota(jnp.int32, (tq, tq), 1)
    krow_f = krow.astype(F32)
    ramps = [slopes[hl] * krow_f for hl in range(2)]

    def scores(ki, s_ref):
        kt = kb_scr[pl.ds(pl.multiple_of(ki * tq, tq), tq), :]
        for hl in range(2):
            s_ref[hl] = _dot_nt(kt, qs_scr[hl])

    def softmax_pv(ki, s_ref, diagonal):
        off = ((ki - qi) * tq).astype(F32)
        pa = []
        for hl in range(2):
            sp = s_ref[hl] + ramps[hl]
            shift = slopes[hl] * off
            if diagonal:
                sp = jnp.where(krow <= qcol, sp, -jnp.inf)
            else:
                shift = shift + sel_scr[hl, pl.ds(ki, 1), :]
            pa.append(_online_stats_t(hl, sp, shift, m_scr))
        for hl in range(2):
            acc_scr[hl] = pa[hl][1] * acc_scr[hl] + _dot(vt_scr[hl, ki], pa[hl][0])

    scores(qi, sd_scr)
    scores(0, sa_scr)
    softmax_pv(qi, sd_scr, True)

    def body(i, carry):
        ki = 2 * i
        scores(ki + 1, sb_scr)
        softmax_pv(ki, sa_scr, False)
        scores(ki + 2, sa_scr)
        softmax_pv(ki + 1, sb_scr, False)
        return carry

    lax.fori_loop(0, qi // 2, body, 0)

    @pl.when(qi % 2 == 1)
    def _():
        softmax_pv(qi - 1, sa_scr, False)

    outs = []
    for hl in range(2):
        outs.append(acc_scr[hl, hl * HEAD_DIM:(hl + 1) * HEAD_DIM, :] * (1.0 / _denominator(acc_scr, hl, hl)))
    o_ref[0] = jnp.concatenate(outs, axis=0).T


def _moba_prompt(q_soft, k_rows, v_rows):
    n, t, _ = q_soft.shape
    tq = MOBA_BLOCK
    assert t % tq == 0 and t // tq <= LANES
    nb = t // tq
    nbp = -(-nb // 8) * 8
    kern = functools.partial(_moba_body, t=t, tq=tq, nb=nb)
    return pl.pallas_call(
        kern,
        grid=(n, 2, nb),
        in_specs=[pl.BlockSpec((1, tq, LANES), lambda b, p, i: (b, i, 2 + p)),
                  pl.BlockSpec((1, t, LANES), lambda b, p, i: (b, 0, 2 + p)),
                  pl.BlockSpec((1, t, LANES), lambda b, p, i: (b, 0, 2 + p))],
        out_specs=pl.BlockSpec((1, tq, LANES), lambda b, p, i: (b, i, p)),
        out_shape=jax.ShapeDtypeStruct((n, t, 2 * LANES), F32),
        scratch_shapes=[pltpu.VMEM((t, LANES), BF16), pltpu.VMEM((2, nb, LANES, tq), BF16),
                        pltpu.VMEM((nbp, LANES), F32),
                        pltpu.VMEM((2, tq, LANES), BF16), pltpu.VMEM((2, nbp, tq), F32),
                        pltpu.VMEM((2, tq, tq), F32), pltpu.VMEM((2, tq, tq), F32), pltpu.VMEM((2, tq, tq), F32),
                        pltpu.VMEM((2, 1, tq), F32),
                        pltpu.VMEM((2, LANES, tq), F32)],
        compiler_params=_cp(3),
        name="moba_prompt",
    )(q_soft, k_rows, v_rows)


def _bdiag(x, lane_lo):
    return jnp.concatenate([jnp.where(lane_lo, x, 0.0), jnp.where(lane_lo, 0.0, x)], axis=0)


def _gdn_prompt_body(qkv_ref, zg_ref, bx_ref, ax_ref, arow_ref, cw_ref, alogx_ref, dtbx_ref, alogr_ref, dtbr_ref,
                     gn_ref, bd_ref, ltri_ref, ubd_ref,
                     o_ref, sfin_ref, xbuf, s_scr, *, c, nseq):
    ci = pl.program_id(0)
    nc = pl.num_programs(0)

    @pl.when(ci == 0)
    def _():
        xbuf[:, 0:8, :] = jnp.zeros((nseq, 8, 3 * DN_W), F32)
        s_scr[...] = jnp.zeros(s_scr.shape, F32)

    ltri = ltri_ref[...]
    ubd = ubd_ref[...]
    bd = bd_ref[...]
    cw = cw_ref[...]
    row = lax.broadcasted_iota(jnp.int32, (c, LANES), 0)
    lane = lax.broadcasted_iota(jnp.int32, (c, LANES), 1)
    col = lane % HEAD_DIM
    lane_lo = lane < HEAD_DIM
    incl = row >= col
    strict = row > col
    eye2 = jnp.where(row == col, 1.0, 0.0)
    r128 = lax.broadcasted_iota(jnp.int32, (LANES, LANES), 0)
    c128 = lax.broadcasted_iota(jnp.int32, (LANES, LANES), 1)
    same_head = (r128 // HEAD_DIM) == (c128 // HEAD_DIM)
    steps = max(1, int(math.ceil(math.log2(c))) - 1)

    def split3_dot(x, lhs01=None, rhs01=None):
        x0, x1 = _split2(x)
        x2 = (x - x0.astype(F32) - x1.astype(F32)).astype(BF16)
        if lhs01 is not None:
            return _dot(lhs01, x0) + _dot(lhs01, x1) + _dot(lhs01, x2)
        return _dot(x0, rhs01) + _dot(x1, rhs01) + _dot(x2, rhs01)

    cq, beta_x, gc_x, gc_row = [], [], [], []
    for b in range(nseq):
        xbuf[b, 8:8 + c, :] = qkv_ref[b]
        y = (cw[3:4] * xbuf[b, 8:8 + c, :] + cw[2:3] * xbuf[b, 7:7 + c, :]
             + cw[1:2] * xbuf[b, 6:6 + c, :] + cw[0:1] * xbuf[b, 5:5 + c, :])
        xbuf[b, 0:8, :] = xbuf[b, c:c + 8, :]
        cq.append(_silu(y))
        beta_x.append(_sigmoid(bx_ref[b]))
        g_x = -jnp.exp(alogx_ref[...]) * _softplus(ax_ref[b] + dtbx_ref[...])
        g_row = -jnp.exp(alogr_ref[...]) * _softplus(arow_ref[b, 0] + dtbr_ref[...])
        gc_x.append(split3_dot(g_x, lhs01=ltri))
        gc_row.append(split3_dot(g_row, rhs01=ubd))

    chains = [(b, p) for b in range(nseq) for p in range(N_PAIRS)]
    pair = lambda arr, p, off=0: arr[:, off + p * LANES:off + (p + 1) * LANES]
    qp = [pair(cq[b], p) for b, p in chains]
    kp = [pair(cq[b], p, DN_W) for b, p in chains]
    vp = [pair(cq[b], p, 2 * DN_W) for b, p in chains]
    bexp = [pair(beta_x[b], p) for b, p in chains]
    gcx = [pair(gc_x[b], p) for b, p in chains]
    gcr = [gc_row[b][p:p + 1, :] for b, p in chains]
    n_ch = len(chains)
    rng = range(n_ch)

    ssq_q = [_mm2(qp[i] * qp[i], bd) for i in rng]
    ssq_k = [_mm2(kp[i] * kp[i], bd) for i in rng]
    qn = [qp[i] * lax.rsqrt(ssq_q[i] + EPS) * (HEAD_DIM ** -0.5) for i in rng]
    kn = [kp[i] * lax.rsqrt(ssq_k[i] + EPS) for i in rng]
    decay = [jnp.exp(jnp.where(incl, gcx[i] - gcr[i], -jnp.inf)) for i in rng]
    kq = [_mm3(jnp.concatenate([kn[i], qn[i]], axis=0), _bdiag(kn[i], lane_lo), _dot_nt) for i in rng]
    qk = [jnp.where(incl, kq[i][c:2 * c] * decay[i], 0.0) for i in rng]
    bpow = [-jnp.where(strict, bexp[i] * kq[i][0:c] * decay[i], 0.0) for i in rng]
    x = [eye2 + bpow[i] for i in rng]
    def mm3_parts(a, b_parts):
        ah, al = _split2(a)
        bh, bl = b_parts
        return _dot(ah, bh) + _dot(ah, bl) + _dot(al, bh)

    pow_bd = [_split2(_bdiag(bpow[i], lane_lo)) for i in rng]
    for _ in range(steps):
        bpow = [mm3_parts(bpow[i], pow_bd[i]) for i in rng]
        pow_bd = [_split2(_bdiag(bpow[i], lane_lo)) for i in rng]
        x = [x[i] + mm3_parts(x[i], pow_bd[i]) for i in rng]
    eg = [jnp.exp(gcx[i]) for i in rng]
    uw = [_mm3(x[i], jnp.concatenate([_bdiag(vp[i] * bexp[i], lane_lo), _bdiag(kn[i] * bexp[i] * eg[i], lane_lo)], axis=1))
          for i in rng]
    s_old = [s_scr[b, p] for b, p in chains]
    ws = [_mm3(jnp.concatenate([uw[i][:, LANES:2 * LANES], qn[i] * eg[i]], axis=0), s_old[i]) for i in rng]
    v_new = [uw[i][:, 0:LANES] - ws[i][0:c] for i in rng]
    glast = [gcx[i][c - 1:c, :] for i in rng]
    intra = [_mm3(qk[i], _bdiag(v_new[i], lane_lo)) for i in rng]
    upd = [_mm3(kn[i] * jnp.exp(glast[i] - gcx[i]), v_new[i], _dot_tn) for i in rng]
    o = [ws[i][c:2 * c] + intra[i] for i in rng]
    ms = [_mm2(o[i] * o[i], bd) * (1.0 / HEAD_DIM) for i in rng]
    for i, (b, p) in enumerate(chains):
        sl = slice(p * LANES, (p + 1) * LANES)
        s_scr[b, p] = s_old[i] * jnp.exp(glast[i]) + jnp.where(same_head, upd[i], 0.0)
        o_ref[b, :, sl] = o[i] * lax.rsqrt(ms[i] + EPS) * gn_ref[...] * _silu(zg_ref[b, :, sl])

    @pl.when(ci == nc - 1)
    def _():
        sfin_ref[...] = s_scr[...]


def _gdn_prompt(qkv, zg, b_x, a_x, ba, conv_w, a_log, dt_bias, norm_g, c=64):
    n, t, _ = qkv.shape
    nc = t // c
    a_raw = ba[..., N_DN_HEADS:2 * N_DN_HEADS]
    a_row = a_raw.reshape(n, nc, c, N_PAIRS, 2).transpose(0, 1, 3, 4, 2).reshape(n, nc, N_PAIRS, 2 * c)
    rep = lambda v: jnp.repeat(v.reshape(N_PAIRS, 2), c, axis=1)
    lanes = lambda v: jnp.repeat(v, HEAD_DIM).reshape(1, DN_W)
    i = np.arange(c)
    ltri = jnp.asarray((i[:, None] >= i[None, :]).astype(np.float32)).astype(BF16)
    j = np.arange(2 * c)
    ubd = jnp.asarray(((j[:, None] // c == j[None, :] // c) & (j[:, None] % c <= j[None, :] % c)).astype(np.float32))
    assert 2 * c == LANES
    kern = functools.partial(_gdn_prompt_body, c=c, nseq=n)
    tok = lambda w: pl.BlockSpec((n, c, w), lambda i: (0, i, 0))
    return pl.pallas_call(
        kern,
        grid=(nc,),
        in_specs=[tok(3 * DN_W), tok(DN_W), tok(DN_W), tok(DN_W),
                  pl.BlockSpec((n, 1, N_PAIRS, LANES), lambda i: (0, i, 0, 0)),
                  _const_spec((DN_CONV, 3 * DN_W)),
                  _const_spec((1, DN_W)), _const_spec((1, DN_W)),
                  _const_spec((N_PAIRS, LANES)), _const_spec((N_PAIRS, LANES)),
                  _const_spec((1, LANES)), _const_spec((LANES, LANES)),
                  _const_spec((c, c)), _const_spec((LANES, LANES))],
        out_specs=[tok(DN_W), _const_spec((n, N_PAIRS, LANES, LANES))],
        out_shape=[jax.ShapeDtypeStruct((n, t, DN_W), F32),
                   jax.ShapeDtypeStruct((n, N_PAIRS, LANES, LANES), F32)],
        scratch_shapes=[pltpu.VMEM((n, c + 8, 3 * DN_W), F32), pltpu.VMEM((n, N_PAIRS, LANES, LANES), F32)],
        compiler_params=_cp(1),
        name="gdn_prompt",
    )(qkv, zg, b_x, a_x, a_row, conv_w, lanes(a_log), lanes(dt_bias),
      rep(a_log), rep(dt_bias), jnp.tile(norm_g.reshape(1, HEAD_DIM), (1, 2)), _bd_ones(BF16), ltri, ubd.astype(BF16))


def _state_to_bd(s):
    n = s.shape[0]
    s = s.reshape(n, N_PAIRS, 2, HEAD_DIM, HEAD_DIM)
    z = jnp.zeros_like(s[:, :, 0])
    top = jnp.concatenate([s[:, :, 0], z], axis=-1)
    bot = jnp.concatenate([z, s[:, :, 1]], axis=-1)
    return jnp.concatenate([top, bot], axis=-2)


def _state_from_bd(sbd):
    n = sbd.shape[0]
    s = jnp.stack([sbd[:, :, :HEAD_DIM, :HEAD_DIM], sbd[:, :, HEAD_DIM:, HEAD_DIM:]], axis=2)
    return s.reshape(n, N_DN_HEADS, HEAD_DIM, HEAD_DIM)


def _gdn_decode_body(qkv_ref, cb_ref, zg_ref, bx_ref, ax_ref, cw_ref, alogx_ref, dtbx_ref, gn_ref, bd_ref,
                     s_ref, o_ref, snew_ref, cnew_ref):
    x = qkv_ref[...]
    buf = cb_ref[...]
    cw = cw_ref[...]
    y = cw[0:1] * buf[0:1] + cw[1:2] * buf[1:2] + cw[2:3] * buf[2:3] + cw[3:4] * x
    cnew_ref[...] = jnp.concatenate([buf[1:3], x], axis=0)
    cq = _silu(y)
    bexp = _sigmoid(bx_ref[...])
    eg = jnp.exp(-jnp.exp(alogx_ref[...]) * _softplus(ax_ref[...] + dtbx_ref[...]))
    bd = bd_ref[...]
    r128 = lax.broadcasted_iota(jnp.int32, (LANES, LANES), 0)
    c128 = lax.broadcasted_iota(jnp.int32, (LANES, LANES), 1)
    eye = r128 == c128
    rows8 = lambda v: jnp.broadcast_to(v, (8, v.shape[-1]))
    pr = range(N_PAIRS)
    sl = [slice(p * LANES, (p + 1) * LANES) for p in pr]
    qp = [cq[:, p * LANES:(p + 1) * LANES] for p in pr]
    kp = [cq[:, DN_W + p * LANES:DN_W + (p + 1) * LANES] for p in pr]
    vp = [cq[:, 2 * DN_W + p * LANES:2 * DN_W + (p + 1) * LANES] for p in pr]
    ssq_q = [_mm2(rows8(qp[p] * qp[p]), bd)[0:1] for p in pr]
    ssq_k = [_mm2(rows8(kp[p] * kp[p]), bd)[0:1] for p in pr]
    qn = [qp[p] * lax.rsqrt(ssq_q[p] + EPS) * (HEAD_DIM ** -0.5) for p in pr]
    kn = [kp[p] * lax.rsqrt(ssq_k[p] + EPS) for p in pr]
    s0 = [s_ref[p] * eg[:, sl[p]] for p in pr]
    ks = [_mm3(rows8(kn[p]), s0[p])[0:1] for p in pr]
    kcol = [_mm2(jnp.where(eye, jnp.broadcast_to(kn[p], (LANES, LANES)), 0.0), bd) for p in pr]
    s1 = [s0[p] + kcol[p] * ((vp[p] - ks[p]) * bexp[:, sl[p]]) for p in pr]
    o = [_mm3(rows8(qn[p]), s1[p])[0:1] for p in pr]
    ms = [_mm2(rows8(o[p] * o[p]), bd)[0:1] * (1.0 / HEAD_DIM) for p in pr]
    for p in pr:
        snew_ref[p] = s1[p]
        o_ref[:, sl[p]] = o[p] * lax.rsqrt(ms[p] + EPS) * gn_ref[...] * _silu(zg_ref[:, sl[p]])


def _gdn_decode(qkv, zg, b_x, a_x, conv_buf, s_bd, conv_w, a_log, dt_bias, norm_g):
    ns = qkv.shape[0]
    row = lambda w: pl.BlockSpec((None, 1, w), lambda i: (i, 0, 0))
    lanes = lambda v: jnp.repeat(v, HEAD_DIM).reshape(1, DN_W)
    return pl.pallas_call(
        _gdn_decode_body,
        grid=(ns,),
        in_specs=[row(3 * DN_W), pl.BlockSpec((None, DN_CONV - 1, 3 * DN_W), lambda i: (i, 0, 0)), row(DN_W),
                  row(DN_W), row(DN_W),
                  _const_spec((DN_CONV, 3 * DN_W)), _const_spec((1, DN_W)), _const_spec((1, DN_W)),
                  _const_spec((1, LANES)), _const_spec((LANES, LANES)),
                  pl.BlockSpec((None, N_PAIRS, LANES, LANES), lambda i: (i, 0, 0, 0))],
        out_specs=[row(DN_W), pl.BlockSpec((None, N_PAIRS, LANES, LANES), lambda i: (i, 0, 0, 0)),
                   pl.BlockSpec((None, DN_CONV - 1, 3 * DN_W), lambda i: (i, 0, 0))],
        out_shape=[jax.ShapeDtypeStruct((ns, 1, DN_W), F32),
                   jax.ShapeDtypeStruct((ns, N_PAIRS, LANES, LANES), F32),
                   jax.ShapeDtypeStruct((ns, DN_CONV - 1, 3 * DN_W), F32)],
        compiler_params=_cp(1),
        name="gdn_decode",
    )(qkv[:, None, :], conv_buf, zg[:, None, :], b_x[:, None, :], a_x[:, None, :], conv_w, lanes(a_log), lanes(dt_bias),
      jnp.tile(norm_g.reshape(1, HEAD_DIM), (1, 2)), _bd_ones(BF16), s_bd)


def _rms(x, g):
    return x * lax.rsqrt(jnp.mean(x * x, axis=-1, keepdims=True) + EPS) * g


def _ffn_prompt_body(x_ref, oa_ref, ob_ref, oc_ref, wo_ref, g2_ref, wg_ref, wu_ref, cw_ref, cb_ref, wd_ref, gf_ref,
                     y_ref, st_ref, acc, h2, gbuf, carry, *, tm, final):
    i = pl.program_id(1)
    j = pl.program_id(2)
    nj = pl.num_programs(2)

    @pl.when(j == 0)
    def _():
        x1 = (x_ref[0] + _dot(oa_ref[0].astype(BF16), wo_ref[0:256, :]) + _dot(ob_ref[0].astype(BF16), wo_ref[256:512, :])
              + _dot(oc_ref[0].astype(BF16), wo_ref[512:1024, :]))
        acc[...] = x1
        h2[...] = _rms(x1, g2_ref[...]).astype(BF16)

    @pl.when(i == 0)
    def _():
        carry[j] = jnp.zeros(carry.shape[1:], F32)

    h = h2[...]
    g = _dot(h, wg_ref[...])
    u = _dot(h, wu_ref[...])
    gbuf[0:8, :] = carry[j]
    gbuf[8:8 + tm, :] = g
    cw = cw_ref[...]
    gc = cw[2:3] * g + cw[1:2] * gbuf[7:7 + tm, :] + cw[0:1] * gbuf[6:6 + tm, :] + cb_ref[...]
    carry[j] = gbuf[tm:tm + 8, :]
    st_ref[0, j] = gbuf[tm + 6:tm + 8, :]
    act = (_silu(gc) * u).astype(BF16)
    acc[...] += _dot(act, wd_ref[...])

    @pl.when(j == nj - 1)
    def _():
        if final:
            y_ref[0] = _rms(acc[...], gf_ref[...])
        else:
            y_ref[0] = acc[...]


def _ffn_prompt(x, o_a, o_b, o_c, w_out, norm2_g, w_ffn_in, conv_w, conv_b, w_ffn_out, normf_g, final, tm=1024, tf=512):
    n, t, d = x.shape
    tm = min(tm, t)
    assert t % tm == 0
    nj = D_FF // tf
    kern = functools.partial(_ffn_prompt_body, tm=tm, final=final)
    rows = lambda w: pl.BlockSpec((1, tm, w), lambda b, i, j: (b, i, 0))
    y, st = pl.pallas_call(
        kern,
        grid=(n, t // tm, nj),
        in_specs=[rows(d), rows(256), rows(256), rows(512),
                  _const_spec((d, d)), _const_spec((1, d)),
                  pl.BlockSpec((d, tf), lambda b, i, j: (0, j)),
                  pl.BlockSpec((d, tf), lambda b, i, j: (0, j + nj)),
                  pl.BlockSpec((FFN_CONV, tf), lambda b, i, j: (0, j)),
                  pl.BlockSpec((1, tf), lambda b, i, j: (0, j)),
                  pl.BlockSpec((tf, d), lambda b, i, j: (j, 0)),
                  _const_spec((1, d))],
        out_specs=[rows(d), pl.BlockSpec((1, nj, FFN_CONV - 1, tf), lambda b, i, j: (b, 0, 0, 0))],
        out_shape=[jax.ShapeDtypeStruct((n, t, d), F32), jax.ShapeDtypeStruct((n, nj, FFN_CONV - 1, tf), F32)],
        scratch_shapes=[pltpu.VMEM((tm, d), F32), pltpu.VMEM((tm, d), BF16), pltpu.VMEM((tm + 8, tf), F32),
                        pltpu.VMEM((nj, 8, tf), F32)],
        compiler_params=_cp(3),
        name="ffn_prompt",
    )(x, o_a, o_b, o_c, w_out, norm2_g.reshape(1, d), w_ffn_in, w_ffn_in, conv_w, conv_b.reshape(1, D_FF),
      w_ffn_out, normf_g.reshape(1, d))
    return y, st.transpose(0, 2, 1, 3).reshape(n, FFN_CONV - 1, D_FF)


def _ffn_decode_body(x_ref, oa_ref, ob_ref, oc_ref, wo_ref, g2_ref, wg_ref, wu_ref, cw_ref, cb_ref, wd_ref, gf_ref,
                     prev_ref, y_ref, st_ref, acc, h2, *, final):
    j = pl.program_id(0)
    nj = pl.num_programs(0)

    @pl.when(j == 0)
    def _():
        x1 = (x_ref[...] + _dot(oa_ref[...].astype(BF16), wo_ref[0:256, :]) + _dot(ob_ref[...].astype(BF16), wo_ref[256:512, :])
              + _dot(oc_ref[...].astype(BF16), wo_ref[512:1024, :]))
        acc[...] = x1
        h2[...] = _rms(x1, g2_ref[...]).astype(BF16)

    h = h2[...]
    g = _dot(h, wg_ref[...])
    u = _dot(h, wu_ref[...])
    cw = cw_ref[...]
    gc = cw[2:3] * g + cw[1:2] * prev_ref[1] + cw[0:1] * prev_ref[0] + cb_ref[...]
    st_ref[0] = prev_ref[1]
    st_ref[1] = g
    act = (_silu(gc) * u).astype(BF16)
    acc[...] += _dot(act, wd_ref[...])

    @pl.when(j == nj - 1)
    def _():
        if final:
            y_ref[...] = _rms(acc[...], gf_ref[...])
        else:
            y_ref[...] = acc[...]


def _ffn_decode(x, o_a, o_b, o_c, prev, w_out, norm2_g, w_ffn_in, conv_w, conv_b, w_ffn_out, normf_g, final, tf=512):
    ns, d = x.shape
    nj = D_FF // tf
    kern = functools.partial(_ffn_decode_body, final=final)
    return pl.pallas_call(
        kern,
        grid=(nj,),
        in_specs=[_const_spec((ns, d)), _const_spec((ns, 256)), _const_spec((ns, 256)), _const_spec((ns, 512)),
                  _const_spec((d, d)), _const_spec((1, d)),
                  pl.BlockSpec((d, tf), lambda j: (0, j)),
                  pl.BlockSpec((d, tf), lambda j: (0, j + nj)),
                  pl.BlockSpec((FFN_CONV, tf), lambda j: (0, j)),
                  pl.BlockSpec((1, tf), lambda j: (0, j)),
                  pl.BlockSpec((tf, d), lambda j: (j, 0)),
                  _const_spec((1, d)),
                  pl.BlockSpec((FFN_CONV - 1, ns, tf), lambda j: (0, 0, j))],
        out_specs=[_const_spec((ns, d)), pl.BlockSpec((FFN_CONV - 1, ns, tf), lambda j: (0, 0, j))],
        out_shape=[jax.ShapeDtypeStruct((ns, d), F32), jax.ShapeDtypeStruct((FFN_CONV - 1, ns, D_FF), F32)],
        scratch_shapes=[pltpu.VMEM((ns, d), F32), pltpu.VMEM((ns, d), BF16)],
        compiler_params=_cp(1),
        name="ffn_decode",
    )(x, o_a, o_b, o_c, w_out, norm2_g.reshape(1, d), w_ffn_in, w_ffn_in, conv_w, conv_b.reshape(1, D_FF),
      w_ffn_out, normf_g.reshape(1, d), prev)


ROWS = 16
PAGES_PER_STEP = 32


def _decode_consts():
    j = np.arange(ROWS)[:, None]
    lane = np.arange(SOFT_W)[None, :]
    is_diff = j < 8
    is_moba = (j >= 8) & (j < 12)
    q_lo = np.where(is_diff, (j // 2) * HEAD_DIM + (j % 2) * DIFF_HALF, 256 + (j - 8) * HEAD_DIM)
    q_w = np.where(is_diff, DIFF_HALF, HEAD_DIM)
    qmask = (lane >= q_lo) & (lane < q_lo + q_w) & (is_diff | is_moba)
    qscale = np.where(is_diff, DIFF_HALF ** -0.5, HEAD_DIM ** -0.5) * LOG2E
    v_lo = np.where(is_diff, (j // 2) * HEAD_DIM, 256 + (j - 8) * HEAD_DIM)
    vmask = (lane >= v_lo) & (lane < v_lo + HEAD_DIM) & (is_diff | is_moba)
    head = np.where(is_diff, j // 2, j - 8)
    slope = np.where(is_diff, 2.0 ** -(2 * head + 1), np.where(is_moba, 2.0 ** -(2 * head + 2), 0.0)) * LOG2E
    coef_a = np.where((is_diff & (j % 2 == 0)) | is_moba, 1.0, 0.0)
    coef_b = np.where(is_diff & (j % 2 == 1), -1.0, 0.0)
    bc = lambda a: jnp.asarray(np.broadcast_to(a, (ROWS, LANES)).astype(np.float32))
    i = np.arange(SOFT_W)
    bd512 = (i[:, None] // HEAD_DIM == i[None, :] // HEAD_DIM).astype(np.float32)
    return (jnp.asarray((qmask * qscale).astype(np.float32)), jnp.asarray((qmask & is_moba).astype(np.float32)),
            jnp.asarray(vmask.astype(np.float32)), bc(slope), bc(coef_a), bc(coef_b), jnp.asarray(bd512))


def _decode_attn_body(pt_ref, lam_ref, q_ref, ks_ref, vs_ref, *rest, nblk, npg, page, past_len, out_scale):
    del pt_ref
    k_refs = rest[0:npg]
    v_refs = rest[npg:2 * npg]
    qmask_ref, gmask_ref, vmask_ref, slope_ref, ca_ref, cb_ref, bd_ref, g_ref, o_ref = rest[2 * npg:2 * npg + 9]
    m_scr, l_scr, acc_scr, mblk, lblk, gblk, accblk = rest[2 * npg + 9:]
    b = pl.program_id(1)
    nstep = pl.num_programs(1)
    tk = 2 * page

    @pl.when(b == 0)
    def _():
        m_scr[...] = jnp.full(m_scr.shape, -jnp.inf, F32)
        l_scr[...] = jnp.zeros(l_scr.shape, F32)
        acc_scr[...] = jnp.zeros(acc_scr.shape, F32)
        mblk[...] = jnp.full(mblk.shape, -jnp.inf, F32)
        lblk[...] = jnp.zeros(lblk.shape, F32)
        gblk[...] = jnp.full(gblk.shape, -jnp.inf, F32)

    qs = q_ref[...] * qmask_ref[...]
    qsb = qs.astype(BF16)
    qgb = (q_ref[...] * gmask_ref[...]).astype(BF16)
    slope = slope_ref[:, 0:1]
    lane = lax.broadcasted_iota(jnp.int32, (ROWS, LANES), 1)
    tok = lax.broadcasted_iota(jnp.int32, (1, tk), 1)

    nu = npg // 2
    raw = [jnp.concatenate([_dot(qsb, k_refs[2 * u][...].astype(BF16)), _dot(qsb, k_refs[2 * u + 1][...].astype(BF16))],
                           axis=1) for u in range(nu)]
    mbs, lbs, pes = [], [], []
    for u in range(nu):
        dist = (past_len - ((b * nu + u) * tk + tok)).astype(F32)
        s = raw[u] - slope * dist
        mb = jnp.max(s, axis=-1, keepdims=True)
        pexp = jnp.exp2(s - mb)
        mbs.append(mb)
        lbs.append(jnp.sum(pexp, axis=-1, keepdims=True))
        pes.append(pexp.astype(BF16))
    accbs = [_dot_nt(pes[u][:, 0:page], v_refs[2 * u][...].astype(BF16))
             + _dot_nt(pes[u][:, page:tk], v_refs[2 * u + 1][...].astype(BF16)) for u in range(nu)]
    kmeans = [jnp.sum(k_refs[2 * u][...] + k_refs[2 * u + 1][...], axis=-1, keepdims=True) * (1.0 / tk)
              for u in range(nu)]
    gates = [_dot(qgb, kmeans[u].astype(BF16)) for u in range(nu)]

    for u in range(nu):
        bb = b * nu + u
        mb, lb, accb = mbs[u], lbs[u], accbs[u]
        m_old = m_scr[:, 0:1]
        m_new = jnp.maximum(m_old, mb)
        a_old = jnp.exp2(m_old - m_new)
        a_blk = jnp.exp2(mb - m_new)
        l_scr[...] = jnp.broadcast_to(a_old * l_scr[:, 0:1] + a_blk * lb, l_scr.shape)
        acc_scr[...] = a_old * acc_scr[...] + a_blk * accb
        m_scr[...] = jnp.broadcast_to(m_new, m_scr.shape)
        hit = lane == bb
        mblk[...] = jnp.where(hit, mb, mblk[...])
        lblk[...] = jnp.where(hit, lb, lblk[...])
        gblk[...] = jnp.where(hit, gates[u], gblk[...])
        accblk[bb] = accb

    @pl.when(b == nstep - 1)
    def _():
        s_self = jnp.sum(qs * ks_ref[...], axis=-1, keepdims=True)
        vs = vs_ref[...]
        m_o = m_scr[:, 0:1]
        m_d = jnp.maximum(m_o, s_self)
        a_o = jnp.exp2(m_o - m_d)
        p_d = jnp.exp2(s_self - m_d)
        o_d = (a_o * acc_scr[...] + p_d * vs) / (a_o * l_scr[:, 0:1] + p_d)
        sel = _topk_select(gblk[...], lane, nblk, 1, LANES)
        mm = mblk[...]
        m_f = jnp.maximum(jnp.max(jnp.where(sel, mm, -jnp.inf), axis=-1, keepdims=True), s_self)
        wgt = jnp.where(sel, jnp.exp2(mm - m_f), 0.0)
        p_m = jnp.exp2(s_self - m_f)
        l_m = jnp.sum(wgt * lblk[...], axis=-1, keepdims=True) + p_m
        acc_m = p_m * vs
        for blk in range(nblk):
            acc_m = acc_m + wgt[:, blk:blk + 1] * accblk[blk]
        o_m = acc_m / l_m
        rowi = lax.broadcasted_iota(jnp.int32, (ROWS, SOFT_W), 0)
        o_all = jnp.where(rowi < 8, o_d, o_m)
        coef = ca_ref[:, 0:1] + lam_ref[0] * cb_ref[:, 0:1]
        o_row = jnp.sum(coef * vmask_ref[...] * o_all, axis=0, keepdims=True)
        ms = _dot(jnp.broadcast_to(o_row * o_row, (8, SOFT_W)), bd_ref[...], HI)[0:1] * (1.0 / HEAD_DIM)
        o_norm = o_row * lax.rsqrt(ms + EPS) * g_ref[...] * out_scale
        lane5 = lax.broadcasted_iota(jnp.int32, (1, SOFT_W), 1)
        o_ref[...] = jnp.where(lane5 < 256, o_norm, o_row)


def _decode_attn(layer, q_soft, k_self, v_self, cache_k, cache_v, page_table, lam, subln_g, lam_init):
    ns = q_soft.shape[0]
    n_pages = page_table.shape[1]
    page = cache_k.shape[3]
    npg = PAGES_PER_STEP
    assert 2 * page == MOBA_BLOCK and n_pages % npg == 0 and cache_k.shape[2] == SOFT_W
    nblk = n_pages // 2
    assert nblk <= LANES
    kern = functools.partial(_decode_attn_body, nblk=nblk, npg=npg, page=page, past_len=n_pages * page,
                             out_scale=1.0 - lam_init)
    row = pl.BlockSpec((None, 1, SOFT_W), lambda s, b, pt: (s, 0, 0))
    pg = lambda off: pl.BlockSpec((None, None, SOFT_W, page), lambda s, b, pt: (layer, pt[s, npg * b + off], 0, 0))
    cst = lambda shp: pl.BlockSpec(shp, lambda s, b, pt: (0,) * len(shp))
    g4 = jnp.concatenate([jnp.tile(subln_g.reshape(1, HEAD_DIM), (1, 4)), jnp.ones((1, 256), F32)], axis=1)
    grid_spec = pltpu.PrefetchScalarGridSpec(
        num_scalar_prefetch=1,
        grid=(ns, n_pages // npg),
        in_specs=([pl.BlockSpec(memory_space=pltpu.SMEM), row, row, row]
                  + [pg(u) for u in range(npg)] + [pg(u) for u in range(npg)]
                  + [cst((ROWS, SOFT_W)), cst((ROWS, SOFT_W)), cst((ROWS, SOFT_W)), cst((ROWS, LANES)), cst((ROWS, LANES)),
                     cst((ROWS, LANES)), cst((SOFT_W, SOFT_W)), cst((1, SOFT_W))]),
        out_specs=row,
        scratch_shapes=[pltpu.VMEM((ROWS, LANES), F32), pltpu.VMEM((ROWS, LANES), F32), pltpu.VMEM((ROWS, SOFT_W), F32),
                        pltpu.VMEM((ROWS, LANES), F32), pltpu.VMEM((ROWS, LANES), F32), pltpu.VMEM((ROWS, LANES), F32),
                        pltpu.VMEM((nblk, ROWS, SOFT_W), F32)],
    )
    out = pl.pallas_call(
        kern,
        grid_spec=grid_spec,
        out_shape=jax.ShapeDtypeStruct((ns, 1, SOFT_W), F32),
        compiler_params=_cp(2),
        name="decode_attn",
    )(page_table, lam.reshape(1), q_soft[:, None, :], k_self[:, None, :], v_self[:, None, :],
      *([cache_k] * npg), *([cache_v] * npg), *_decode_consts(), g4)
    return out[:, 0, :]


def _pages_transposed(cache):
    d, n_pool, page = cache.shape[0:3]
    return jnp.transpose(cache, (0, 1, 3, 4, 2)).reshape(d, n_pool, SOFT_W, page)


def _lam(l, lam_q1, lam_k1, lam_q2, lam_k2):
    lam_init = 0.8 - 0.6 * math.exp(-0.3 * l)
    lam = (jnp.exp(jnp.sum(lam_q1[l] * lam_k1[l])) - jnp.exp(jnp.sum(lam_q2[l] * lam_k2[l])) + lam_init)
    return lam.astype(F32), lam_init


def _prompt_layer(x, w, lam, lam_init, final):
    n, t, d = x.shape
    outs = _proj_in(x.reshape(n * t, d), w["norm1"], w["w_in"], seq_len=t)
    qs, kr, vr, qkv, zg, b_x, a_x, ba = [a.reshape(n, t, a.shape[-1]) for a in outs[:len(_PROJ_WIDTHS)]]
    rows_t = lambda a: jnp.transpose(a.reshape(n, N_SOFT_HEADS, HEAD_DIM, t), (0, 3, 1, 2))
    o_a = _diff_attn_prompt(qs, kr, vr, lam, w["subln"], lam_init)
    o_b = _moba_prompt(qs, kr, vr)
    o_c, s_bd = _gdn_prompt(qkv, zg, b_x, a_x, ba, w["dn_conv_w"], w["dn_a_log"], w["dn_dt_bias"], w["dn_norm"])
    y, ffn_new = _ffn_prompt(x, o_a, o_b, o_c, w["w_out"], w["norm2"], w["ffn_in"], w["ffn_conv_w"], w["ffn_conv_b"],
                             w["ffn_out"], w["normf"], final)
    rows = (rows_t(outs[-2]), rows_t(outs[-1]), _state_from_bd(s_bd),
            qkv[:, t - (DN_CONV - 1):, :], ffn_new)
    return y, rows


def _sample_layer(l, x, w, lam, lam_init, final, cache_k, cache_v, page_table, state_dn, conv_qkv, conv_ffn):
    ns, d = x.shape
    qs, kr, vr, qkv, zg, b_x, a_x, _ = _proj_in(x, w["norm1"], w["w_in"])
    o_ab = _decode_attn(l, qs, kr, vr, cache_k, cache_v, page_table, lam, w["subln"], lam_init)
    o_c, s_bd, conv_new = _gdn_decode(qkv, zg, b_x, a_x, conv_qkv, _state_to_bd(state_dn), w["dn_conv_w"], w["dn_a_log"],
                                      w["dn_dt_bias"], w["dn_norm"])
    y, ffn_new = _ffn_decode(x, o_ab[:, 0:256], o_ab[:, 256:512], o_c[:, 0, :], jnp.swapaxes(conv_ffn, 0, 1),
                             w["w_out"], w["norm2"], w["ffn_in"], w["ffn_conv_w"], w["ffn_conv_b"], w["ffn_out"],
                             w["normf"], final)
    rows = (kr.reshape(ns, 1, 8, HEAD_DIM), vr.reshape(ns, 1, 8, HEAD_DIM), _state_from_bd(s_bd), conv_new,
            jnp.swapaxes(ffn_new, 0, 1))
    return y, rows


def kernel(x_prompt, x_sample, cache_k, cache_v, page_table, state_dn, state_conv_qkv, state_conv_ffn, norm1_g, norm2_g, normf_g, w_in, w_out, lam_q1, lam_k1, lam_q2, lam_k2, subln_g, dn_conv_w, dn_a_log, dn_dt_bias, dn_norm_g, ffn_w_in, ffn_conv_w, ffn_conv_b, ffn_w_out):
    depth = w_in.shape[0]
    cache_k = _pages_transposed(cache_k)
    cache_v = _pages_transposed(cache_v)
    xp = x_prompt
    xs = x_sample.reshape(x_sample.shape[0], x_sample.shape[2])
    rows_p, rows_s = [], []
    for l in range(depth):
        w = {"norm1": norm1_g[l], "w_in": _permute_w_in(w_in[l]), "subln": subln_g[l], "dn_conv_w": dn_conv_w[l],
             "dn_a_log": dn_a_log[l], "dn_dt_bias": dn_dt_bias[l], "dn_norm": dn_norm_g[l],
             "w_out": w_out[l].astype(BF16), "norm2": norm2_g[l], "ffn_in": ffn_w_in[l].astype(BF16),
             "ffn_conv_w": ffn_conv_w[l], "ffn_conv_b": ffn_conv_b[l], "ffn_out": ffn_w_out[l].astype(BF16),
             "normf": normf_g}
        lam, lam_init = _lam(l, lam_q1, lam_k1, lam_q2, lam_k2)
        final = l == depth - 1
        xp, rp = _prompt_layer(xp, w, lam, lam_init, final)
        xs, rs = _sample_layer(l, xs, w, lam, lam_init, final, cache_k, cache_v, page_table, state_dn[l],
                               state_conv_qkv[l], state_conv_ffn[l])
        rows_p.append(rp)
        rows_s.append(rs)
    stack = lambda rows, i: jnp.stack([r[i] for r in rows], axis=0)
    y_sample = xs.reshape(x_sample.shape)
    return (xp, y_sample,
            stack(rows_p, 0), stack(rows_p, 1), stack(rows_p, 2), stack(rows_p, 3), stack(rows_p, 4),
            stack(rows_s, 0), stack(rows_s, 1), stack(rows_s, 2), stack(rows_s, 3), stack(rows_s, 4))
```
